```python
import math
import jax, jax.numpy as jnp
from jax import lax
import numpy as np

D_MODEL = 1024
BATCH = 32
SEQ = 2048
DEPTH = 1

N_HEADS = 16
N_KV_HEADS = 2
GROUP = N_HEADS // N_KV_HEADS
HEAD_DIM = 64
WINDOW = 128
BLOCK = 128
ATTN_SCALE = 1.0 / math.sqrt(HEAD_DIM)
NEG_INF = -1e30
POOL_WINDOWS = (2, 4, 8, 16)
N_POOL_GROUPS = len(POOL_WINDOWS)
POOL_GROUP = 128
POOL_W = N_POOL_GROUPS * POOL_GROUP
Q_W = N_HEADS * HEAD_DIM
KV_W = N_KV_HEADS * HEAD_DIM
IN_W = Q_W + 2 * KV_W + POOL_W + 2 * D_MODEL
SPLITS = (Q_W, Q_W + KV_W, Q_W + 2 * KV_W, Q_W + 2 * KV_W + POOL_W, Q_W + 2 * KV_W + POOL_W + D_MODEL)
PEER_HEADS = 8
N_KEYS = 128
N_EXPERTS = N_KEYS * N_KEYS
D_KEY = 256
D_HALF = D_KEY // 2
PEER_TOPK = 16
PEER_CHUNK = 128
EPS = 1e-5

kernel_name = 'hybrid_swa_pool_peer_block'


def rmsnorm(x, g):
    xf = x.astype(jnp.float32)
    y = xf * lax.rsqrt(jnp.mean(xf * xf, axis=-1, keepdims=True) + EPS)
    return (y * g.astype(jnp.float32)).astype(x.dtype)


def sliding_window_attention(q, k, v, sinks):
    b, s = q.shape[0], q.shape[1]
    nblk = s // BLOCK
    kp = jnp.pad(k, ((0, 0), (BLOCK, 0), (0, 0), (0, 0)))
    vp = jnp.pad(v, ((0, 0), (BLOCK, 0), (0, 0), (0, 0)))
    qb = q.reshape(b, nblk, BLOCK, N_KV_HEADS, GROUP, HEAD_DIM).swapaxes(0, 1)
    sink = sinks.astype(jnp.float32).reshape(1, N_KV_HEADS, GROUP, 1, 1)

    def one_block(args):
        i, qi = args
        start = i * BLOCK
        kb = lax.dynamic_slice_in_dim(kp, start, 2 * BLOCK, axis=1)
        vb = lax.dynamic_slice_in_dim(vp, start, 2 * BLOCK, axis=1)
        sc = jnp.einsum('bqhgd,bkhd->bhgqk', qi, kb).astype(jnp.float32) * ATTN_SCALE
        qpos = start + jnp.arange(BLOCK)
        kpos = start - BLOCK + jnp.arange(2 * BLOCK)
        rel = qpos[:, None] - kpos[None, :]
        mask = (rel >= 0) & (rel < WINDOW) & (kpos[None, :] >= 0)
        sc = jnp.where(mask, sc, NEG_INF)
        sc = jnp.concatenate([sc, jnp.broadcast_to(sink, sc.shape[:-1] + (1,))], axis=-1)
        p = jax.nn.softmax(sc, axis=-1)[..., :-1]
        return jnp.einsum('bhgqk,bkhd->bqhgd', p.astype(vb.dtype), vb)

    o = lax.map(one_block, (jnp.arange(nblk), qb))
    return o.swapaxes(0, 1).reshape(b, s, Q_W)


def multiscale_pool(p, w_grp, scale):
    b, s = p.shape[0], p.shape[1]
    pf = p.astype(jnp.float32).reshape(b, s, N_POOL_GROUPS, POOL_GROUP)
    c = jnp.cumsum(pf, axis=1)
    t1 = jnp.arange(1, s + 1, dtype=jnp.float32)
    outs = []
    for j, w in enumerate(POOL_WINDOWS):
        cj = c[:, :, j]
        lag = jnp.pad(cj, ((0, 0), (w, 0), (0, 0)))[:, :s]
        cnt = jnp.minimum(t1, float(w))[None, :, None]
        outs.append((cj - lag) / cnt - pf[:, :, j])
    pooled = jnp.stack(outs, axis=2).astype(p.dtype)
    y = jnp.einsum('bsgc,gcd->bsgd', pooled, w_grp).reshape(b, s, POOL_W)
    return y * scale


def peer(xn, w_query, sub_keys, u_tab, v_tab):
    b, s, d = xn.shape
    xt = xn.reshape((b * s) // PEER_CHUNK, PEER_CHUNK, d)

    def chunk(xc):
        c = xc.shape[0]
        q = (xc @ w_query).reshape(c, PEER_HEADS, 2, D_HALF)
        sc = jnp.einsum('chpd,hpkd->chpk', q, sub_keys).astype(jnp.float32)
        sv, si = lax.top_k(sc, PEER_TOPK)
        cand = (sv[:, :, 0, :, None] + sv[:, :, 1, None, :]).reshape(c, PEER_HEADS, PEER_TOPK * PEER_TOPK)
        cidx = (si[:, :, 0, :, None] * N_KEYS + si[:, :, 1, None, :]).reshape(c, PEER_HEADS, PEER_TOPK * PEER_TOPK)
        fv, fi = lax.top_k(cand, PEER_TOPK)
        idx = jnp.take_along_axis(cidx, fi, axis=-1)
        g = jax.nn.softmax(fv, axis=-1)
        u = u_tab[idx]
        a = jax.nn.gelu(jnp.einsum('chkd,cd->chk', u, xc).astype(jnp.float32), approximate=False)
        vv = v_tab[idx]
        return jnp.einsum('chk,chkd->cd', (g * a).astype(vv.dtype), vv)

    return lax.map(chunk, xt).reshape(b, s, d)


def setup_inputs(seed: int = 0) -> dict:
    key = jax.random.key(seed)
    ks = jax.random.split(key, 16)
    f32 = jnp.float32
    L = DEPTH
    nrm = lambda k, shape, std: jax.random.normal(k, shape, f32) * std
    return {
        'x': jax.random.normal(ks[0], (BATCH, SEQ, D_MODEL), f32),
        'ln_mix_g': 1.0 + nrm(ks[1], (L, D_MODEL), 0.05),
        'w_in': nrm(ks[2], (L, D_MODEL, IN_W), D_MODEL ** -0.5),
        'b_in': nrm(ks[3], (L, IN_W), 0.02),
        'attn_sinks': nrm(ks[4], (L, N_HEADS), 0.5),
        'w_attn_up': nrm(ks[5], (L, Q_W, D_MODEL), Q_W ** -0.5),
        'w_pool_grp': nrm(ks[6], (L, N_POOL_GROUPS, POOL_GROUP, POOL_GROUP), POOL_GROUP ** -0.5),
        'pool_scale': 1.0 + nrm(ks[7], (L, POOL_W), 0.1),
        'w_pool_up': nrm(ks[8], (L, POOL_W, D_MODEL), POOL_W ** -0.5),
        'w_o': nrm(ks[9], (L, D_MODEL, D_MODEL), D_MODEL ** -0.5),
        'ln_ffn_g': 1.0 + nrm(ks[10], (L, D_MODEL), 0.05),
        'w_query': nrm(ks[11], (L, D_MODEL, PEER_HEADS * D_KEY), D_MODEL ** -0.5),
        'sub_keys': nrm(ks[12], (L, PEER_HEADS, 2, N_KEYS, D_HALF), D_HALF ** -0.5),
        'u_experts': nrm(ks[13], (L, N_EXPERTS, D_MODEL), D_MODEL ** -0.5),
        'v_experts': nrm(ks[14], (L, N_EXPERTS, D_MODEL), PEER_HEADS ** -0.5),
        'ln_final_g': 1.0 + nrm(ks[15], (D_MODEL,), 0.05),
    }


def reference(x, ln_mix_g, w_in, b_in, attn_sinks, w_attn_up, w_pool_grp, pool_scale, w_pool_up, w_o,
              ln_ffn_g, w_query, sub_keys, u_experts, v_experts, ln_final_g):
    b, s, _ = x.shape
    h = x
    for l in range(DEPTH):
        xn = rmsnorm(h, ln_mix_g[l])
        z = xn @ w_in[l] + b_in[l]
        q, k, v, pz, ga, gp = jnp.split(z, SPLITS, axis=-1)
        q = q.reshape(b, s, N_KV_HEADS, GROUP, HEAD_DIM)
        k = k.reshape(b, s, N_KV_HEADS, HEAD_DIM)
        v = v.reshape(b, s, N_KV_HEADS, HEAD_DIM)
        y_a = sliding_window_attention(q, k, v, attn_sinks[l]) @ w_attn_up[l]
        y_p = multiscale_pool(pz, w_pool_grp[l], pool_scale[l]) @ w_pool_up[l]
        merged = jax.nn.sigmoid(ga) * y_a + jax.nn.sigmoid(gp) * y_p
        h = h + merged @ w_o[l]
        h = h + peer(rmsnorm(h, ln_ffn_g[l]), w_query[l], sub_keys[l], u_experts[l], v_experts[l])
    return rmsnorm(h, ln_final_g)
```

```python
import functools
import math

import jax
import jax.numpy as jnp
from jax import lax
from jax.experimental import pallas as pl
from jax.experimental.pallas import tpu as pltpu
from jax.experimental.pallas import tpu_sc as plsc

F32 = jnp.float32
BF16 = jnp.bfloat16
I32 = jnp.int32

D_MODEL = 1024
N_HEADS = 16
N_KV_HEADS = 2
GROUP = N_HEADS // N_KV_HEADS
HEAD_DIM = 64
BLOCK = 128
ATTN_SCALE = 1.0 / math.sqrt(HEAD_DIM)
NEG_INF = -1e30
POOL_WINDOWS = (2, 4, 8, 16)
POOL_GROUP = 128
POOL_W = len(POOL_WINDOWS) * POOL_GROUP
POOL_HIST = max(POOL_WINDOWS)
Q_W = N_HEADS * HEAD_DIM
PEER_HEADS = 8
N_KEYS = 128
D_HALF = 128
PEER_TOPK = 16
PEER_SEL = PEER_HEADS * PEER_TOPK
EPS = 1e-5

VMEM_LIMIT_BYTES = 56 * 1024 * 1024

C_Q = 0
C_K = C_Q + Q_W
C_V = C_K + 2 * BLOCK
C_P = C_V + 2 * BLOCK
C_G = C_P + POOL_W
C_END = C_G + 2 * D_MODEL
KV_COLS = C_P - C_K

TM_PROJ = 512
TS_MERGE = 512
TC_ROUTE = 256
TM_ELEM = 1024


def _tc_params(*sem):
    return pltpu.CompilerParams(dimension_semantics=sem, vmem_limit_bytes=VMEM_LIMIT_BYTES)


def _rms_scale(x):
    return lax.rsqrt(jnp.mean(x * x, axis=-1, keepdims=True) + EPS)


def _inproj_kernel(x_ref, g_ref, w_ref, b_ref, q_ref, kv_ref, pz_ref, gate_ref):
    x = x_ref[...]
    xn = (x * _rms_scale(x) * g_ref[...]).astype(BF16)

    def proj(lo, hi):
        return jnp.dot(xn, w_ref[:, lo:hi], preferred_element_type=F32) + b_ref[:, lo:hi]

    q_ref[...] = proj(C_Q, C_K).astype(BF16)
    kv_ref[...] = proj(C_K, C_P).astype(BF16)
    pz_ref[...] = proj(C_P, C_G)
    gate_ref[...] = proj(C_G, C_END).astype(BF16)


def _inproj(x2, g, w, b):
    t = x2.shape[0]
    row = lambda i: (i, 0)
    fixed = lambda i: (0, 0)
    return pl.pallas_call(
        _inproj_kernel,
        grid=(t // TM_PROJ,),
        in_specs=[
            pl.BlockSpec((TM_PROJ, D_MODEL), row),
            pl.BlockSpec((1, D_MODEL), fixed),
            pl.BlockSpec((D_MODEL, C_END), fixed),
            pl.BlockSpec((1, C_END), fixed),
        ],
        out_specs=[
            pl.BlockSpec((TM_PROJ, Q_W), row),
            pl.BlockSpec((TM_PROJ, KV_COLS), row),
            pl.BlockSpec((TM_PROJ, POOL_W), row),
            pl.BlockSpec((TM_PROJ, 2 * D_MODEL), row),
        ],
        out_shape=[
            jax.ShapeDtypeStruct((t, Q_W), BF16),
            jax.ShapeDtypeStruct((t, KV_COLS), BF16),
            jax.ShapeDtypeStruct((t, POOL_W), F32),
            jax.ShapeDtypeStruct((t, 2 * D_MODEL), BF16),
        ],
        compiler_params=_tc_params("parallel"),
        name="inproj",
    )(x2, g, w, b)


def _attn_kernel(sink_ref, q_ref, kvc_ref, kvp_ref, o_ref):
    j = pl.program_id(1)
    kvc = kvc_ref[0]
    kvp = kvp_ref[0]
    row = lax.broadcasted_iota(I32, (BLOCK, BLOCK), 0)
    col = lax.broadcasted_iota(I32, (BLOCK, BLOCK), 1)
    mask = jnp.concatenate([jnp.logical_and(col > row, j > 0), col <= row], axis=1)
    low = lax.broadcasted_iota(I32, (1, BLOCK), 1) < HEAD_DIM
    zero = jnp.zeros((), BF16)
    for hkv in range(N_KV_HEADS):
        kc = slice(hkv * BLOCK, (hkv + 1) * BLOCK)
        vc = slice(2 * BLOCK + hkv * BLOCK, 2 * BLOCK + (hkv + 1) * BLOCK)
        k2 = jnp.concatenate([kvp[:, kc], kvc[:, kc]], axis=0)
        v2 = jnp.concatenate([kvp[:, vc], kvc[:, vc]], axis=0)
        v_half = (jnp.where(low, v2, zero), jnp.where(low, zero, v2))
        for pair in range(GROUP // 2):
            c0 = (hkv * (GROUP // 2) + pair) * BLOCK
            q2 = q_ref[0, :, c0:c0 + BLOCK]
            o2 = jnp.zeros((BLOCK, BLOCK), F32)
            for half in range(2):
                head = hkv * GROUP + pair * 2 + half
                qm = jnp.where(low, q2, zero) if half == 0 else jnp.where(low, zero, q2)
                s = lax.dot_general(qm, k2, (((1,), (1,)), ((), ())), preferred_element_type=F32)
                s = jnp.where(mask, s * ATTN_SCALE, NEG_INF)
                sink = sink_ref[head]
                m = jnp.maximum(jnp.max(s, axis=-1, keepdims=True), sink)
                e = jnp.exp(s - m)
                denom = jnp.sum(e, axis=-1, keepdims=True) + jnp.exp(sink - m)
                p = (e * (1.0 / denom)).astype(BF16)
                o2 = o2 + jnp.dot(p, v_half[half], preferred_element_type=F32)
            o_ref[0, :, c0:c0 + BLOCK] = o2.astype(BF16)


def _attention(sinks, q3, kv3):
    b, s, _ = q3.shape
    return pl.pallas_call(
        _attn_kernel,
        grid=(b, s // BLOCK),
        in_specs=[
            pl.BlockSpec(memory_space=pltpu.SMEM),
            pl.BlockSpec((1, BLOCK, Q_W), lambda i, j: (i, j, 0)),
            pl.BlockSpec((1, BLOCK, KV_COLS), lambda i, j: (i, j, 0)),
            pl.BlockSpec((1, BLOCK, KV_COLS), lambda i, j: (i, jnp.maximum(j - 1, 0), 0)),
        ],
        out_specs=pl.BlockSpec((1, BLOCK, Q_W), lambda i, j: (i, j, 0)),
        out_shape=jax.ShapeDtypeStruct((b, s, Q_W), BF16),
        compiler_params=_tc_params("parallel", "arbitrary"),
        name="attention",
    )(sinks, q3, kv3, kv3)


def _sigmoid(x):
    return 1.0 / (1.0 + jnp.exp(-x))


def _merge_kernel(o_ref, gate_ref, x_ref, pz_ref, pzp_ref, wau_ref, wgrp_ref, scale_ref, wpu_ref,
                  wo_ref, g_ref, h_ref, xn_ref):
    j = pl.program_id(1)
    ts = pz_ref.shape[1]
    prev = jnp.where(j > 0, pzp_ref[0], 0.0)
    ext = jnp.concatenate([prev, pz_ref[0]], axis=0)
    t1 = (j * ts + 1 + lax.broadcasted_iota(I32, (ts, 1), 0)).astype(F32)
    ys = []
    for g, w in enumerate(POOL_WINDOWS):
        e = ext[:, g * POOL_GROUP:(g + 1) * POOL_GROUP]
        tsum = e
        span = 1
        while span < w:
            tsum = tsum + pltpu.roll(tsum, span, 0)
            span *= 2
        pooled = tsum[POOL_HIST:] / jnp.minimum(t1, float(w)) - e[POOL_HIST:]
        y = jnp.dot(pooled.astype(BF16), wgrp_ref[g], preferred_element_type=F32)
        ys.append((y * scale_ref[:, g * POOL_GROUP:(g + 1) * POOL_GROUP]).astype(BF16))
    y_p = jnp.dot(jnp.concatenate(ys, axis=1), wpu_ref[...], preferred_element_type=F32)
    y_a = jnp.dot(o_ref[0], wau_ref[...], preferred_element_type=F32)
    gate = gate_ref[0]
    merged = (_sigmoid(gate[:, :D_MODEL].astype(F32)) * y_a
              + _sigmoid(gate[:, D_MODEL:].astype(F32)) * y_p)
    h = x_ref[0] + jnp.dot(merged.astype(BF16), wo_ref[...], preferred_element_type=F32)
    h_ref[0] = h
    xn_ref[0] = h * _rms_scale(h) * g_ref[...]


def _merge(o3, gate3, x, pz3, wau, wgrp, scale, wpu, wo, g):
    b, s, _ = x.shape
    ts = TS_MERGE
    hist_blocks = ts // POOL_HIST
    tile = lambda i, j: (i, j, 0)
    fixed2 = lambda i, j: (0, 0)
    return pl.pallas_call(
        _merge_kernel,
        grid=(b, s // ts),
        in_specs=[
            pl.BlockSpec((1, ts, Q_W), tile),
            pl.BlockSpec((1, ts, 2 * D_MODEL), tile),
            pl.BlockSpec((1, ts, D_MODEL), tile),
            pl.BlockSpec((1, ts, POOL_W), tile),
            pl.BlockSpec((1, POOL_HIST, POOL_W), lambda i, j: (i, jnp.maximum(j * hist_blocks - 1, 0), 0)),
            pl.BlockSpec((Q_W, D_MODEL), fixed2),
            pl.BlockSpec((len(POOL_WINDOWS), POOL_GROUP, POOL_GROUP), lambda i, j: (0, 0, 0)),
            pl.BlockSpec((1, POOL_W), fixed2),
            pl.BlockSpec((POOL_W, D_MODEL), fixed2),
            pl.BlockSpec((D_MODEL, D_MODEL), fixed2),
            pl.BlockSpec((1, D_MODEL), fixed2),
        ],
        out_specs=[pl.BlockSpec((1, ts, D_MODEL), tile), pl.BlockSpec((1, ts, D_MODEL), tile)],
        out_shape=[jax.ShapeDtypeStruct((b, s, D_MODEL), F32), jax.ShapeDtypeStruct((b, s, D_MODEL), F32)],
        compiler_params=_tc_params("parallel", "arbitrary"),
        name="merge",
    )(o3, gate3, x, pz3, pz3, wau, wgrp, scale, wpu, wo, g)


def _top16_rows(sc, pos):
    n = sc.shape[0]
    vals, rows = [], []
    for _ in range(PEER_TOPK):
        m = jnp.max(sc, axis=0, keepdims=True)
        am = jnp.min(jnp.where(sc == m, pos, n), axis=0, keepdims=True)
        vals.append(m)
        rows.append(am)
        sc = jnp.where(pos == am, -jnp.inf, sc)
    return jnp.concatenate(vals, axis=0), jnp.concatenate(rows, axis=0)


def _select_row(table, sel, pos16):
    out = []
    for k in range(PEER_TOPK):
        out.append(jnp.sum(jnp.where(pos16 == sel[k:k + 1], table, 0), axis=0, keepdims=True))
    return jnp.concatenate(out, axis=0)


def _route_kernel(xn_ref, wq_ref, keys_ref, idx_ref, gate_ref, idx_scr, gate_scr):
    c = xn_ref.shape[0]
    xb = xn_ref[...].astype(BF16)
    pos_keys = lax.broadcasted_iota(I32, (N_KEYS, c), 0)
    pos_cand = lax.broadcasted_iota(I32, (PEER_TOPK * PEER_TOPK, c), 0)
    pos16 = lax.broadcasted_iota(I32, (PEER_TOPK, c), 0)

    def head_body(h, carry):
        sv, si = [], []
        for half in range(2):
            hp = h * 2 + half
            wq = wq_ref[pl.ds(pl.multiple_of(hp * D_HALF, D_HALF), D_HALF), :]
            q_t = lax.dot_general(wq, xb, (((1,), (1,)), ((), ())), preferred_element_type=F32)
            sc = jnp.dot(keys_ref[hp], q_t.astype(BF16), preferred_element_type=F32)
            v, i = _top16_rows(sc, pos_keys)
            sv.append(v)
            si.append(i)
        cand = jnp.concatenate([sv[0][a:a + 1] + sv[1] for a in range(PEER_TOPK)], axis=0)
        fv, fpos = _top16_rows(cand, pos_cand)
        i0 = _select_row(si[0], fpos >> 4, pos16)
        i1 = _select_row(si[1], fpos & (PEER_TOPK - 1), pos16)
        e = jnp.exp(fv - fv[0:1])
        rows = pl.ds(pl.multiple_of(h * PEER_TOPK, PEER_TOPK), PEER_TOPK)
        idx_scr[rows, :] = i0 * N_KEYS + i1
        gate_scr[rows, :] = e / jnp.sum(e, axis=0, keepdims=True)
        return carry

    lax.fori_loop(0, PEER_HEADS, head_body, 0)
    idx_ref[...] = idx_scr[...].T
    gate_ref[...] = gate_scr[...].T


def _route(xn2, wq_t, keys):
    t = xn2.shape[0]
    c = TC_ROUTE
    return pl.pallas_call(
        _route_kernel,
        grid=(t // c,),
        in_specs=[
            pl.BlockSpec((c, D_MODEL), lambda i: (i, 0)),
            pl.BlockSpec((2 * PEER_HEADS * D_HALF, D_MODEL), lambda i: (0, 0)),
            pl.BlockSpec((2 * PEER_HEADS, N_KEYS, D_HALF), lambda i: (0, 0, 0)),
        ],
        out_specs=[pl.BlockSpec((c, PEER_SEL), lambda i: (i, 0)), pl.BlockSpec((c, PEER_SEL), lambda i: (i, 0))],
        out_shape=[jax.ShapeDtypeStruct((t, PEER_SEL), I32), jax.ShapeDtypeStruct((t, PEER_SEL), F32)],
        scratch_shapes=[pltpu.VMEM((PEER_SEL, c), I32), pltpu.VMEM((PEER_SEL, c), F32)],
        compiler_params=_tc_params("parallel"),
        name="route",
    )(xn2, wq_t, keys)


def _expert_weight_kernel(a_ref, g_ref, w_ref):
    a = a_ref[...]
    w_ref[...] = g_ref[...] * (0.5 * a * (1.0 + lax.erf(a * math.sqrt(0.5))))


def _expert_weight(a, g):
    t = a.shape[0]
    spec = pl.BlockSpec((TM_ELEM, PEER_SEL), lambda i: (i, 0))
    return pl.pallas_call(
        _expert_weight_kernel,
        grid=(t // TM_ELEM,),
        in_specs=[spec, spec],
        out_specs=spec,
        out_shape=jax.ShapeDtypeStruct((t, PEER_SEL), F32),
        compiler_params=_tc_params("parallel"),
        name="expert_weight",
    )(a, g)


def _final_kernel(h_ref, y_ref, g_ref, o_ref):
    h = h_ref[...] + y_ref[...]
    o_ref[...] = h * _rms_scale(h) * g_ref[...]


def _final(h2, y2, g):
    t = h2.shape[0]
    spec = pl.BlockSpec((TM_ELEM, D_MODEL), lambda i: (i, 0))
    return pl.pallas_call(
        _final_kernel,
        grid=(t // TM_ELEM,),
        in_specs=[spec, spec, pl.BlockSpec((1, D_MODEL), lambda i: (0, 0))],
        out_specs=spec,
        out_shape=jax.ShapeDtypeStruct((t, D_MODEL), F32),
        compiler_params=_tc_params("parallel"),
        name="final_norm",
    )(h2, y2, g)


SC_CORES = 2
SC_SUBCORES = 16
SC_WORKERS = SC_CORES * SC_SUBCORES
SC_LANES = 16
SC_GATHER_ROWS = 16
SC_GATHERS_PER_TOKEN = PEER_SEL // SC_GATHER_ROWS
SC_TOKENS = 8
SC_CHUNKS = D_MODEL // SC_LANES


_SC_PARAMS = pltpu.CompilerParams(needs_layout_passes=False)


def _sc_mesh():
    return plsc.VectorSubcoreMesh(core_axis_name="c", subcore_axis_name="s")


def _sc_worker_id():
    return lax.axis_index("s") * SC_CORES + lax.axis_index("c")


def _sc_gather_pipeline(tab_hbm, idx_v, rows_v, sems, compute):
    n_gathers = idx_v.shape[0]

    def gather(n, slot):
        return pltpu.make_async_copy(tab_hbm.at[idx_v[n, :]], rows_v.at[slot], sems.at[slot])

    gather(0, 0).start()

    def step(n2, carry):
        for slot in range(2):
            n = n2 * 2 + slot

            @pl.when(n + 1 < n_gathers)
            def _():
                gather(n + 1, 1 - slot).start()

            gather(n, slot).wait()
            compute(n, slot)
        return carry

    lax.fori_loop(0, n_gathers // 2, step, 0)


def _expert_dots_sc(u_tab, idx, xn2):
    t = idx.shape[0]
    tok_per_worker = t // SC_WORKERS
    idx2 = idx.reshape(t * SC_GATHERS_PER_TOKEN, SC_GATHER_ROWS)

    def body(u_hbm, idx_hbm, x_hbm, a_hbm, idx_v, x_v, rows_v, a_v, sems):
        wid = _sc_worker_id()
        lanes = lax.iota(I32, SC_LANES)

        def compute(n, slot):
            tl = n // SC_GATHERS_PER_TOKEN
            g = n % SC_GATHERS_PER_TOKEN

            def chunk(c, accs):
                off = pl.multiple_of(c * SC_LANES, SC_LANES)
                xv = x_v[tl, pl.ds(off, SC_LANES)]
                return tuple(accs[r] + rows_v[slot, r, pl.ds(off, SC_LANES)] * xv for r in range(SC_GATHER_ROWS))

            accs = lax.fori_loop(0, SC_CHUNKS, chunk,
                                 tuple(jnp.zeros((SC_LANES,), F32) for _ in range(SC_GATHER_ROWS)))
            tot = jnp.zeros((SC_LANES,), F32)
            for r in range(SC_GATHER_ROWS):
                tot = jnp.where(lanes == r, jnp.sum(accs[r]), tot)
            a_v[tl, pl.ds(pl.multiple_of(g * SC_GATHER_ROWS, SC_LANES), SC_LANES)] = tot

        def block(bi, carry):
            tok0 = wid * tok_per_worker + bi * SC_TOKENS
            pltpu.sync_copy(idx_hbm.at[pl.ds(tok0 * SC_GATHERS_PER_TOKEN, SC_TOKENS * SC_GATHERS_PER_TOKEN)], idx_v)
            pltpu.sync_copy(x_hbm.at[pl.ds(tok0, SC_TOKENS)], x_v)
            _sc_gather_pipeline(u_hbm, idx_v, rows_v, sems, compute)
            pltpu.sync_copy(a_v, a_hbm.at[pl.ds(tok0, SC_TOKENS)])
            return carry

        lax.fori_loop(0, tok_per_worker // SC_TOKENS, block, 0)

    return pl.kernel(
        body,
        out_type=jax.ShapeDtypeStruct((t, PEER_SEL), F32),
        mesh=_sc_mesh(),
        scratch_types=[
            pltpu.VMEM((SC_TOKENS * SC_GATHERS_PER_TOKEN, SC_GATHER_ROWS), I32),
            pltpu.VMEM((SC_TOKENS, D_MODEL), F32),
            pltpu.VMEM((2, SC_GATHER_ROWS, D_MODEL), F32),
            pltpu.VMEM((SC_TOKENS, PEER_SEL), F32),
            pltpu.SemaphoreType.DMA((2,)),
        ],
        compiler_params=_SC_PARAMS,
        name="expert_dots",
    )(u_tab, idx2, xn2)


def _expert_mix_sc(v_tab, idx, w):
    t = idx.shape[0]
    tok_per_worker = t // SC_WORKERS
    idx2 = idx.reshape(t * SC_GATHERS_PER_TOKEN, SC_GATHER_ROWS)

    def body(v_hbm, idx_hbm, w_hbm, y_hbm, idx_v, w_v, rows_v, y_v, sems):
        wid = _sc_worker_id()

        def compute(n, slot):
            tl = n // SC_GATHERS_PER_TOKEN
            g = n % SC_GATHERS_PER_TOKEN
            wvec = w_v[tl, pl.ds(pl.multiple_of(g * SC_GATHER_ROWS, SC_LANES), SC_LANES)]

            for half in range(2):
                base = half * (D_MODEL // 2)

                def row(r, accs):
                    wv = jnp.take_along_axis(wvec, jnp.full((SC_LANES,), r, I32), axis=0)
                    return tuple(accs[c] + wv * rows_v[slot, r, pl.ds(base + c * SC_LANES, SC_LANES)]
                                 for c in range(SC_CHUNKS // 2))

                accs = lax.fori_loop(0, SC_GATHER_ROWS, row,
                                     tuple(jnp.zeros((SC_LANES,), F32) for _ in range(SC_CHUNKS // 2)))
                for c in range(SC_CHUNKS // 2):
                    sl = pl.ds(base + c * SC_LANES, SC_LANES)
                    y_v[tl, sl] = y_v[tl, sl] + accs[c]

        def block(bi, carry):
            tok0 = wid * tok_per_worker + bi * SC_TOKENS
            pltpu.sync_copy(idx_hbm.at[pl.ds(tok0 * SC_GATHERS_PER_TOKEN, SC_TOKENS * SC_GATHERS_PER_TOKEN)], idx_v)
            pltpu.sync_copy(w_hbm.at[pl.ds(tok0, SC_TOKENS)], w_v)
            zero = jnp.zeros((SC_LANES,), F32)
            for tl in range(SC_TOKENS):
                for c in range(SC_CHUNKS):
                    y_v[tl, pl.ds(c * SC_LANES, SC_LANES)] = zero
            _sc_gather_pipeline(v_hbm, idx_v, rows_v, sems, compute)
            pltpu.sync_copy(y_v, y_hbm.at[pl.ds(tok0, SC_TOKENS)])
            return carry

        lax.fori_loop(0, tok_per_worker // SC_TOKENS, block, 0)

    return pl.kernel(
        body,
        out_type=jax.ShapeDtypeStruct((t, D_MODEL), F32),
        mesh=_sc_mesh(),
        scratch_types=[
            pltpu.VMEM((SC_TOKENS * SC_GATHERS_PER_TOKEN, SC_GATHER_ROWS), I32),
            pltpu.VMEM((SC_TOKENS, PEER_SEL), F32),
            pltpu.VMEM((2, SC_GATHER_ROWS, D_MODEL), F32),
            pltpu.VMEM((SC_TOKENS, D_MODEL), F32),
            pltpu.SemaphoreType.DMA((2,)),
        ],
        compiler_params=_SC_PARAMS,
        name="expert_mix",
    )(v_tab, idx2, w)


def _rearranged_in_proj(w_in, b_in):
    def cols(a):
        q = a[..., :Q_W]
        k = a[..., Q_W:Q_W + N_KV_HEADS * HEAD_DIM]
        v = a[..., Q_W + N_KV_HEADS * HEAD_DIM:Q_W + 2 * N_KV_HEADS * HEAD_DIM]
        rest = a[..., Q_W + 2 * N_KV_HEADS * HEAD_DIM:]
        dup = lambda m: jnp.concatenate(
            [m[..., hd * HEAD_DIM:(hd + 1) * HEAD_DIM] for hd in range(N_KV_HEADS) for _ in range(2)], axis=-1)
        return jnp.concatenate([q, dup(k), dup(v), rest], axis=-1)
    return cols(w_in).astype(BF16), cols(b_in)[None, :]


def kernel(x, ln_mix_g, w_in, b_in, attn_sinks, w_attn_up, w_pool_grp, pool_scale, w_pool_up, w_o,
           ln_ffn_g, w_query, sub_keys, u_experts, v_experts, ln_final_g):
    b, s, d = x.shape
    t = b * s
    h = x
    for l in range(w_in.shape[0]):
        w_r, b_r = _rearranged_in_proj(w_in[l], b_in[l])
        q, kv, pz, gate = _inproj(h.reshape(t, d), ln_mix_g[l][None, :], w_r, b_r)
        o = _attention(attn_sinks[l], q.reshape(b, s, Q_W), kv.reshape(b, s, KV_COLS))
        h, xn2 = _merge(o, gate.reshape(b, s, 2 * d), h, pz.reshape(b, s, POOL_W),
                        w_attn_up[l].astype(BF16), w_pool_grp[l].astype(BF16), pool_scale[l][None, :],
                        w_pool_up[l].astype(BF16), w_o[l].astype(BF16), ln_ffn_g[l][None, :])
        xn2 = xn2.reshape(t, d)
        wq_t = w_query[l].T.astype(BF16)
        keys = sub_keys[l].reshape(2 * PEER_HEADS, N_KEYS, D_HALF).astype(BF16)
        idx, g = _route(xn2, wq_t, keys)
        a = _expert_dots_sc(u_experts[l], idx, xn2)
        w = _expert_weight(a, g)
        y = _expert_mix_sc(v_experts[l], idx, w)
        h = h.reshape(t, d)
        if l + 1 < w_in.shape[0]:
            h = (h + y).reshape(b, s, d)
    return _final(h, y, ln_final_g[None, :]).reshape(b, s, d)
```

```python
import functools
import math

import jax
import jax.numpy as jnp
from jax import lax
from jax.experimental import pallas as pl
from jax.experimental.pallas import tpu as pltpu

F32 = jnp.float32
BF16 = jnp.bfloat16
I32 = jnp.int32

D_MODEL = 1024
N_HEADS = 16
N_KV_HEADS = 2
GROUP = N_HEADS // N_KV_HEADS
HEAD_DIM = 64
BLOCK = 128
ATTN_SCALE = 1.0 / math.sqrt(HEAD_DIM)
NEG_INF = -1e30
POOL_WINDOWS = (2, 4, 8, 16)
POOL_GROUP = 128
POOL_W = len(POOL_WINDOWS) * POOL_GROUP
POOL_HIST = max(POOL_WINDOWS)
Q_W = N_HEADS * HEAD_DIM
PEER_HEADS = 8
N_KEYS = 128
N_EXPERTS = N_KEYS * N_KEYS
D_HALF = 128
PEER_TOPK = 16
PEER_SEL = PEER_HEADS * PEER_TOPK
EPS = 1e-5

VMEM_LIMIT_BYTES = 56 * 1024 * 1024
SUBLANES = 8

C_Q = 0
C_K = C_Q + Q_W
C_V = C_K + 2 * BLOCK
C_P = C_V + 2 * BLOCK
C_G = C_P + POOL_W
C_END = C_G + 2 * D_MODEL
KV_COLS = C_P - C_K

TM_PROJ = 512
TS_MERGE = 512
TC_ROUTE = 256
TC_EXPERT = 512
EXPERT_BLOCK = 1024
KEYS_PER_BLOCK = EXPERT_BLOCK // N_KEYS


def _tc_params(*sem):
    return pltpu.CompilerParams(dimension_semantics=sem, vmem_limit_bytes=VMEM_LIMIT_BYTES)


def _rms_scale(x):
    return lax.rsqrt(jnp.mean(x * x, axis=-1, keepdims=True) + EPS)


def _inproj_kernel(x_ref, g_ref, w_ref, b_ref, q_ref, kv_ref, pz_ref, gate_ref):
    x = x_ref[...]
    xn = (x * _rms_scale(x) * g_ref[...]).astype(BF16)

    def proj(lo, hi):
        return jnp.dot(xn, w_ref[:, lo:hi], preferred_element_type=F32) + b_ref[:, lo:hi]

    q_ref[...] = proj(C_Q, C_K).astype(BF16)
    kv_ref[...] = proj(C_K, C_P).astype(BF16)
    pz_ref[...] = proj(C_P, C_G)
    gate_ref[...] = proj(C_G, C_END).astype(BF16)


def _inproj(x2, g, w, b):
    t = x2.shape[0]
    row = lambda i: (i, 0)
    fixed = lambda i: (0, 0)
    return pl.pallas_call(
        _inproj_kernel,
        grid=(t // TM_PROJ,),
        in_specs=[
            pl.BlockSpec((TM_PROJ, D_MODEL), row),
            pl.BlockSpec((1, D_MODEL), fixed),
            pl.BlockSpec((D_MODEL, C_END), fixed),
            pl.BlockSpec((1, C_END), fixed),
        ],
        out_specs=[
            pl.BlockSpec((TM_PROJ, Q_W), row),
            pl.BlockSpec((TM_PROJ, KV_COLS), row),
            pl.BlockSpec((TM_PROJ, POOL_W), row),
            pl.BlockSpec((TM_PROJ, 2 * D_MODEL), row),
        ],
        out_shape=[
            jax.ShapeDtypeStruct((t, Q_W), BF16),
            jax.ShapeDtypeStruct((t, KV_COLS), BF16),
            jax.ShapeDtypeStruct((t, POOL_W), F32),
            jax.ShapeDtypeStruct((t, 2 * D_MODEL), BF16),
        ],
        compiler_params=_tc_params("parallel"),
        name="inproj",
    )(x2, g, w, b)


def _attn_kernel(sink_ref, q_ref, kvc_ref, kvp_ref, o_ref):
    j = pl.program_id(1)
    kvc = kvc_ref[0]
    kvp = kvp_ref[0]
    row = lax.broadcasted_iota(I32, (BLOCK, BLOCK), 0)
    col = lax.broadcasted_iota(I32, (BLOCK, BLOCK), 1)
    mask = jnp.concatenate([jnp.logical_and(col > row, j > 0), col <= row], axis=1)
    low = lax.broadcasted_iota(I32, (1, BLOCK), 1) < HEAD_DIM
    zero = jnp.zeros((), BF16)
    for hkv in range(N_KV_HEADS):
        kc = slice(hkv * BLOCK, (hkv + 1) * BLOCK)
        vc = slice(2 * BLOCK + hkv * BLOCK, 2 * BLOCK + (hkv + 1) * BLOCK)
        k2 = jnp.concatenate([kvp[:, kc], kvc[:, kc]], axis=0)
        v2 = jnp.concatenate([kvp[:, vc], kvc[:, vc]], axis=0)
        v_half = (jnp.where(low, v2, zero), jnp.where(low, zero, v2))
        for pair in range(GROUP // 2):
            c0 = (hkv * (GROUP // 2) + pair) * BLOCK
            q2 = q_ref[0, :, c0:c0 + BLOCK]
            o2 = jnp.zeros((BLOCK, BLOCK), F32)
            for half in range(2):
                head = hkv * GROUP + pair * 2 + half
                qm = jnp.where(low, q2, zero) if half == 0 else jnp.where(low, zero, q2)
                s = lax.dot_general(qm, k2, (((1,), (1,)), ((), ())), preferred_element_type=F32)
                s = jnp.where(mask, s * ATTN_SCALE, NEG_INF)
                sink = sink_ref[head]
                m = jnp.maximum(jnp.max(s, axis=-1, keepdims=True), sink)
                e = jnp.exp(s - m)
                denom = jnp.sum(e, axis=-1, keepdims=True) + jnp.exp(sink - m)
                p = (e * (1.0 / denom)).astype(BF16)
                o2 = o2 + jnp.dot(p, v_half[half], preferred_element_type=F32)
            o_ref[0, :, c0:c0 + BLOCK] = o2.astype(BF16)


def _attention(sinks, q3, kv3):
    b, s, _ = q3.shape
    return pl.pallas_call(
        _attn_kernel,
        grid=(b, s // BLOCK),
        in_specs=[
            pl.BlockSpec(memory_space=pltpu.SMEM),
            pl.BlockSpec((1, BLOCK, Q_W), lambda i, j: (i, j, 0)),
            pl.BlockSpec((1, BLOCK, KV_COLS), lambda i, j: (i, j, 0)),
            pl.BlockSpec((1, BLOCK, KV_COLS), lambda i, j: (i, jnp.maximum(j - 1, 0), 0)),
        ],
        out_specs=pl.BlockSpec((1, BLOCK, Q_W), lambda i, j: (i, j, 0)),
        out_shape=jax.ShapeDtypeStruct((b, s, Q_W), BF16),
        compiler_params=_tc_params("parallel", "arbitrary"),
        name="attention",
    )(sinks, q3, kv3, kv3)


def _sigmoid(x):
    return 1.0 / (1.0 + jnp.exp(-x))


def _merge_kernel(o_ref, gate_ref, x_ref, pz_ref, pzp_ref, wau_ref, wgrp_ref, scale_ref, wpu_ref,
                  wo_ref, g_ref, h_ref, xn_ref):
    j = pl.program_id(1)
    ts = pz_ref.shape[1]
    prev = jnp.where(j > 0, pzp_ref[0], 0.0)
    ext = jnp.concatenate([prev, pz_ref[0]], axis=0)
    t1 = (j * ts + 1 + lax.broadcasted_iota(I32, (ts, 1), 0)).astype(F32)
    ys = []
    for g, w in enumerate(POOL_WINDOWS):
        e = ext[:, g * POOL_GROUP:(g + 1) * POOL_GROUP]
        tsum = e
        span = 1
        while span < w:
            tsum = tsum + pltpu.roll(tsum, span, 0)
            span *= 2
        pooled = tsum[POOL_HIST:] / jnp.minimum(t1, float(w)) - e[POOL_HIST:]
        y = jnp.dot(pooled.astype(BF16), wgrp_ref[g], preferred_element_type=F32)
        ys.append((y * scale_ref[:, g * POOL_GROUP:(g + 1) * POOL_GROUP]).astype(BF16))
    y_p = jnp.dot(jnp.concatenate(ys, axis=1), wpu_ref[...], preferred_element_type=F32)
    y_a = jnp.dot(o_ref[0], wau_ref[...], preferred_element_type=F32)
    gate = gate_ref[0]
    merged = (_sigmoid(gate[:, :D_MODEL].astype(F32)) * y_a
              + _sigmoid(gate[:, D_MODEL:].astype(F32)) * y_p)
    h = x_ref[0] + jnp.dot(merged.astype(BF16), wo_ref[...], preferred_element_type=F32)
    h_ref[0] = h
    xn_ref[0] = h * _rms_scale(h) * g_ref[...]


def _merge(o3, gate3, x, pz3, wau, wgrp, scale, wpu, wo, g):
    b, s, _ = x.shape
    ts = TS_MERGE
    hist_blocks = ts // POOL_HIST
    tile = lambda i, j: (i, j, 0)
    fixed2 = lambda i, j: (0, 0)
    return pl.pallas_call(
        _merge_kernel,
        grid=(b, s // ts),
        in_specs=[
            pl.BlockSpec((1, ts, Q_W), tile),
            pl.BlockSpec((1, ts, 2 * D_MODEL), tile),
            pl.BlockSpec((1, ts, D_MODEL), tile),
            pl.BlockSpec((1, ts, POOL_W), tile),
            pl.BlockSpec((1, POOL_HIST, POOL_W), lambda i, j: (i, jnp.maximum(j * hist_blocks - 1, 0), 0)),
            pl.BlockSpec((Q_W, D_MODEL), fixed2),
            pl.BlockSpec((len(POOL_WINDOWS), POOL_GROUP, POOL_GROUP), lambda i, j: (0, 0, 0)),
            pl.BlockSpec((1, POOL_W), fixed2),
            pl.BlockSpec((POOL_W, D_MODEL), fixed2),
            pl.BlockSpec((D_MODEL, D_MODEL), fixed2),
            pl.BlockSpec((1, D_MODEL), fixed2),
        ],
        out_specs=[pl.BlockSpec((1, ts, D_MODEL), tile), pl.BlockSpec((1, ts, D_MODEL), tile)],
        out_shape=[jax.ShapeDtypeStruct((b, s, D_MODEL), F32), jax.ShapeDtypeStruct((b, s, D_MODEL), F32)],
        compiler_params=_tc_params("parallel", "arbitrary"),
        name="merge",
    )(o3, gate3, x, pz3, pz3, wau, wgrp, scale, wpu, wo, g)


def _top16_rows(sc, pos):
    n = sc.shape[0]
    vals, rows = [], []
    for _ in range(PEER_TOPK):
        m = jnp.max(sc, axis=0, keepdims=True)
        am = jnp.min(jnp.where(sc == m, pos, n), axis=0, keepdims=True)
        vals.append(m)
        rows.append(am)
        sc = jnp.where(pos == am, -jnp.inf, sc)
    return jnp.concatenate(vals, axis=0), jnp.concatenate(rows, axis=0)


def _select_row(table, sel, pos16):
    out = []
    for k in range(PEER_TOPK):
        out.append(jnp.sum(jnp.where(pos16 == sel[k:k + 1], table, 0), axis=0, keepdims=True))
    return jnp.concatenate(out, axis=0)


def _route_kernel(xn_ref, wq_ref, keys_ref, idx_ref, gate_ref, idx_scr, gate_scr):
    c = xn_ref.shape[0]
    xb = xn_ref[...].astype(BF16)
    pos_keys = lax.broadcasted_iota(I32, (N_KEYS, c), 0)
    pos_cand = lax.broadcasted_iota(I32, (PEER_TOPK * PEER_TOPK, c), 0)
    pos16 = lax.broadcasted_iota(I32, (PEER_TOPK, c), 0)

    def head_body(h, carry):
        sv, si = [], []
        for half in range(2):
            hp = h * 2 + half
            wq = wq_ref[pl.ds(pl.multiple_of(hp * D_HALF, D_HALF), D_HALF), :]
            q_t = lax.dot_general(wq, xb, (((1,), (1,)), ((), ())), preferred_element_type=F32)
            sc = jnp.dot(keys_ref[hp], q_t.astype(BF16), preferred_element_type=F32)
            v, i = _top16_rows(sc, pos_keys)
            sv.append(v)
            si.append(i)
        cand = jnp.concatenate([sv[0][a:a + 1] + sv[1] for a in range(PEER_TOPK)], axis=0)
        fv, fpos = _top16_rows(cand, pos_cand)
        i0 = _select_row(si[0], fpos >> 4, pos16)
        i1 = _select_row(si[1], fpos & (PEER_TOPK - 1), pos16)
        e = jnp.exp(fv - fv[0:1])
        rows = pl.ds(pl.multiple_of(h * PEER_TOPK, PEER_TOPK), PEER_TOPK)
        idx_scr[rows, :] = i0 * N_KEYS + i1
        gate_scr[rows, :] = e / jnp.sum(e, axis=0, keepdims=True)
        return carry

    lax.fori_loop(0, PEER_HEADS, head_body, 0)
    idx_ref[...] = idx_scr[...].T
    gate_ref[...] = gate_scr[...].T


def _route(xn2, wq_t, keys):
    t = xn2.shape[0]
    c = TC_ROUTE
    return pl.pallas_call(
        _route_kernel,
        grid=(t // c,),
        in_specs=[
            pl.BlockSpec((c, D_MODEL), lambda i: (i, 0)),
            pl.BlockSpec((2 * PEER_HEADS * D_HALF, D_MODEL), lambda i: (0, 0)),
            pl.BlockSpec((2 * PEER_HEADS, N_KEYS, D_HALF), lambda i: (0, 0, 0)),
        ],
        out_specs=[pl.BlockSpec((c, PEER_SEL), lambda i: (i, 0)), pl.BlockSpec((c, PEER_SEL), lambda i: (i, 0))],
        out_shape=[jax.ShapeDtypeStruct((t, PEER_SEL), I32), jax.ShapeDtypeStruct((t, PEER_SEL), F32)],
        scratch_shapes=[pltpu.VMEM((PEER_SEL, c), I32), pltpu.VMEM((PEER_SEL, c), F32)],
        compiler_params=_tc_params("parallel"),
        name="route",
    )(xn2, wq_t, keys)


def _key_onehots(idx_row):
    key = lax.broadcasted_iota(I32, (N_KEYS, PEER_SEL), 0)
    return key == (idx_row >> 7), key == (idx_row & (N_KEYS - 1))


def _expert_act_kernel(xn_ref, ut_ref, idx_ref, a_ref, grid_ref, xb_ref):
    k = pl.program_id(1)
    c = xn_ref.shape[0]

    @pl.when(k == 0)
    def _():
        xb_ref[...] = xn_ref[...].astype(BF16)

    dense = jnp.dot(xb_ref[...], ut_ref[...], preferred_element_type=F32)
    for kk in range(KEYS_PER_BLOCK):
        grid_ref[pl.ds(k * KEYS_PER_BLOCK + kk, c, stride=N_KEYS), :] = dense[:, kk * N_KEYS:(kk + 1) * N_KEYS]

    @pl.when(k == pl.num_programs(1) - 1)
    def _():
        def group(gi, carry):
            t0 = pl.multiple_of(gi * SUBLANES, SUBLANES)
            rows = []
            for g in range(SUBLANES):
                first, second = _key_onehots(idx_ref[pl.ds(t0 + g, 1), :])
                a_t = grid_ref[pl.ds(pl.multiple_of((t0 + g) * N_KEYS, N_KEYS), N_KEYS), :]
                picked = jnp.dot(a_t.astype(BF16), jnp.where(first, 1.0, 0.0).astype(BF16),
                                 preferred_element_type=F32)
                rows.append(jnp.sum(jnp.where(second, picked, 0.0), axis=0, keepdims=True))
            a_ref[pl.ds(t0, SUBLANES), :] = jnp.concatenate(rows, axis=0)
            return carry

        lax.fori_loop(0, c // SUBLANES, group, 0)


def _expert_act(xn2, u_t, idx):
    t = xn2.shape[0]
    c = TC_EXPERT
    return pl.pallas_call(
        _expert_act_kernel,
        grid=(t // c, N_EXPERTS // EXPERT_BLOCK),
        in_specs=[
            pl.BlockSpec((c, D_MODEL), lambda i, k: (i, 0)),
            pl.BlockSpec((D_MODEL, EXPERT_BLOCK), lambda i, k: (0, k)),
            pl.BlockSpec((c, PEER_SEL), lambda i, k: (i, 0)),
        ],
        out_specs=pl.BlockSpec((c, PEER_SEL), lambda i, k: (i, 0)),
        out_shape=jax.ShapeDtypeStruct((t, PEER_SEL), F32),
        scratch_shapes=[pltpu.VMEM((c * N_KEYS, N_KEYS), F32), pltpu.VMEM((c, D_MODEL), BF16)],
        compiler_params=_tc_params("parallel", "arbitrary"),
        name="expert_act",
    )(xn2, u_t, idx)


def _gelu(a):
    return 0.5 * a * (1.0 + lax.erf(a * math.sqrt(0.5)))


def _expert_mix_kernel(normalize, idx_ref, a_ref, gate_ref, v_ref, h_ref, g_ref, o_ref, grid_ref, w_ref, acc_ref):
    k = pl.program_id(1)
    c = idx_ref.shape[0]

    @pl.when(k == 0)
    def _():
        acc_ref[...] = h_ref[...]
        w_ref[...] = gate_ref[...] * _gelu(a_ref[...])

        def token(t, carry):
            first, second = _key_onehots(idx_ref[pl.ds(t, 1), :])
            weighted = jnp.where(first, w_ref[pl.ds(t, 1), :], 0.0).astype(BF16)
            w_t = lax.dot_general(weighted, jnp.where(second, 1.0, 0.0).astype(BF16),
                                  (((1,), (1,)), ((), ())), preferred_element_type=F32)
            grid_ref[pl.ds(pl.multiple_of(t * N_KEYS, N_KEYS), N_KEYS), :] = w_t
            return carry

        lax.fori_loop(0, c, token, 0, unroll=SUBLANES)

    dense = jnp.concatenate(
        [grid_ref[pl.ds(k * KEYS_PER_BLOCK + kk, c, stride=N_KEYS), :] for kk in range(KEYS_PER_BLOCK)], axis=1)
    acc_ref[...] += jnp.dot(dense.astype(BF16), v_ref[...], preferred_element_type=F32)

    @pl.when(k == pl.num_programs(1) - 1)
    def _():
        hh = acc_ref[...]
        o_ref[...] = hh * _rms_scale(hh) * g_ref[...] if normalize else hh


def _expert_mix(idx, a, gate, v_tab, h2, g, normalize):
    t = idx.shape[0]
    c = TC_EXPERT
    sel = pl.BlockSpec((c, PEER_SEL), lambda i, k: (i, 0))
    tok = pl.BlockSpec((c, D_MODEL), lambda i, k: (i, 0))
    return pl.pallas_call(
        functools.partial(_expert_mix_kernel, normalize),
        grid=(t // c, N_EXPERTS // EXPERT_BLOCK),
        in_specs=[sel, sel, sel, pl.BlockSpec((EXPERT_BLOCK, D_MODEL), lambda i, k: (k, 0)), tok,
                  pl.BlockSpec((1, D_MODEL), lambda i, k: (0, 0))],
        out_specs=tok,
        out_shape=jax.ShapeDtypeStruct((t, D_MODEL), F32),
        scratch_shapes=[pltpu.VMEM((c * N_KEYS, N_KEYS), F32), pltpu.VMEM((c, PEER_SEL), F32),
                        pltpu.VMEM((c, D_MODEL), F32)],
        compiler_params=_tc_params("parallel", "arbitrary"),
        name="expert_mix",
    )(idx, a, gate, v_tab, h2, g)


def _rearranged_in_proj(w_in, b_in):
    def cols(a):
        q = a[..., :Q_W]
        k = a[..., Q_W:Q_W + N_KV_HEADS * HEAD_DIM]
        v = a[..., Q_W + N_KV_HEADS * HEAD_DIM:Q_W + 2 * N_KV_HEADS * HEAD_DIM]
        rest = a[..., Q_W + 2 * N_KV_HEADS * HEAD_DIM:]
        dup = lambda m: jnp.concatenate(
            [m[..., hd * HEAD_DIM:(hd + 1) * HEAD_DIM] for hd in range(N_KV_HEADS) for _ in range(2)], axis=-1)
        return jnp.concatenate([q, dup(k), dup(v), rest], axis=-1)
    return cols(w_in).astype(BF16), cols(b_in)[None, :]


def kernel(x, ln_mix_g, w_in, b_in, attn_sinks, w_attn_up, w_pool_grp, pool_scale, w_pool_up, w_o,
           ln_ffn_g, w_query, sub_keys, u_experts, v_experts, ln_final_g):
    b, s, d = x.shape
    t = b * s
    depth = w_in.shape[0]
    h = x
    for l in range(depth):
        w_r, b_r = _rearranged_in_proj(w_in[l], b_in[l])
        q, kv, pz, gate = _inproj(h.reshape(t, d), ln_mix_g[l][None, :], w_r, b_r)
        o = _attention(attn_sinks[l], q.reshape(b, s, Q_W), kv.reshape(b, s, KV_COLS))
        h, xn2 = _merge(o, gate.reshape(b, s, 2 * d), h, pz.reshape(b, s, POOL_W),
                        w_attn_up[l].astype(BF16), w_pool_grp[l].astype(BF16), pool_scale[l][None, :],
                        w_pool_up[l].astype(BF16), w_o[l].astype(BF16), ln_ffn_g[l][None, :])
        xn2 = xn2.reshape(t, d)
        wq_t = w_query[l].T.astype(BF16)
        keys = sub_keys[l].reshape(2 * PEER_HEADS, N_KEYS, D_HALF).astype(BF16)
        idx, g = _route(xn2, wq_t, keys)
        u_t = u_experts[l].astype(BF16).reshape(N_KEYS, N_KEYS, d).transpose(2, 1, 0).reshape(d, N_EXPERTS)
        a = _expert_act(xn2, u_t, idx)
        last = l + 1 == depth
        h = _expert_mix(idx, a, g, v_experts[l].astype(BF16), h.reshape(t, d), ln_final_g[None, :], last)
        h = h.reshape(b, s, d)
    return h
```

```python
import functools
import math

import jax
import jax.numpy as jnp
from jax import lax
from jax.experimental import pallas as pl
from jax.experimental.pallas import tpu as pltpu

F32 = jnp.float32
BF16 = jnp.bfloat16
I32 = jnp.int32

D_MODEL = 1024
N_HEADS = 16
N_KV_HEADS = 2
GROUP = N_HEADS // N_KV_HEADS
HEAD_DIM = 64
BLOCK = 128
ATTN_SCALE = 1.0 / math.sqrt(HEAD_DIM)
NEG_INF = -1e30
POOL_WINDOWS = (2, 4, 8, 16)
POOL_GROUP = 128
POOL_W = len(POOL_WINDOWS) * POOL_GROUP
POOL_HIST = max(POOL_WINDOWS)
Q_W = N_HEADS * HEAD_DIM
PEER_HEADS = 8
N_KEYS = 128
N_EXPERTS = N_KEYS * N_KEYS
D_HALF = 128
PEER_TOPK = 16
PEER_SEL = PEER_HEADS * PEER_TOPK
EPS = 1e-5

VMEM_LIMIT_BYTES = 56 * 1024 * 1024
SUBLANES = 8

C_Q = 0
C_K = C_Q + Q_W
C_V = C_K + 2 * BLOCK
C_P = C_V + 2 * BLOCK
C_G = C_P + POOL_W
C_END = C_G + 2 * D_MODEL
KV_COLS = C_P - C_K

TM_PROJ = 512
TS_MERGE = 512
TC_ROUTE = 256
TC_EXPERT = 512
EXPERT_BLOCK = 1024
KEYS_PER_BLOCK = EXPERT_BLOCK // N_KEYS
TOKENS_PER_TRIP = 16


def _tc_params(*sem):
    return pltpu.CompilerParams(dimension_semantics=sem, vmem_limit_bytes=VMEM_LIMIT_BYTES)


def _rms_scale(x):
    return lax.rsqrt(jnp.mean(x * x, axis=-1, keepdims=True) + EPS)


def _inproj_kernel(x_ref, g_ref, w_ref, b_ref, q_ref, kv_ref, pz_ref, gate_ref):
    x = x_ref[...]
    xn = (x * _rms_scale(x) * g_ref[...]).astype(BF16)

    def proj(lo, hi):
        return jnp.dot(xn, w_ref[:, lo:hi], preferred_element_type=F32) + b_ref[:, lo:hi]

    q_ref[...] = proj(C_Q, C_K).astype(BF16)
    kv_ref[...] = proj(C_K, C_P).astype(BF16)
    pz_ref[...] = proj(C_P, C_G)
    gate_ref[...] = proj(C_G, C_END).astype(BF16)


def _inproj(x2, g, w, b):
    t = x2.shape[0]
    row = lambda i: (i, 0)
    fixed = lambda i: (0, 0)
    return pl.pallas_call(
        _inproj_kernel,
        grid=(t // TM_PROJ,),
        in_specs=[
            pl.BlockSpec((TM_PROJ, D_MODEL), row),
            pl.BlockSpec((1, D_MODEL), fixed),
            pl.BlockSpec((D_MODEL, C_END), fixed),
            pl.BlockSpec((1, C_END), fixed),
        ],
        out_specs=[
            pl.BlockSpec((TM_PROJ, Q_W), row),
            pl.BlockSpec((TM_PROJ, KV_COLS), row),
            pl.BlockSpec((TM_PROJ, POOL_W), row),
            pl.BlockSpec((TM_PROJ, 2 * D_MODEL), row),
        ],
        out_shape=[
            jax.ShapeDtypeStruct((t, Q_W), BF16),
            jax.ShapeDtypeStruct((t, KV_COLS), BF16),
            jax.ShapeDtypeStruct((t, POOL_W), F32),
            jax.ShapeDtypeStruct((t, 2 * D_MODEL), BF16),
        ],
        compiler_params=_tc_params("parallel"),
        name="inproj",
    )(x2, g, w, b)


def _attn_kernel(sink_ref, q_ref, kvc_ref, kvp_ref, o_ref):
    j = pl.program_id(1)
    kvc = kvc_ref[0]
    kvp = kvp_ref[0]
    row = lax.broadcasted_iota(I32, (BLOCK, BLOCK), 0)
    col = lax.broadcasted_iota(I32, (BLOCK, BLOCK), 1)
    mask = jnp.concatenate([jnp.logical_and(col > row, j > 0), col <= row], axis=1)
    low = lax.broadcasted_iota(I32, (1, BLOCK), 1) < HEAD_DIM
    zero = jnp.zeros((), BF16)
    for hkv in range(N_KV_HEADS):
        kc = slice(hkv * BLOCK, (hkv + 1) * BLOCK)
        vc = slice(2 * BLOCK + hkv * BLOCK, 2 * BLOCK + (hkv + 1) * BLOCK)
        k2 = jnp.concatenate([kvp[:, kc], kvc[:, kc]], axis=0)
        v2 = jnp.concatenate([kvp[:, vc], kvc[:, vc]], axis=0)
        v_half = (jnp.where(low, v2, zero), jnp.where(low, zero, v2))
        for pair in range(GROUP // 2):
            c0 = (hkv * (GROUP // 2) + pair) * BLOCK
            q2 = q_ref[0, :, c0:c0 + BLOCK]
            o2 = jnp.zeros((BLOCK, BLOCK), F32)
            for half in range(2):
                head = hkv * GROUP + pair * 2 + half
                qm = jnp.where(low, q2, zero) if half == 0 else jnp.where(low, zero, q2)
                s = lax.dot_general(qm, k2, (((1,), (1,)), ((), ())), preferred_element_type=F32)
                s = jnp.where(mask, s * ATTN_SCALE, NEG_INF)
                sink = sink_ref[head]
                m = jnp.maximum(jnp.max(s, axis=-1, keepdims=True), sink)
                e = jnp.exp(s - m)
                denom = jnp.sum(e, axis=-1, keepdims=True) + jnp.exp(sink - m)
                p = (e * (1.0 / denom)).astype(BF16)
                o2 = o2 + jnp.dot(p, v_half[half], preferred_element_type=F32)
            o_ref[0, :, c0:c0 + BLOCK] = o2.astype(BF16)


def _attention(sinks, q3, kv3):
    b, s, _ = q3.shape
    return pl.pallas_call(
        _attn_kernel,
        grid=(b, s // BLOCK),
        in_specs=[
            pl.BlockSpec(memory_space=pltpu.SMEM),
            pl.BlockSpec((1, BLOCK, Q_W), lambda i, j: (i, j, 0)),
            pl.BlockSpec((1, BLOCK, KV_COLS), lambda i, j: (i, j, 0)),
            pl.BlockSpec((1, BLOCK, KV_COLS), lambda i, j: (i, jnp.maximum(j - 1, 0), 0)),
        ],
        out_specs=pl.BlockSpec((1, BLOCK, Q_W), lambda i, j: (i, j, 0)),
        out_shape=jax.ShapeDtypeStruct((b, s, Q_W), BF16),
        compiler_params=_tc_params("parallel", "arbitrary"),
        name="attention",
    )(sinks, q3, kv3, kv3)


def _sigmoid(x):
    return 1.0 / (1.0 + jnp.exp(-x))


def _merge_kernel(o_ref, gate_ref, x_ref, pz_ref, pzp_ref, wau_ref, wgrp_ref, scale_ref, wpu_ref,
                  wo_ref, g_ref, h_ref, xn_ref):
    j = pl.program_id(1)
    ts = pz_ref.shape[1]
    prev = jnp.where(j > 0, pzp_ref[0], 0.0)
    ext = jnp.concatenate([prev, pz_ref[0]], axis=0)
    t1 = (j * ts + 1 + lax.broadcasted_iota(I32, (ts, 1), 0)).astype(F32)
    ys = []
    for g, w in enumerate(POOL_WINDOWS):
        e = ext[:, g * POOL_GROUP:(g + 1) * POOL_GROUP]
        tsum = e
        span = 1
        while span < w:
            tsum = tsum + pltpu.roll(tsum, span, 0)
            span *= 2
        pooled = tsum[POOL_HIST:] / jnp.minimum(t1, float(w)) - e[POOL_HIST:]
        y = jnp.dot(pooled.astype(BF16), wgrp_ref[g], preferred_element_type=F32)
        ys.append((y * scale_ref[:, g * POOL_GROUP:(g + 1) * POOL_GROUP]).astype(BF16))
    y_p = jnp.dot(jnp.concatenate(ys, axis=1), wpu_ref[...], preferred_element_type=F32)
    y_a = jnp.dot(o_ref[0], wau_ref[...], preferred_element_type=F32)
    gate = gate_ref[0]
    merged = (_sigmoid(gate[:, :D_MODEL].astype(F32)) * y_a
              + _sigmoid(gate[:, D_MODEL:].astype(F32)) * y_p)
    h = x_ref[0] + jnp.dot(merged.astype(BF16), wo_ref[...], preferred_element_type=F32)
    h_ref[0] = h
    xn_ref[0] = h * _rms_scale(h) * g_ref[...]


def _merge(o3, gate3, x, pz3, wau, wgrp, scale, wpu, wo, g):
    b, s, _ = x.shape
    ts = TS_MERGE
    hist_blocks = ts // POOL_HIST
    tile = lambda i, j: (i, j, 0)
    fixed2 = lambda i, j: (0, 0)
    return pl.pallas_call(
        _merge_kernel,
        grid=(b, s // ts),
        in_specs=[
            pl.BlockSpec((1, ts, Q_W), tile),
            pl.BlockSpec((1, ts, 2 * D_MODEL), tile),
            pl.BlockSpec((1, ts, D_MODEL), tile),
            pl.BlockSpec((1, ts, POOL_W), tile),
            pl.BlockSpec((1, POOL_HIST, POOL_W), lambda i, j: (i, jnp.maximum(j * hist_blocks - 1, 0), 0)),
            pl.BlockSpec((Q_W, D_MODEL), fixed2),
            pl.BlockSpec((len(POOL_WINDOWS), POOL_GROUP, POOL_GROUP), lambda i, j: (0, 0, 0)),
            pl.BlockSpec((1, POOL_W), fixed2),
            pl.BlockSpec((POOL_W, D_MODEL), fixed2),
            pl.BlockSpec((D_MODEL, D_MODEL), fixed2),
            pl.BlockSpec((1, D_MODEL), fixed2),
        ],
        out_specs=[pl.BlockSpec((1, ts, D_MODEL), tile), pl.BlockSpec((1, ts, D_MODEL), tile)],
        out_shape=[jax.ShapeDtypeStruct((b, s, D_MODEL), F32), jax.ShapeDtypeStruct((b, s, D_MODEL), F32)],
        compiler_params=_tc_params("parallel", "arbitrary"),
        name="merge",
    )(o3, gate3, x, pz3, pz3, wau, wgrp, scale, wpu, wo, g)


def _top16_rows(sc, pos):
    n = sc.shape[0]
    vals, rows = [], []
    for _ in range(PEER_TOPK):
        m = jnp.max(sc, axis=0, keepdims=True)
        am = jnp.min(jnp.where(sc == m, pos, n), axis=0, keepdims=True)
        vals.append(m)
        rows.append(am)
        sc = jnp.where(pos == am, -jnp.inf, sc)
    return jnp.concatenate(vals, axis=0), jnp.concatenate(rows, axis=0)


def _select_row(table, sel, pos16):
    out = []
    for k in range(PEER_TOPK):
        out.append(jnp.sum(jnp.where(pos16 == sel[k:k + 1], table, 0), axis=0, keepdims=True))
    return jnp.concatenate(out, axis=0)


def _route_kernel(xn_ref, wq_ref, keys_ref, idx_ref, gate_ref, idx_scr, gate_scr):
    c = xn_ref.shape[0]
    xb = xn_ref[...].astype(BF16)
    pos_keys = lax.broadcasted_iota(I32, (N_KEYS, c), 0)
    pos_cand = lax.broadcasted_iota(I32, (PEER_TOPK * PEER_TOPK, c), 0)
    pos16 = lax.broadcasted_iota(I32, (PEER_TOPK, c), 0)

    def head_body(h, carry):
        sv, si = [], []
        for half in range(2):
            hp = h * 2 + half
            wq = wq_ref[pl.ds(pl.multiple_of(hp * D_HALF, D_HALF), D_HALF), :]
            q_t = lax.dot_general(wq, xb, (((1,), (1,)), ((), ())), preferred_element_type=F32)
            sc = jnp.dot(keys_ref[hp], q_t.astype(BF16), preferred_element_type=F32)
            v, i = _top16_rows(sc, pos_keys)
            sv.append(v)
            si.append(i)
        cand = jnp.concatenate([sv[0][a:a + 1] + sv[1] for a in range(PEER_TOPK)], axis=0)
        fv, fpos = _top16_rows(cand, pos_cand)
        i0 = _select_row(si[0], fpos >> 4, pos16)
        i1 = _select_row(si[1], fpos & (PEER_TOPK - 1), pos16)
        e = jnp.exp(fv - fv[0:1])
        rows = pl.ds(pl.multiple_of(h * PEER_TOPK, PEER_TOPK), PEER_TOPK)
        idx_scr[rows, :] = i0 * N_KEYS + i1
        gate_scr[rows, :] = e / jnp.sum(e, axis=0, keepdims=True)
        return carry

    lax.fori_loop(0, PEER_HEADS, head_body, 0)
    idx_ref[...] = idx_scr[...].T
    gate_ref[...] = gate_scr[...].T


def _route(xn2, wq_t, keys):
    t = xn2.shape[0]
    c = TC_ROUTE
    return pl.pallas_call(
        _route_kernel,
        grid=(t // c,),
        in_specs=[
            pl.BlockSpec((c, D_MODEL), lambda i: (i, 0)),
            pl.BlockSpec((2 * PEER_HEADS * D_HALF, D_MODEL), lambda i: (0, 0)),
            pl.BlockSpec((2 * PEER_HEADS, N_KEYS, D_HALF), lambda i: (0, 0, 0)),
        ],
        out_specs=[pl.BlockSpec((c, PEER_SEL), lambda i: (i, 0)), pl.BlockSpec((c, PEER_SEL), lambda i: (i, 0))],
        out_shape=[jax.ShapeDtypeStruct((t, PEER_SEL), I32), jax.ShapeDtypeStruct((t, PEER_SEL), F32)],
        scratch_shapes=[pltpu.VMEM((PEER_SEL, c), I32), pltpu.VMEM((PEER_SEL, c), F32)],
        compiler_params=_tc_params("parallel"),
        name="route",
    )(xn2, wq_t, keys)


def _key_rows():
    key = lax.broadcasted_iota(I32, (N_KEYS, PEER_SEL), 0)
    return key, key.astype(F32).astype(BF16)


def _split_keys(idx_rows):
    first, second = idx_rows >> 7, idx_rows & (N_KEYS - 1)
    as_bf16 = lambda v: v.astype(F32).astype(BF16)
    return first, second, as_bf16(first), as_bf16(second)


def _expert_act_kernel(xn_ref, ut_ref, idx_ref, a_ref, grid_ref, xb_ref):
    k = pl.program_id(1)
    c = xn_ref.shape[0]

    @pl.when(k == 0)
    def _():
        xb_ref[...] = xn_ref[...].astype(BF16)

    dense = jnp.dot(xb_ref[...], ut_ref[...], preferred_element_type=F32)
    by_key = jnp.stack([dense[:, kk * N_KEYS:(kk + 1) * N_KEYS] for kk in range(KEYS_PER_BLOCK)], axis=0)
    second_rows = pl.ds(pl.multiple_of(k * KEYS_PER_BLOCK, KEYS_PER_BLOCK), KEYS_PER_BLOCK)
    grid_ref[:, second_rows, :] = jnp.swapaxes(by_key, 0, 1)

    @pl.when(k == pl.num_programs(1) - 1)
    def _():
        key, key_bf = _key_rows()
        one, zero = jnp.ones((), BF16), jnp.zeros((), BF16)

        def group(gi, carry):
            t0 = pl.multiple_of(gi * TOKENS_PER_TRIP, TOKENS_PER_TRIP)
            _, second, first_bf, _ = _split_keys(idx_ref[pl.ds(t0, TOKENS_PER_TRIP), :])
            rows = []
            for g in range(TOKENS_PER_TRIP):
                pick_first = jnp.where(key_bf == first_bf[g:g + 1], one, zero)
                picked = jnp.dot(grid_ref[t0 + g].astype(BF16), pick_first,
                                 preferred_element_type=F32)
                rows.append(jnp.sum(jnp.where(key == second[g:g + 1], picked, 0.0), axis=0, keepdims=True))
            a_ref[pl.ds(t0, TOKENS_PER_TRIP), :] = jnp.concatenate(rows, axis=0)
            return carry

        lax.fori_loop(0, c // TOKENS_PER_TRIP, group, 0)


def _expert_act(xn2, u_t, idx):
    t = xn2.shape[0]
    c = TC_EXPERT
    return pl.pallas_call(
        _expert_act_kernel,
        grid=(t // c, N_EXPERTS // EXPERT_BLOCK),
        in_specs=[
            pl.BlockSpec((c, D_MODEL), lambda i, k: (i, 0)),
            pl.BlockSpec((D_MODEL, EXPERT_BLOCK), lambda i, k: (0, k)),
            pl.BlockSpec((c, PEER_SEL), lambda i, k: (i, 0)),
        ],
        out_specs=pl.BlockSpec((c, PEER_SEL), lambda i, k: (i, 0)),
        out_shape=jax.ShapeDtypeStruct((t, PEER_SEL), F32),
        scratch_shapes=[pltpu.VMEM((c, N_KEYS, N_KEYS), F32), pltpu.VMEM((c, D_MODEL), BF16)],
        compiler_params=_tc_params("parallel", "arbitrary"),
        name="expert_act",
    )(xn2, u_t, idx)


def _gelu(a):
    return 0.5 * a * (1.0 + lax.erf(a * math.sqrt(0.5)))


def _expert_mix_kernel(normalize, idx_ref, a_ref, gate_ref, v_ref, h_ref, g_ref, o_ref, grid_ref, w_ref, acc_ref):
    k = pl.program_id(1)
    c = idx_ref.shape[0]

    @pl.when(k == 0)
    def _():
        acc_ref[...] = h_ref[...]
        w_ref[...] = gate_ref[...] * _gelu(a_ref[...])

        _, key_bf = _key_rows()
        one, zero = jnp.ones((), BF16), jnp.zeros((), BF16)

        def group(gi, carry):
            t0 = pl.multiple_of(gi * TOKENS_PER_TRIP, TOKENS_PER_TRIP)
            _, _, first_bf, second_bf = _split_keys(idx_ref[pl.ds(t0, TOKENS_PER_TRIP), :])
            w_bf = w_ref[pl.ds(t0, TOKENS_PER_TRIP), :].astype(BF16)
            for g0 in range(0, TOKENS_PER_TRIP, SUBLANES):
                mats = []
                for g in range(g0, g0 + SUBLANES):
                    weighted = jnp.where(key_bf == first_bf[g:g + 1], w_bf[g:g + 1], zero)
                    pick_second = jnp.where(key_bf == second_bf[g:g + 1], one, zero)
                    mats.append(lax.dot_general(weighted, pick_second, (((1,), (1,)), ((), ())),
                                                preferred_element_type=F32))
                grid_ref[:, pl.ds(t0 + g0, SUBLANES), :] = jnp.swapaxes(jnp.stack(mats, axis=0), 0, 1)
            return carry

        lax.fori_loop(0, c // TOKENS_PER_TRIP, group, 0)

    dense = jnp.concatenate([grid_ref[k * KEYS_PER_BLOCK + kk] for kk in range(KEYS_PER_BLOCK)], axis=1)
    acc_ref[...] += jnp.dot(dense.astype(BF16), v_ref[...], preferred_element_type=F32)

    @pl.when(k == pl.num_programs(1) - 1)
    def _():
        hh = acc_ref[...]
        o_ref[...] = hh * _rms_scale(hh) * g_ref[...] if normalize else hh


def _expert_mix(idx, a, gate, v_tab, h2, g, normalize):
    t = idx.shape[0]
    c = TC_EXPERT
    sel = pl.BlockSpec((c, PEER_SEL), lambda i, k: (i, 0))
    tok = pl.BlockSpec((c, D_MODEL), lambda i, k: (i, 0))
    return pl.pallas_call(
        functools.partial(_expert_mix_kernel, normalize),
        grid=(t // c, N_EXPERTS // EXPERT_BLOCK),
        in_specs=[sel, sel, sel, pl.BlockSpec((EXPERT_BLOCK, D_MODEL), lambda i, k: (k, 0)), tok,
                  pl.BlockSpec((1, D_MODEL), lambda i, k: (0, 0))],
        out_specs=tok,
        out_shape=jax.ShapeDtypeStruct((t, D_MODEL), F32),
        scratch_shapes=[pltpu.VMEM((N_KEYS, c, N_KEYS), F32), pltpu.VMEM((c, PEER_SEL), F32),
                        pltpu.VMEM((c, D_MODEL), F32)],
        compiler_params=_tc_params("parallel", "arbitrary"),
        name="expert_mix",
    )(idx, a, gate, v_tab, h2, g)


def _rearranged_in_proj(w_in, b_in):
    def cols(a):
        q = a[..., :Q_W]
        k = a[..., Q_W:Q_W + N_KV_HEADS * HEAD_DIM]
        v = a[..., Q_W + N_KV_HEADS * HEAD_DIM:Q_W + 2 * N_KV_HEADS * HEAD_DIM]
        rest = a[..., Q_W + 2 * N_KV_HEADS * HEAD_DIM:]
        dup = lambda m: jnp.concatenate(
            [m[..., hd * HEAD_DIM:(hd + 1) * HEAD_DIM] for hd in range(N_KV_HEADS) for _ in range(2)], axis=-1)
        return jnp.concatenate([q, dup(k), dup(v), rest], axis=-1)
    return cols(w_in).astype(BF16), cols(b_in)[None, :]


def kernel(x, ln_mix_g, w_in, b_in, attn_sinks, w_attn_up, w_pool_grp, pool_scale, w_pool_up, w_o,
           ln_ffn_g, w_query, sub_keys, u_experts, v_experts, ln_final_g):
    b, s, d = x.shape
    t = b * s
    depth = w_in.shape[0]
    h = x
    for l in range(depth):
        w_r, b_r = _rearranged_in_proj(w_in[l], b_in[l])
        q, kv, pz, gate = _inproj(h.reshape(t, d), ln_mix_g[l][None, :], w_r, b_r)
        o = _attention(attn_sinks[l], q.reshape(b, s, Q_W), kv.reshape(b, s, KV_COLS))
        h, xn2 = _merge(o, gate.reshape(b, s, 2 * d), h, pz.reshape(b, s, POOL_W),
                        w_attn_up[l].astype(BF16), w_pool_grp[l].astype(BF16), pool_scale[l][None, :],
                        w_pool_up[l].astype(BF16), w_o[l].astype(BF16), ln_ffn_g[l][None, :])
        xn2 = xn2.reshape(t, d)
        wq_t = w_query[l].T.astype(BF16)
        keys = sub_keys[l].reshape(2 * PEER_HEADS, N_KEYS, D_HALF).astype(BF16)
        idx, g = _route(xn2, wq_t, keys)
        u_t = u_experts[l].astype(BF16).reshape(N_KEYS, N_KEYS, d).transpose(2, 1, 0).reshape(d, N_EXPERTS)
        a = _expert_act(xn2, u_t, idx)
        last = l + 1 == depth
        h = _expert_mix(idx, a, g, v_experts[l].astype(BF16), h.reshape(t, d), ln_final_g[None, :], last)
        h = h.reshape(b, s, d)
    return h
```

```python
import functools
import math

import jax
import jax.numpy as jnp
from jax import lax
from jax.experimental import pallas as pl
from jax.experimental.pallas import tpu as pltpu

F32 = jnp.float32
BF16 = jnp.bfloat16
I32 = jnp.int32

D_MODEL = 1024
N_HEADS = 16
N_KV_HEADS = 2
GROUP = N_HEADS // N_KV_HEADS
HEAD_DIM = 64
BLOCK = 128
ATTN_SCALE = 1.0 / math.sqrt(HEAD_DIM)
NEG_INF = -1e30
POOL_WINDOWS = (2, 4, 8, 16)
POOL_GROUP = 128
POOL_W = len(POOL_WINDOWS) * POOL_GROUP
POOL_HIST = max(POOL_WINDOWS)
Q_W = N_HEADS * HEAD_DIM
PEER_HEADS = 8
N_KEYS = 128
N_EXPERTS = N_KEYS * N_KEYS
D_HALF = 128
PEER_TOPK = 16
PEER_SEL = PEER_HEADS * PEER_TOPK
EPS = 1e-5

VMEM_LIMIT_BYTES = 56 * 1024 * 1024
SUBLANES = 8
SUBLANE_BITREV = (0, 4, 2, 6, 1, 5, 3, 7)

C_Q = 0
C_K = C_Q + Q_W
C_V = C_K + 2 * BLOCK
C_P = C_V + 2 * BLOCK
C_G = C_P + POOL_W
C_END = C_G + 2 * D_MODEL
KV_COLS = C_P - C_K

TM_PROJ = 512
TS_MERGE = 512
TC_ROUTE = 256
TC_EXPERT = 512
EXPERT_BLOCK = 1024
KEYS_PER_BLOCK = EXPERT_BLOCK // N_KEYS
TOKENS_PER_TRIP = 16


def _tc_params(*sem):
    return pltpu.CompilerParams(dimension_semantics=sem, vmem_limit_bytes=VMEM_LIMIT_BYTES)


def _rms_scale(x):
    return lax.rsqrt(jnp.mean(x * x, axis=-1, keepdims=True) + EPS)


def _inproj_kernel(x_ref, g_ref, w_ref, b_ref, q_ref, kv_ref, pz_ref, gate_ref):
    x = x_ref[...]
    xn = (x * _rms_scale(x) * g_ref[...]).astype(BF16)

    def proj(lo, hi):
        return jnp.dot(xn, w_ref[:, lo:hi], preferred_element_type=F32) + b_ref[:, lo:hi]

    q_ref[...] = proj(C_Q, C_K).astype(BF16)
    kv_ref[...] = proj(C_K, C_P).astype(BF16)
    pz_ref[...] = proj(C_P, C_G)
    gate_ref[...] = proj(C_G, C_END).astype(BF16)


def _inproj(x2, g, w, b):
    t = x2.shape[0]
    row = lambda i: (i, 0)
    fixed = lambda i: (0, 0)
    return pl.pallas_call(
        _inproj_kernel,
        grid=(t // TM_PROJ,),
        in_specs=[
            pl.BlockSpec((TM_PROJ, D_MODEL), row),
            pl.BlockSpec((1, D_MODEL), fixed),
            pl.BlockSpec((D_MODEL, C_END), fixed),
            pl.BlockSpec((1, C_END), fixed),
        ],
        out_specs=[
            pl.BlockSpec((TM_PROJ, Q_W), row),
            pl.BlockSpec((TM_PROJ, KV_COLS), row),
            pl.BlockSpec((TM_PROJ, POOL_W), row),
            pl.BlockSpec((TM_PROJ, 2 * D_MODEL), row),
        ],
        out_shape=[
            jax.ShapeDtypeStruct((t, Q_W), BF16),
            jax.ShapeDtypeStruct((t, KV_COLS), BF16),
            jax.ShapeDtypeStruct((t, POOL_W), F32),
            jax.ShapeDtypeStruct((t, 2 * D_MODEL), BF16),
        ],
        compiler_params=_tc_params("parallel"),
        name="inproj",
    )(x2, g, w, b)


def _attn_kernel(sink_ref, q_ref, kvc_ref, kvp_ref, o_ref):
    j = pl.program_id(1)
    kvc = kvc_ref[0]
    kvp = kvp_ref[0]
    row = lax.broadcasted_iota(I32, (BLOCK, BLOCK), 0)
    col = lax.broadcasted_iota(I32, (BLOCK, BLOCK), 1)
    mask = jnp.concatenate([jnp.logical_and(col > row, j > 0), col <= row], axis=1)
    low = lax.broadcasted_iota(I32, (1, BLOCK), 1) < HEAD_DIM
    zero = jnp.zeros((), BF16)
    for hkv in range(N_KV_HEADS):
        kc = slice(hkv * BLOCK, (hkv + 1) * BLOCK)
        vc = slice(2 * BLOCK + hkv * BLOCK, 2 * BLOCK + (hkv + 1) * BLOCK)
        k2 = jnp.concatenate([kvp[:, kc], kvc[:, kc]], axis=0)
        v2 = jnp.concatenate([kvp[:, vc], kvc[:, vc]], axis=0)
        v_half = (jnp.where(low, v2, zero), jnp.where(low, zero, v2))
        for pair in range(GROUP // 2):
            c0 = (hkv * (GROUP // 2) + pair) * BLOCK
            q2 = q_ref[0, :, c0:c0 + BLOCK]
            o2 = jnp.zeros((BLOCK, BLOCK), F32)
            for half in range(2):
                head = hkv * GROUP + pair * 2 + half
                qm = jnp.where(low, q2, zero) if half == 0 else jnp.where(low, zero, q2)
                s = lax.dot_general(qm, k2, (((1,), (1,)), ((), ())), preferred_element_type=F32)
                s = jnp.where(mask, s * ATTN_SCALE, NEG_INF)
                sink = sink_ref[head]
                m = jnp.maximum(jnp.max(s, axis=-1, keepdims=True), sink)
                e = jnp.exp(s - m)
                denom = jnp.sum(e, axis=-1, keepdims=True) + jnp.exp(sink - m)
                p = (e * (1.0 / denom)).astype(BF16)
                o2 = o2 + jnp.dot(p, v_half[half], preferred_element_type=F32)
            o_ref[0, :, c0:c0 + BLOCK] = o2.astype(BF16)


def _attention(sinks, q3, kv3):
    b, s, _ = q3.shape
    return pl.pallas_call(
        _attn_kernel,
        grid=(b, s // BLOCK),
        in_specs=[
            pl.BlockSpec(memory_space=pltpu.SMEM),
            pl.BlockSpec((1, BLOCK, Q_W), lambda i, j: (i, j, 0)),
            pl.BlockSpec((1, BLOCK, KV_COLS), lambda i, j: (i, j, 0)),
            pl.BlockSpec((1, BLOCK, KV_COLS), lambda i, j: (i, jnp.maximum(j - 1, 0), 0)),
        ],
        out_specs=pl.BlockSpec((1, BLOCK, Q_W), lambda i, j: (i, j, 0)),
        out_shape=jax.ShapeDtypeStruct((b, s, Q_W), BF16),
        compiler_params=_tc_params("parallel", "arbitrary"),
        name="attention",
    )(sinks, q3, kv3, kv3)


def _sigmoid(x):
    return 1.0 / (1.0 + jnp.exp(-x))


def _merge_kernel(o_ref, gate_ref, x_ref, pz_ref, pzp_ref, wau_ref, wgrp_ref, scale_ref, wpu_ref,
                  wo_ref, g_ref, h_ref, xn_ref):
    j = pl.program_id(1)
    ts = pz_ref.shape[1]
    prev = jnp.where(j > 0, pzp_ref[0], 0.0)
    ext = jnp.concatenate([prev, pz_ref[0]], axis=0)
    t1 = (j * ts + 1 + lax.broadcasted_iota(I32, (ts, 1), 0)).astype(F32)
    ys = []
    for g, w in enumerate(POOL_WINDOWS):
        e = ext[:, g * POOL_GROUP:(g + 1) * POOL_GROUP]
        tsum = e
        span = 1
        while span < w:
            tsum = tsum + pltpu.roll(tsum, span, 0)
            span *= 2
        pooled = tsum[POOL_HIST:] / jnp.minimum(t1, float(w)) - e[POOL_HIST:]
        y = jnp.dot(pooled.astype(BF16), wgrp_ref[g], preferred_element_type=F32)
        ys.append((y * scale_ref[:, g * POOL_GROUP:(g + 1) * POOL_GROUP]).astype(BF16))
    y_p = jnp.dot(jnp.concatenate(ys, axis=1), wpu_ref[...], preferred_element_type=F32)
    y_a = jnp.dot(o_ref[0], wau_ref[...], preferred_element_type=F32)
    gate = gate_ref[0]
    merged = (_sigmoid(gate[:, :D_MODEL].astype(F32)) * y_a
              + _sigmoid(gate[:, D_MODEL:].astype(F32)) * y_p)
    h = x_ref[0] + jnp.dot(merged.astype(BF16), wo_ref[...], preferred_element_type=F32)
    h_ref[0] = h
    xn_ref[0] = h * _rms_scale(h) * g_ref[...]


def _merge(o3, gate3, x, pz3, wau, wgrp, scale, wpu, wo, g):
    b, s, _ = x.shape
    ts = TS_MERGE
    hist_blocks = ts // POOL_HIST
    tile = lambda i, j: (i, j, 0)
    fixed2 = lambda i, j: (0, 0)
    return pl.pallas_call(
        _merge_kernel,
        grid=(b, s // ts),
        in_specs=[
            pl.BlockSpec((1, ts, Q_W), tile),
            pl.BlockSpec((1, ts, 2 * D_MODEL), tile),
            pl.BlockSpec((1, ts, D_MODEL), tile),
            pl.BlockSpec((1, ts, POOL_W), tile),
            pl.BlockSpec((1, POOL_HIST, POOL_W), lambda i, j: (i, jnp.maximum(j * hist_blocks - 1, 0), 0)),
            pl.BlockSpec((Q_W, D_MODEL), fixed2),
            pl.BlockSpec((len(POOL_WINDOWS), POOL_GROUP, POOL_GROUP), lambda i, j: (0, 0, 0)),
            pl.BlockSpec((1, POOL_W), fixed2),
            pl.BlockSpec((POOL_W, D_MODEL), fixed2),
            pl.BlockSpec((D_MODEL, D_MODEL), fixed2),
            pl.BlockSpec((1, D_MODEL), fixed2),
        ],
        out_specs=[pl.BlockSpec((1, ts, D_MODEL), tile), pl.BlockSpec((1, ts, D_MODEL), tile)],
        out_shape=[jax.ShapeDtypeStruct((b, s, D_MODEL), F32), jax.ShapeDtypeStruct((b, s, D_MODEL), F32)],
        compiler_params=_tc_params("parallel", "arbitrary"),
        name="merge",
    )(o3, gate3, x, pz3, pz3, wau, wgrp, scale, wpu, wo, g)


def _first_max(groups, ordered):
    while len(groups) > 1:
        merged = []
        for i in range(0, len(groups) - 1, 2):
            (vl, pl_), (vr, pr) = groups[i], groups[i + 1]
            merged.append((jnp.maximum(vl, vr), jnp.where(vl >= vr, pl_, pr)))
        if len(groups) % 2:
            merged.append(groups[-1])
        groups = merged
    v, p = groups[0]
    rows = SUBLANES
    while rows > 1:
        rows //= 2
        vl, vr, pl_, pr = v[:rows], v[rows:2 * rows], p[:rows], p[rows:2 * rows]
        take = (vl >= vr) if ordered else jnp.logical_or(vl > vr, jnp.logical_and(vl == vr, pl_ < pr))
        v, p = jnp.where(take, vl, vr), jnp.where(take, pl_, pr)
    return v, p


def _top16_rows(groups, ordered):
    vals, poss = [], []
    for _ in range(PEER_TOPK):
        m, am = _first_max(groups, ordered)
        vals.append(m)
        poss.append(am)
        groups = [(jnp.where(p == am, -jnp.inf, v), p) for v, p in groups]
    return jnp.concatenate(vals, axis=0), jnp.concatenate(poss, axis=0)


def _select_row(table, sel, pos16):
    out = []
    for k in range(PEER_TOPK):
        out.append(jnp.sum(jnp.where(pos16 == sel[k:k + 1], table, 0), axis=0, keepdims=True))
    return jnp.concatenate(out, axis=0)


def _route_kernel(xn_ref, wq_ref, keys_ref, idx_ref, gate_ref, idx_scr, gate_scr):
    c = xn_ref.shape[0]
    xb = xn_ref[...].astype(BF16)
    sub = lax.broadcasted_iota(I32, (SUBLANES, c), 0)
    sub_rev = ((sub & 1) << 2) | (sub & 2) | ((sub & 4) >> 2)
    key_pos = [(sub_rev * (N_KEYS // SUBLANES) + g).astype(F32) for g in range(N_KEYS // SUBLANES)]
    pos16 = lax.broadcasted_iota(I32, (PEER_TOPK, c), 0)
    cand_groups = [(a, b0) for a in range(PEER_TOPK) for b0 in range(0, PEER_TOPK // (a + 1), SUBLANES)]

    def head_body(h, carry):
        sv, si = [], []
        for half in range(2):
            hp = h * 2 + half
            wq = wq_ref[pl.ds(pl.multiple_of(hp * D_HALF, D_HALF), D_HALF), :]
            q_t = lax.dot_general(wq, xb, (((1,), (1,)), ((), ())), preferred_element_type=F32)
            sc = jnp.dot(keys_ref[hp], q_t.astype(BF16), preferred_element_type=F32)
            groups = [(sc[g * SUBLANES:(g + 1) * SUBLANES], key_pos[g]) for g in range(N_KEYS // SUBLANES)]
            v, p = _top16_rows(groups, ordered=True)
            sv.append(v)
            si.append(p.astype(I32))
        groups = []
        for a, b0 in cand_groups:
            val = sv[0][a:a + 1] + sv[1][b0:b0 + SUBLANES]
            val = jnp.where(sub + b0 < PEER_TOPK // (a + 1), val, -jnp.inf)
            groups.append((val, (sub + (a * PEER_TOPK + b0)).astype(F32)))
        fv, fpos = _top16_rows(groups, ordered=False)
        fpos = fpos.astype(I32)
        i0 = _select_row(si[0], fpos >> 4, pos16)
        i1 = _select_row(si[1], fpos & (PEER_TOPK - 1), pos16)
        e = jnp.exp(fv - fv[0:1])
        rows = pl.ds(pl.multiple_of(h * PEER_TOPK, PEER_TOPK), PEER_TOPK)
        idx_scr[rows, :] = i0 * N_KEYS + i1
        gate_scr[rows, :] = e / jnp.sum(e, axis=0, keepdims=True)
        return carry

    lax.fori_loop(0, PEER_HEADS, head_body, 0)
    idx_ref[...] = idx_scr[...].T
    gate_ref[...] = gate_scr[...].T


def _route(xn2, wq_t, keys):
    t = xn2.shape[0]
    c = TC_ROUTE
    return pl.pallas_call(
        _route_kernel,
        grid=(t // c,),
        in_specs=[
            pl.BlockSpec((c, D_MODEL), lambda i: (i, 0)),
            pl.BlockSpec((2 * PEER_HEADS * D_HALF, D_MODEL), lambda i: (0, 0)),
            pl.BlockSpec((2 * PEER_HEADS, N_KEYS, D_HALF), lambda i: (0, 0, 0)),
        ],
        out_specs=[pl.BlockSpec((c, PEER_SEL), lambda i: (i, 0)), pl.BlockSpec((c, PEER_SEL), lambda i: (i, 0))],
        out_shape=[jax.ShapeDtypeStruct((t, PEER_SEL), I32), jax.ShapeDtypeStruct((t, PEER_SEL), F32)],
        scratch_shapes=[pltpu.VMEM((PEER_SEL, c), I32), pltpu.VMEM((PEER_SEL, c), F32)],
        compiler_params=_tc_params("parallel"),
        name="route",
    )(xn2, wq_t, keys)


def _key_rows():
    key = lax.broadcasted_iota(I32, (N_KEYS, PEER_SEL), 0)
    return key, key.astype(F32).astype(BF16)


def _split_keys(idx_rows):
    first, second = idx_rows >> 7, idx_rows & (N_KEYS - 1)
    as_bf16 = lambda v: v.astype(F32).astype(BF16)
    return first, second, as_bf16(first), as_bf16(second)


def _expert_act_kernel(xn_ref, ut_ref, idx_ref, a_ref, grid_ref, xb_ref):
    k = pl.program_id(1)
    c = xn_ref.shape[0]

    @pl.when(k == 0)
    def _():
        xb_ref[...] = xn_ref[...].astype(BF16)

    dense = jnp.dot(xb_ref[...], ut_ref[...], preferred_element_type=F32)
    by_key = jnp.stack([dense[:, kk * N_KEYS:(kk + 1) * N_KEYS] for kk in range(KEYS_PER_BLOCK)], axis=0)
    second_rows = pl.ds(pl.multiple_of(k * KEYS_PER_BLOCK, KEYS_PER_BLOCK), KEYS_PER_BLOCK)
    grid_ref[:, second_rows, :] = jnp.swapaxes(by_key, 0, 1)

    @pl.when(k == pl.num_programs(1) - 1)
    def _():
        key, key_bf = _key_rows()
        one, zero = jnp.ones((), BF16), jnp.zeros((), BF16)

        def group(gi, carry):
            t0 = pl.multiple_of(gi * TOKENS_PER_TRIP, TOKENS_PER_TRIP)
            _, second, first_bf, _ = _split_keys(idx_ref[pl.ds(t0, TOKENS_PER_TRIP), :])
            rows = []
            for g in range(TOKENS_PER_TRIP):
                pick_first = jnp.where(key_bf == first_bf[g:g + 1], one, zero)
                picked = jnp.dot(grid_ref[t0 + g].astype(BF16), pick_first,
                                 preferred_element_type=F32)
                rows.append(jnp.sum(jnp.where(key == second[g:g + 1], picked, 0.0), axis=0, keepdims=True))
            a_ref[pl.ds(t0, TOKENS_PER_TRIP), :] = jnp.concatenate(rows, axis=0)
            return carry

        lax.fori_loop(0, c // TOKENS_PER_TRIP, group, 0)


def _expert_act(xn2, u_t, idx):
    t = xn2.shape[0]
    c = TC_EXPERT
    return pl.pallas_call(
        _expert_act_kernel,
        grid=(t // c, N_EXPERTS // EXPERT_BLOCK),
        in_specs=[
            pl.BlockSpec((c, D_MODEL), lambda i, k: (i, 0)),
            pl.BlockSpec((D_MODEL, EXPERT_BLOCK), lambda i, k: (0, k)),
            pl.BlockSpec((c, PEER_SEL), lambda i, k: (i, 0)),
        ],
        out_specs=pl.BlockSpec((c, PEER_SEL), lambda i, k: (i, 0)),
        out_shape=jax.ShapeDtypeStruct((t, PEER_SEL), F32),
        scratch_shapes=[pltpu.VMEM((c, N_KEYS, N_KEYS), F32), pltpu.VMEM((c, D_MODEL), BF16)],
        compiler_params=_tc_params("parallel", "arbitrary"),
        name="expert_act",
    )(xn2, u_t, idx)


def _gelu(a):
    return 0.5 * a * (1.0 + lax.erf(a * math.sqrt(0.5)))


def _expert_mix_kernel(normalize, idx_ref, a_ref, gate_ref, v_ref, h_ref, g_ref, o_ref, grid_ref, w_ref, acc_ref):
    k = pl.program_id(1)
    c = idx_ref.shape[0]

    @pl.when(k == 0)
    def _():
        acc_ref[...] = h_ref[...]
        w_ref[...] = gate_ref[...] * _gelu(a_ref[...])

        _, key_bf = _key_rows()
        one, zero = jnp.ones((), BF16), jnp.zeros((), BF16)

        def group(gi, carry):
            t0 = pl.multiple_of(gi * TOKENS_PER_TRIP, TOKENS_PER_TRIP)
            _, _, first_bf, second_bf = _split_keys(idx_ref[pl.ds(t0, TOKENS_PER_TRIP), :])
            w_bf = w_ref[pl.ds(t0, TOKENS_PER_TRIP), :].astype(BF16)
            for g0 in range(0, TOKENS_PER_TRIP, SUBLANES):
                mats = []
                for g in range(g0, g0 + SUBLANES):
                    weighted = jnp.where(key_bf == first_bf[g:g + 1], w_bf[g:g + 1], zero)
                    pick_second = jnp.where(key_bf == second_bf[g:g + 1], one, zero)
                    mats.append(lax.dot_general(weighted, pick_second, (((1,), (1,)), ((), ())),
                                                preferred_element_type=F32))
                grid_ref[:, pl.ds(t0 + g0, SUBLANES), :] = jnp.swapaxes(jnp.stack(mats, axis=0), 0, 1)
            return carry

        lax.fori_loop(0, c // TOKENS_PER_TRIP, group, 0)

    dense = jnp.concatenate([grid_ref[k * KEYS_PER_BLOCK + kk] for kk in range(KEYS_PER_BLOCK)], axis=1)
    acc_ref[...] += jnp.dot(dense.astype(BF16), v_ref[...], preferred_element_type=F32)

    @pl.when(k == pl.num_programs(1) - 1)
    def _():
        hh = acc_ref[...]
        o_ref[...] = hh * _rms_scale(hh) * g_ref[...] if normalize else hh


def _expert_mix(idx, a, gate, v_tab, h2, g, normalize):
    t = idx.shape[0]
    c = TC_EXPERT
    sel = pl.BlockSpec((c, PEER_SEL), lambda i, k: (i, 0))
    tok = pl.BlockSpec((c, D_MODEL), lambda i, k: (i, 0))
    return pl.pallas_call(
        functools.partial(_expert_mix_kernel, normalize),
        grid=(t // c, N_EXPERTS // EXPERT_BLOCK),
        in_specs=[sel, sel, sel, pl.BlockSpec((EXPERT_BLOCK, D_MODEL), lambda i, k: (k, 0)), tok,
                  pl.BlockSpec((1, D_MODEL), lambda i, k: (0, 0))],
        out_specs=tok,
        out_shape=jax.ShapeDtypeStruct((t, D_MODEL), F32),
        scratch_shapes=[pltpu.VMEM((N_KEYS, c, N_KEYS), F32), pltpu.VMEM((c, PEER_SEL), F32),
                        pltpu.VMEM((c, D_MODEL), F32)],
        compiler_params=_tc_params("parallel", "arbitrary"),
        name="expert_mix",
    )(idx, a, gate, v_tab, h2, g)


def _rearranged_in_proj(w_in, b_in):
    def cols(a):
        q = a[..., :Q_W]
        k = a[..., Q_W:Q_W + N_KV_HEADS * HEAD_DIM]
        v = a[..., Q_W + N_KV_HEADS * HEAD_DIM:Q_W + 2 * N_KV_HEADS * HEAD_DIM]
        rest = a[..., Q_W + 2 * N_KV_HEADS * HEAD_DIM:]
        dup = lambda m: jnp.concatenate(
            [m[..., hd * HEAD_DIM:(hd + 1) * HEAD_DIM] for hd in range(N_KV_HEADS) for _ in range(2)], axis=-1)
        return jnp.concatenate([q, dup(k), dup(v), rest], axis=-1)
    return cols(w_in).astype(BF16), cols(b_in)[None, :]


def kernel(x, ln_mix_g, w_in, b_in, attn_sinks, w_attn_up, w_pool_grp, pool_scale, w_pool_up, w_o,
           ln_ffn_g, w_query, sub_keys, u_experts, v_experts, ln_final_g):
    b, s, d = x.shape
    t = b * s
    depth = w_in.shape[0]
    h = x
    for l in range(depth):
        w_r, b_r = _rearranged_in_proj(w_in[l], b_in[l])
        q, kv, pz, gate = _inproj(h.reshape(t, d), ln_mix_g[l][None, :], w_r, b_r)
        o = _attention(attn_sinks[l], q.reshape(b, s, Q_W), kv.reshape(b, s, KV_COLS))
        h, xn2 = _merge(o, gate.reshape(b, s, 2 * d), h, pz.reshape(b, s, POOL_W),
                        w_attn_up[l].astype(BF16), w_pool_grp[l].astype(BF16), pool_scale[l][None, :],
                        w_pool_up[l].astype(BF16), w_o[l].astype(BF16), ln_ffn_g[l][None, :])
        xn2 = xn2.reshape(t, d)
        wq_t = w_query[l].T.astype(BF16)
        keys = sub_keys[l].reshape(2 * PEER_HEADS, SUBLANES, N_KEYS // SUBLANES, D_HALF)
        keys = keys[:, jnp.array(SUBLANE_BITREV)].transpose(0, 2, 1, 3)
        keys = keys.reshape(2 * PEER_HEADS, N_KEYS, D_HALF).astype(BF16)
        idx, g = _route(xn2, wq_t, keys)
        u_t = u_experts[l].astype(BF16).reshape(N_KEYS, N_KEYS, d).transpose(2, 1, 0).reshape(d, N_EXPERTS)
        a = _expert_act(xn2, u_t, idx)
        last = l + 1 == depth
        h = _expert_mix(idx, a, g, v_experts[l].astype(BF16), h.reshape(t, d), ln_final_g[None, :], last)
        h = h.reshape(b, s, d)
    return h
```

```python
import functools
import math

import jax
import jax.numpy as jnp
from jax import lax
from jax.experimental import pallas as pl
from jax.experimental.pallas import tpu as pltpu

F32 = jnp.float32
BF16 = jnp.bfloat16
I32 = jnp.int32

D_MODEL = 1024
N_HEADS = 16
N_KV_HEADS = 2
GROUP = N_HEADS // N_KV_HEADS
HEAD_DIM = 64
BLOCK = 128
ATTN_SCALE = 1.0 / math.sqrt(HEAD_DIM)
NEG_INF = -1e30
POOL_WINDOWS = (2, 4, 8, 16)
POOL_GROUP = 128
POOL_W = len(POOL_WINDOWS) * POOL_GROUP
POOL_HIST = max(POOL_WINDOWS)
Q_W = N_HEADS * HEAD_DIM
PEER_HEADS = 8
N_KEYS = 128
N_EXPERTS = N_KEYS * N_KEYS
D_HALF = 128
PEER_TOPK = 16
PEER_SEL = PEER_HEADS * PEER_TOPK
EPS = 1e-5

VMEM_LIMIT_BYTES = 56 * 1024 * 1024
SUBLANES = 8
LANES = 128
SUBLANE_BITREV = (0, 4, 2, 6, 1, 5, 3, 7)

C_Q = 0
C_K = C_Q + Q_W
C_V = C_K + 2 * BLOCK
C_P = C_V + 2 * BLOCK
C_G = C_P + POOL_W
C_END = C_G + 2 * D_MODEL
KV_COLS = C_P - C_K

TM_PROJ = 512
TS_MERGE = 512
ROUTE_SPLIT = 1
TC_ROUTE = 256
TC_EXPERT = 512
EXPERT_BLOCK = 1024
KEYS_PER_BLOCK = EXPERT_BLOCK // N_KEYS
TOKENS_PER_TRIP = 16


def _tc_params(*sem):
    return pltpu.CompilerParams(dimension_semantics=sem, vmem_limit_bytes=VMEM_LIMIT_BYTES)


def _rms_scale(x):
    return lax.rsqrt(jnp.mean(x * x, axis=-1, keepdims=True) + EPS)


def _inproj_kernel(x_ref, g_ref, w_ref, b_ref, q_ref, kv_ref, pz_ref, gate_ref):
    x = x_ref[...]
    xn = (x * _rms_scale(x) * g_ref[...]).astype(BF16)

    def proj(lo, hi):
        return jnp.dot(xn, w_ref[:, lo:hi], preferred_element_type=F32) + b_ref[:, lo:hi]

    q_ref[...] = proj(C_Q, C_K).astype(BF16)
    kv_ref[...] = proj(C_K, C_P).astype(BF16)
    pz_ref[...] = proj(C_P, C_G)
    gate_ref[...] = proj(C_G, C_END).astype(BF16)


def _inproj(x2, g, w, b):
    t = x2.shape[0]
    row = lambda i: (i, 0)
    fixed = lambda i: (0, 0)
    return pl.pallas_call(
        _inproj_kernel,
        grid=(t // TM_PROJ,),
        in_specs=[
            pl.BlockSpec((TM_PROJ, D_MODEL), row),
            pl.BlockSpec((1, D_MODEL), fixed),
            pl.BlockSpec((D_MODEL, C_END), fixed),
            pl.BlockSpec((1, C_END), fixed),
        ],
        out_specs=[
            pl.BlockSpec((TM_PROJ, Q_W), row),
            pl.BlockSpec((TM_PROJ, KV_COLS), row),
            pl.BlockSpec((TM_PROJ, POOL_W), row),
            pl.BlockSpec((TM_PROJ, 2 * D_MODEL), row),
        ],
        out_shape=[
            jax.ShapeDtypeStruct((t, Q_W), BF16),
            jax.ShapeDtypeStruct((t, KV_COLS), BF16),
            jax.ShapeDtypeStruct((t, POOL_W), F32),
            jax.ShapeDtypeStruct((t, 2 * D_MODEL), BF16),
        ],
        compiler_params=_tc_params("parallel"),
        name="inproj",
    )(x2, g, w, b)


def _attn_kernel(sink_ref, q_ref, kvc_ref, kvp_ref, o_ref):
    j = pl.program_id(1)
    kvc = kvc_ref[0]
    kvp = kvp_ref[0]
    row = lax.broadcasted_iota(I32, (BLOCK, BLOCK), 0)
    col = lax.broadcasted_iota(I32, (BLOCK, BLOCK), 1)
    mask = jnp.concatenate([jnp.logical_and(col > row, j > 0), col <= row], axis=1)
    low = lax.broadcasted_iota(I32, (1, BLOCK), 1) < HEAD_DIM
    zero = jnp.zeros((), BF16)
    for hkv in range(N_KV_HEADS):
        kc = slice(hkv * BLOCK, (hkv + 1) * BLOCK)
        vc = slice(2 * BLOCK + hkv * BLOCK, 2 * BLOCK + (hkv + 1) * BLOCK)
        k2 = jnp.concatenate([kvp[:, kc], kvc[:, kc]], axis=0)
        v2 = jnp.concatenate([kvp[:, vc], kvc[:, vc]], axis=0)
        v_half = (jnp.where(low, v2, zero), jnp.where(low, zero, v2))
        for pair in range(GROUP // 2):
            c0 = (hkv * (GROUP // 2) + pair) * BLOCK
            q2 = q_ref[0, :, c0:c0 + BLOCK]
            o2 = jnp.zeros((BLOCK, BLOCK), F32)
            for half in range(2):
                head = hkv * GROUP + pair * 2 + half
                qm = jnp.where(low, q2, zero) if half == 0 else jnp.where(low, zero, q2)
                s = lax.dot_general(qm, k2, (((1,), (1,)), ((), ())), preferred_element_type=F32)
                s = jnp.where(mask, s * ATTN_SCALE, NEG_INF)
                sink = sink_ref[head]
                m = jnp.maximum(jnp.max(s, axis=-1, keepdims=True), sink)
                e = jnp.exp(s - m)
                denom = jnp.sum(e, axis=-1, keepdims=True) + jnp.exp(sink - m)
                p = (e * (1.0 / denom)).astype(BF16)
                o2 = o2 + jnp.dot(p, v_half[half], preferred_element_type=F32)
            o_ref[0, :, c0:c0 + BLOCK] = o2.astype(BF16)


def _attention(sinks, q3, kv3):
    b, s, _ = q3.shape
    return pl.pallas_call(
        _attn_kernel,
        grid=(b, s // BLOCK),
        in_specs=[
            pl.BlockSpec(memory_space=pltpu.SMEM),
            pl.BlockSpec((1, BLOCK, Q_W), lambda i, j: (i, j, 0)),
            pl.BlockSpec((1, BLOCK, KV_COLS), lambda i, j: (i, j, 0)),
            pl.BlockSpec((1, BLOCK, KV_COLS), lambda i, j: (i, jnp.maximum(j - 1, 0), 0)),
        ],
        out_specs=pl.BlockSpec((1, BLOCK, Q_W), lambda i, j: (i, j, 0)),
        out_shape=jax.ShapeDtypeStruct((b, s, Q_W), BF16),
        compiler_params=_tc_params("parallel", "arbitrary"),
        name="attention",
    )(sinks, q3, kv3, kv3)


def _sigmoid(x):
    return 1.0 / (1.0 + jnp.exp(-x))


def _merge_kernel(o_ref, gate_ref, x_ref, pz_ref, pzp_ref, wau_ref, wgrp_ref, scale_ref, wpu_ref,
                  wo_ref, g_ref, h_ref, xn_ref):
    j = pl.program_id(1)
    ts = pz_ref.shape[1]
    prev = jnp.where(j > 0, pzp_ref[0], 0.0)
    ext = jnp.concatenate([prev, pz_ref[0]], axis=0)
    t1 = (j * ts + 1 + lax.broadcasted_iota(I32, (ts, 1), 0)).astype(F32)
    ys = []
    for g, w in enumerate(POOL_WINDOWS):
        e = ext[:, g * POOL_GROUP:(g + 1) * POOL_GROUP]
        tsum = e
        span = 1
        while span < w:
            tsum = tsum + pltpu.roll(tsum, span, 0)
            span *= 2
        pooled = tsum[POOL_HIST:] / jnp.minimum(t1, float(w)) - e[POOL_HIST:]
        y = jnp.dot(pooled.astype(BF16), wgrp_ref[g], preferred_element_type=F32)
        ys.append((y * scale_ref[:, g * POOL_GROUP:(g + 1) * POOL_GROUP]).astype(BF16))
    y_p = jnp.dot(jnp.concatenate(ys, axis=1), wpu_ref[...], preferred_element_type=F32)
    y_a = jnp.dot(o_ref[0], wau_ref[...], preferred_element_type=F32)
    gate = gate_ref[0]
    merged = (_sigmoid(gate[:, :D_MODEL].astype(F32)) * y_a
              + _sigmoid(gate[:, D_MODEL:].astype(F32)) * y_p)
    h = x_ref[0] + jnp.dot(merged.astype(BF16), wo_ref[...], preferred_element_type=F32)
    h_ref[0] = h
    xn_ref[0] = h * _rms_scale(h) * g_ref[...]


def _merge(o3, gate3, x, pz3, wau, wgrp, scale, wpu, wo, g):
    b, s, _ = x.shape
    ts = TS_MERGE
    hist_blocks = ts // POOL_HIST
    tile = lambda i, j: (i, j, 0)
    fixed2 = lambda i, j: (0, 0)
    return pl.pallas_call(
        _merge_kernel,
        grid=(b, s // ts),
        in_specs=[
            pl.BlockSpec((1, ts, Q_W), tile),
            pl.BlockSpec((1, ts, 2 * D_MODEL), tile),
            pl.BlockSpec((1, ts, D_MODEL), tile),
            pl.BlockSpec((1, ts, POOL_W), tile),
            pl.BlockSpec((1, POOL_HIST, POOL_W), lambda i, j: (i, jnp.maximum(j * hist_blocks - 1, 0), 0)),
            pl.BlockSpec((Q_W, D_MODEL), fixed2),
            pl.BlockSpec((len(POOL_WINDOWS), POOL_GROUP, POOL_GROUP), lambda i, j: (0, 0, 0)),
            pl.BlockSpec((1, POOL_W), fixed2),
            pl.BlockSpec((POOL_W, D_MODEL), fixed2),
            pl.BlockSpec((D_MODEL, D_MODEL), fixed2),
            pl.BlockSpec((1, D_MODEL), fixed2),
        ],
        out_specs=[pl.BlockSpec((1, ts, D_MODEL), tile), pl.BlockSpec((1, ts, D_MODEL), tile)],
        out_shape=[jax.ShapeDtypeStruct((b, s, D_MODEL), F32), jax.ShapeDtypeStruct((b, s, D_MODEL), F32)],
        compiler_params=_tc_params("parallel", "arbitrary"),
        name="merge",
    )(o3, gate3, x, pz3, pz3, wau, wgrp, scale, wpu, wo, g)


def _first_max(groups, ordered):
    while len(groups) > 1:
        merged = []
        for i in range(0, len(groups) - 1, 2):
            (vl, pl_), (vr, pr) = groups[i], groups[i + 1]
            merged.append((jnp.maximum(vl, vr), jnp.where(vl >= vr, pl_, pr)))
        if len(groups) % 2:
            merged.append(groups[-1])
        groups = merged
    v, p = groups[0]
    rows = SUBLANES
    while rows > 1:
        rows //= 2
        vl, vr, pl_, pr = v[:rows], v[rows:2 * rows], p[:rows], p[rows:2 * rows]
        take = (vl >= vr) if ordered else jnp.logical_or(vl > vr, jnp.logical_and(vl == vr, pl_ < pr))
        v, p = jnp.where(take, vl, vr), jnp.where(take, pl_, pr)
    return v, p


def _top16_rows(groups, ordered):
    vals, poss = [], []
    for _ in range(PEER_TOPK):
        m, am = _first_max(groups, ordered)
        vals.append(m)
        poss.append(am)
        groups = [(jnp.where(p == am, -jnp.inf, v), p) for v, p in groups]
    return jnp.concatenate(vals, axis=0), jnp.concatenate(poss, axis=0)


def _select_row(table, sel, pos16):
    out = []
    for k in range(PEER_TOPK):
        out.append(jnp.sum(jnp.where(pos16 == sel[k:k + 1], table, 0), axis=0, keepdims=True))
    return jnp.concatenate(out, axis=0)


def _route_head(xb, wq_ref, keys_ref, h, idx_scr, gate_scr, tokens):
    c = xb.shape[0]
    sub = lax.broadcasted_iota(I32, (SUBLANES, LANES), 0)
    sub_rev = ((sub & 1) << 2) | (sub & 2) | ((sub & 4) >> 2)
    key_pos = [(sub_rev * (N_KEYS // SUBLANES) + g).astype(F32) for g in range(N_KEYS // SUBLANES)]
    pos16 = lax.broadcasted_iota(I32, (PEER_TOPK, LANES), 0)
    cand_groups = [(a, b0) for a in range(PEER_TOPK) for b0 in range(0, PEER_TOPK // (a + 1), SUBLANES)]

    scores = []
    for half in range(2):
        hp = h * 2 + half
        wq = wq_ref[pl.ds(pl.multiple_of(hp * D_HALF, D_HALF), D_HALF), :]
        q_t = lax.dot_general(wq, xb, (((1,), (1,)), ((), ())), preferred_element_type=F32)
        scores.append(jnp.dot(keys_ref[hp], q_t.astype(BF16), preferred_element_type=F32))

    idx_out, gate_out = [], []
    for l0 in range(0, c, LANES):
        sv, si = [], []
        for sc in scores:
            groups = [(sc[g * SUBLANES:(g + 1) * SUBLANES, l0:l0 + LANES], key_pos[g])
                      for g in range(N_KEYS // SUBLANES)]
            v, p = _top16_rows(groups, ordered=True)
            sv.append(v)
            si.append(p.astype(I32))
        groups = []
        for a, b0 in cand_groups:
            val = sv[0][a:a + 1] + sv[1][b0:b0 + SUBLANES]
            val = jnp.where(sub + b0 < PEER_TOPK // (a + 1), val, -jnp.inf)
            groups.append((val, (sub + (a * PEER_TOPK + b0)).astype(F32)))
        fv, fpos = _top16_rows(groups, ordered=False)
        fpos = fpos.astype(I32)
        i0 = _select_row(si[0], fpos >> 4, pos16)
        i1 = _select_row(si[1], fpos & (PEER_TOPK - 1), pos16)
        e = jnp.exp(fv - fv[0:1])
        idx_out.append(i0 * N_KEYS + i1)
        gate_out.append(e / jnp.sum(e, axis=0, keepdims=True))
    rows = pl.ds(pl.multiple_of(h * PEER_TOPK, PEER_TOPK), PEER_TOPK)
    idx_scr[rows, tokens] = jnp.concatenate(idx_out, axis=1)
    gate_scr[rows, tokens] = jnp.concatenate(gate_out, axis=1)


def _key_rows():
    key = lax.broadcasted_iota(I32, (N_KEYS, PEER_SEL), 0)
    return key, key.astype(F32).astype(BF16)


def _split_keys(idx_rows):
    first, second = idx_rows >> 7, idx_rows & (N_KEYS - 1)
    as_bf16 = lambda v: v.astype(F32).astype(BF16)
    return first, second, as_bf16(first), as_bf16(second)


def _route_act_kernel(xn_ref, ut_ref, wq_ref, keys_ref, idx_ref, gate_ref, a_ref, grid_ref, xb_ref,
                      idx_scr, gate_scr):
    k = pl.program_id(1)
    c = xn_ref.shape[0]

    @pl.when(k == 0)
    def _():
        xb_ref[...] = xn_ref[...].astype(BF16)

    chunk_tokens = pl.ds(pl.multiple_of((k // PEER_HEADS) * TC_ROUTE, TC_ROUTE), TC_ROUTE)
    _route_head(xb_ref[chunk_tokens, :], wq_ref, keys_ref, k % PEER_HEADS, idx_scr, gate_scr, chunk_tokens)

    part = c // ROUTE_SPLIT
    part_tokens = pl.ds(pl.multiple_of((k % ROUTE_SPLIT) * part, part), part)
    dense = jnp.dot(xb_ref[part_tokens, :], ut_ref[...], preferred_element_type=F32)
    by_key = jnp.stack([dense[:, kk * N_KEYS:(kk + 1) * N_KEYS] for kk in range(KEYS_PER_BLOCK)], axis=0)
    second_rows = pl.ds(pl.multiple_of((k // ROUTE_SPLIT) * KEYS_PER_BLOCK, KEYS_PER_BLOCK), KEYS_PER_BLOCK)
    grid_ref[part_tokens, second_rows, :] = jnp.swapaxes(by_key, 0, 1)

    @pl.when(k == pl.num_programs(1) - 1)
    def _():
        idx_ref[...] = idx_scr[...].T
        gate_ref[...] = gate_scr[...].T
        key, key_bf = _key_rows()
        one, zero = jnp.ones((), BF16), jnp.zeros((), BF16)

        def group(gi, carry):
            t0 = pl.multiple_of(gi * TOKENS_PER_TRIP, TOKENS_PER_TRIP)
            _, second, first_bf, _ = _split_keys(idx_ref[pl.ds(t0, TOKENS_PER_TRIP), :])
            rows = []
            for g in range(TOKENS_PER_TRIP):
                pick_first = jnp.where(key_bf == first_bf[g:g + 1], one, zero)
                picked = jnp.dot(grid_ref[t0 + g].astype(BF16), pick_first,
                                 preferred_element_type=F32)
                rows.append(jnp.sum(jnp.where(key == second[g:g + 1], picked, 0.0), axis=0, keepdims=True))
            a_ref[pl.ds(t0, TOKENS_PER_TRIP), :] = jnp.concatenate(rows, axis=0)
            return carry

        lax.fori_loop(0, c // TOKENS_PER_TRIP, group, 0)


def _route_act(xn2, u_t, wq_t, keys):
    t = xn2.shape[0]
    c = TC_EXPERT
    steps = ROUTE_SPLIT * N_EXPERTS // EXPERT_BLOCK
    assert steps == PEER_HEADS * (c // TC_ROUTE)
    sel = pl.BlockSpec((c, PEER_SEL), lambda i, k: (i, 0))
    once = pl.Buffered(1)
    return pl.pallas_call(
        _route_act_kernel,
        grid=(t // c, steps),
        in_specs=[
            pl.BlockSpec((c, D_MODEL), lambda i, k: (i, 0)),
            pl.BlockSpec((D_MODEL, EXPERT_BLOCK), lambda i, k: (0, k // ROUTE_SPLIT)),
            pl.BlockSpec((2 * PEER_HEADS * D_HALF, D_MODEL), lambda i, k: (0, 0), pipeline_mode=once),
            pl.BlockSpec((2 * PEER_HEADS, N_KEYS, D_HALF), lambda i, k: (0, 0, 0), pipeline_mode=once),
        ],
        out_specs=[sel, sel, sel],
        out_shape=[jax.ShapeDtypeStruct((t, PEER_SEL), I32), jax.ShapeDtypeStruct((t, PEER_SEL), F32),
                   jax.ShapeDtypeStruct((t, PEER_SEL), F32)],
        scratch_shapes=[pltpu.VMEM((c, N_KEYS, N_KEYS), F32), pltpu.VMEM((c, D_MODEL), BF16),
                        pltpu.VMEM((PEER_SEL, c), I32), pltpu.VMEM((PEER_SEL, c), F32)],
        compiler_params=_tc_params("parallel", "arbitrary"),
        name="route_act",
    )(xn2, u_t, wq_t, keys)


def _gelu(a):
    return 0.5 * a * (1.0 + lax.erf(a * math.sqrt(0.5)))


def _expert_mix_kernel(normalize, idx_ref, a_ref, gate_ref, v_ref, h_ref, g_ref, o_ref, grid_ref, w_ref, acc_ref):
    k = pl.program_id(1)
    c = idx_ref.shape[0]

    @pl.when(k == 0)
    def _():
        acc_ref[...] = h_ref[...]
        w_ref[...] = gate_ref[...] * _gelu(a_ref[...])

        _, key_bf = _key_rows()
        one, zero = jnp.ones((), BF16), jnp.zeros((), BF16)

        def group(gi, carry):
            t0 = pl.multiple_of(gi * TOKENS_PER_TRIP, TOKENS_PER_TRIP)
            _, _, first_bf, second_bf = _split_keys(idx_ref[pl.ds(t0, TOKENS_PER_TRIP), :])
            w_bf = w_ref[pl.ds(t0, TOKENS_PER_TRIP), :].astype(BF16)
            for g0 in range(0, TOKENS_PER_TRIP, SUBLANES):
                mats = []
                for g in range(g0, g0 + SUBLANES):
                    weighted = jnp.where(key_bf == first_bf[g:g + 1], w_bf[g:g + 1], zero)
                    pick_second = jnp.where(key_bf == second_bf[g:g + 1], one, zero)
                    mats.append(lax.dot_general(weighted, pick_second, (((1,), (1,)), ((), ())),
                                                preferred_element_type=F32))
                grid_ref[:, pl.ds(t0 + g0, SUBLANES), :] = jnp.swapaxes(jnp.stack(mats, axis=0), 0, 1)
            return carry

        lax.fori_loop(0, c // TOKENS_PER_TRIP, group, 0)

    dense = jnp.concatenate([grid_ref[k * KEYS_PER_BLOCK + kk] for kk in range(KEYS_PER_BLOCK)], axis=1)
    acc_ref[...] += jnp.dot(dense.astype(BF16), v_ref[...], preferred_element_type=F32)

    @pl.when(k == pl.num_programs(1) - 1)
    def _():
        hh = acc_ref[...]
        o_ref[...] = hh * _rms_scale(hh) * g_ref[...] if normalize else hh


def _expert_mix(idx, a, gate, v_tab, h2, g, normalize):
    t = idx.shape[0]
    c = TC_EXPERT
    sel = pl.BlockSpec((c, PEER_SEL), lambda i, k: (i, 0))
    tok = pl.BlockSpec((c, D_MODEL), lambda i, k: (i, 0))
    return pl.pallas_call(
        functools.partial(_expert_mix_kernel, normalize),
        grid=(t // c, N_EXPERTS // EXPERT_BLOCK),
        in_specs=[sel, sel, sel, pl.BlockSpec((EXPERT_BLOCK, D_MODEL), lambda i, k: (k, 0)), tok,
                  pl.BlockSpec((1, D_MODEL), lambda i, k: (0, 0))],
        out_specs=tok,
        out_shape=jax.ShapeDtypeStruct((t, D_MODEL), F32),
        scratch_shapes=[pltpu.VMEM((N_KEYS, c, N_KEYS), F32), pltpu.VMEM((c, PEER_SEL), F32),
                        pltpu.VMEM((c, D_MODEL), F32)],
        compiler_params=_tc_params("parallel", "arbitrary"),
        name="expert_mix",
    )(idx, a, gate, v_tab, h2, g)


def _rearranged_in_proj(w_in, b_in):
    def cols(a):
        q = a[..., :Q_W]
        k = a[..., Q_W:Q_W + N_KV_HEADS * HEAD_DIM]
        v = a[..., Q_W + N_KV_HEADS * HEAD_DIM:Q_W + 2 * N_KV_HEADS * HEAD_DIM]
        rest = a[..., Q_W + 2 * N_KV_HEADS * HEAD_DIM:]
        dup = lambda m: jnp.concatenate(
            [m[..., hd * HEAD_DIM:(hd + 1) * HEAD_DIM] for hd in range(N_KV_HEADS) for _ in range(2)], axis=-1)
        return jnp.concatenate([q, dup(k), dup(v), rest], axis=-1)
    return cols(w_in).astype(BF16), cols(b_in)[None, :]


def kernel(x, ln_mix_g, w_in, b_in, attn_sinks, w_attn_up, w_pool_grp, pool_scale, w_pool_up, w_o,
           ln_ffn_g, w_query, sub_keys, u_experts, v_experts, ln_final_g):
    b, s, d = x.shape
    t = b * s
    depth = w_in.shape[0]
    h = x
    for l in range(depth):
        w_r, b_r = _rearranged_in_proj(w_in[l], b_in[l])
        q, kv, pz, gate = _inproj(h.reshape(t, d), ln_mix_g[l][None, :], w_r, b_r)
        o = _attention(attn_sinks[l], q.reshape(b, s, Q_W), kv.reshape(b, s, KV_COLS))
        h, xn2 = _merge(o, gate.reshape(b, s, 2 * d), h, pz.reshape(b, s, POOL_W),
                        w_attn_up[l].astype(BF16), w_pool_grp[l].astype(BF16), pool_scale[l][None, :],
                        w_pool_up[l].astype(BF16), w_o[l].astype(BF16), ln_ffn_g[l][None, :])
        xn2 = xn2.reshape(t, d)
        wq_t = w_query[l].T.astype(BF16)
        keys = sub_keys[l].reshape(2 * PEER_HEADS, SUBLANES, N_KEYS // SUBLANES, D_HALF)
        keys = keys[:, jnp.array(SUBLANE_BITREV)].transpose(0, 2, 1, 3)
        keys = keys.reshape(2 * PEER_HEADS, N_KEYS, D_HALF).astype(BF16)
        u_t = u_experts[l].astype(BF16).reshape(N_KEYS, N_KEYS, d).transpose(2, 1, 0).reshape(d, N_EXPERTS)
        idx, g, a = _route_act(xn2, u_t, wq_t, keys)
        last = l + 1 == depth
        h = _expert_mix(idx, a, g, v_experts[l].astype(BF16), h.reshape(t, d), ln_final_g[None, :], last)
        h = h.reshape(b, s, d)
    return h
```

```python
import functools
import math

import jax
import jax.numpy as jnp
from jax import lax
from jax.experimental import pallas as pl
from jax.experimental.pallas import tpu as pltpu
from jax.experimental.pallas import tpu_sc as plsc

F32 = jnp.float32
BF16 = jnp.bfloat16
I32 = jnp.int32

D_MODEL = 1024
N_HEADS = 16
N_KV_HEADS = 2
GROUP = N_HEADS // N_KV_HEADS
HEAD_DIM = 64
BLOCK = 128
ATTN_SCALE = 1.0 / math.sqrt(HEAD_DIM)
NEG_INF = -1e30
POOL_WINDOWS = (2, 4, 8, 16)
POOL_GROUP = 128
POOL_W = len(POOL_WINDOWS) * POOL_GROUP
POOL_HIST = max(POOL_WINDOWS)
Q_W = N_HEADS * HEAD_DIM
PEER_HEADS = 8
N_KEYS = 128
N_EXPERTS = N_KEYS * N_KEYS
D_HALF = 128
PEER_TOPK = 16
PEER_SEL = PEER_HEADS * PEER_TOPK
EPS = 1e-5

VMEM_LIMIT_BYTES = 56 * 1024 * 1024
SUBLANES = 8
LANES = 128
SUBLANE_BITREV = (0, 4, 2, 6, 1, 5, 3, 7)

C_Q = 0
C_K = C_Q + Q_W
C_V = C_K + 2 * BLOCK
C_P = C_V + 2 * BLOCK
C_G = C_P + POOL_W
C_END = C_G + 2 * D_MODEL
KV_COLS = C_P - C_K

TM_PROJ = 512
TS_MERGE = 512
ROUTE_SPLIT = 1
TC_ROUTE = 256
TC_EXPERT = 512
EXPERT_BLOCK = 1024
KEYS_PER_BLOCK = EXPERT_BLOCK // N_KEYS
TOKENS_PER_TRIP = 16


def _tc_params(*sem):
    return pltpu.CompilerParams(dimension_semantics=sem, vmem_limit_bytes=VMEM_LIMIT_BYTES)


def _rms_scale(x):
    return lax.rsqrt(jnp.mean(x * x, axis=-1, keepdims=True) + EPS)


def _inproj_kernel(x_ref, g_ref, w_ref, b_ref, q_ref, kv_ref, pz_ref, gate_ref):
    x = x_ref[...]
    xn = (x * _rms_scale(x) * g_ref[...]).astype(BF16)

    def proj(lo, hi):
        return jnp.dot(xn, w_ref[:, lo:hi], preferred_element_type=F32) + b_ref[:, lo:hi]

    q_ref[...] = proj(C_Q, C_K).astype(BF16)
    kv_ref[...] = proj(C_K, C_P).astype(BF16)
    pz_ref[...] = proj(C_P, C_G)
    gate_ref[...] = proj(C_G, C_END).astype(BF16)


def _inproj(x2, g, w, b):
    t = x2.shape[0]
    row = lambda i: (i, 0)
    fixed = lambda i: (0, 0)
    return pl.pallas_call(
        _inproj_kernel,
        grid=(t // TM_PROJ,),
        in_specs=[
            pl.BlockSpec((TM_PROJ, D_MODEL), row),
            pl.BlockSpec((1, D_MODEL), fixed),
            pl.BlockSpec((D_MODEL, C_END), fixed),
            pl.BlockSpec((1, C_END), fixed),
        ],
        out_specs=[
            pl.BlockSpec((TM_PROJ, Q_W), row),
            pl.BlockSpec((TM_PROJ, KV_COLS), row),
            pl.BlockSpec((TM_PROJ, POOL_W), row),
            pl.BlockSpec((TM_PROJ, 2 * D_MODEL), row),
        ],
        out_shape=[
            jax.ShapeDtypeStruct((t, Q_W), BF16),
            jax.ShapeDtypeStruct((t, KV_COLS), BF16),
            jax.ShapeDtypeStruct((t, POOL_W), F32),
            jax.ShapeDtypeStruct((t, 2 * D_MODEL), BF16),
        ],
        compiler_params=_tc_params("parallel"),
        name="inproj",
    )(x2, g, w, b)


def _attn_kernel(sink_ref, q_ref, kvc_ref, kvp_ref, o_ref):
    j = pl.program_id(1)
    kvc = kvc_ref[0]
    kvp = kvp_ref[0]
    row = lax.broadcasted_iota(I32, (BLOCK, BLOCK), 0)
    col = lax.broadcasted_iota(I32, (BLOCK, BLOCK), 1)
    mask = jnp.concatenate([jnp.logical_and(col > row, j > 0), col <= row], axis=1)
    low = lax.broadcasted_iota(I32, (1, BLOCK), 1) < HEAD_DIM
    zero = jnp.zeros((), BF16)
    for hkv in range(N_KV_HEADS):
        kc = slice(hkv * BLOCK, (hkv + 1) * BLOCK)
        vc = slice(2 * BLOCK + hkv * BLOCK, 2 * BLOCK + (hkv + 1) * BLOCK)
        k2 = jnp.concatenate([kvp[:, kc], kvc[:, kc]], axis=0)
        v2 = jnp.concatenate([kvp[:, vc], kvc[:, vc]], axis=0)
        v_half = (jnp.where(low, v2, zero), jnp.where(low, zero, v2))
        for pair in range(GROUP // 2):
            c0 = (hkv * (GROUP // 2) + pair) * BLOCK
            q2 = q_ref[0, :, c0:c0 + BLOCK]
            o2 = jnp.zeros((BLOCK, BLOCK), F32)
            for half in range(2):
                head = hkv * GROUP + pair * 2 + half
                qm = jnp.where(low, q2, zero) if half == 0 else jnp.where(low, zero, q2)
                s = lax.dot_general(qm, k2, (((1,), (1,)), ((), ())), preferred_element_type=F32)
                s = jnp.where(mask, s * ATTN_SCALE, NEG_INF)
                sink = sink_ref[head]
                m = jnp.maximum(jnp.max(s, axis=-1, keepdims=True), sink)
                e = jnp.exp(s - m)
                denom = jnp.sum(e, axis=-1, keepdims=True) + jnp.exp(sink - m)
                p = (e * (1.0 / denom)).astype(BF16)
                o2 = o2 + jnp.dot(p, v_half[half], preferred_element_type=F32)
            o_ref[0, :, c0:c0 + BLOCK] = o2.astype(BF16)


def _attention(sinks, q3, kv3):
    b, s, _ = q3.shape
    return pl.pallas_call(
        _attn_kernel,
        grid=(b, s // BLOCK),
        in_specs=[
            pl.BlockSpec(memory_space=pltpu.SMEM),
            pl.BlockSpec((1, BLOCK, Q_W), lambda i, j: (i, j, 0)),
            pl.BlockSpec((1, BLOCK, KV_COLS), lambda i, j: (i, j, 0)),
            pl.BlockSpec((1, BLOCK, KV_COLS), lambda i, j: (i, jnp.maximum(j - 1, 0), 0)),
        ],
        out_specs=pl.BlockSpec((1, BLOCK, Q_W), lambda i, j: (i, j, 0)),
        out_shape=jax.ShapeDtypeStruct((b, s, Q_W), BF16),
        compiler_params=_tc_params("parallel", "arbitrary"),
        name="attention",
    )(sinks, q3, kv3, kv3)


def _sigmoid(x):
    return 1.0 / (1.0 + jnp.exp(-x))


def _merge_kernel(o_ref, gate_ref, x_ref, pz_ref, pzp_ref, wau_ref, wgrp_ref, scale_ref, wpu_ref,
                  wo_ref, g_ref, h_ref, xn_ref):
    j = pl.program_id(1)
    ts = pz_ref.shape[1]
    prev = jnp.where(j > 0, pzp_ref[0], 0.0)
    ext = jnp.concatenate([prev, pz_ref[0]], axis=0)
    t1 = (j * ts + 1 + lax.broadcasted_iota(I32, (ts, 1), 0)).astype(F32)
    ys = []
    for g, w in enumerate(POOL_WINDOWS):
        e = ext[:, g * POOL_GROUP:(g + 1) * POOL_GROUP]
        tsum = e
        span = 1
        while span < w:
            tsum = tsum + pltpu.roll(tsum, span, 0)
            span *= 2
        pooled = tsum[POOL_HIST:] / jnp.minimum(t1, float(w)) - e[POOL_HIST:]
        y = jnp.dot(pooled.astype(BF16), wgrp_ref[g], preferred_element_type=F32)
        ys.append((y * scale_ref[:, g * POOL_GROUP:(g + 1) * POOL_GROUP]).astype(BF16))
    y_p = jnp.dot(jnp.concatenate(ys, axis=1), wpu_ref[...], preferred_element_type=F32)
    y_a = jnp.dot(o_ref[0], wau_ref[...], preferred_element_type=F32)
    gate = gate_ref[0]
    merged = (_sigmoid(gate[:, :D_MODEL].astype(F32)) * y_a
              + _sigmoid(gate[:, D_MODEL:].astype(F32)) * y_p)
    h = x_ref[0] + jnp.dot(merged.astype(BF16), wo_ref[...], preferred_element_type=F32)
    h_ref[0] = h
    xn_ref[0] = h * _rms_scale(h) * g_ref[...]


def _merge(o3, gate3, x, pz3, wau, wgrp, scale, wpu, wo, g):
    b, s, _ = x.shape
    ts = TS_MERGE
    hist_blocks = ts // POOL_HIST
    tile = lambda i, j: (i, j, 0)
    fixed2 = lambda i, j: (0, 0)
    return pl.pallas_call(
        _merge_kernel,
        grid=(b, s // ts),
        in_specs=[
            pl.BlockSpec((1, ts, Q_W), tile),
            pl.BlockSpec((1, ts, 2 * D_MODEL), tile),
            pl.BlockSpec((1, ts, D_MODEL), tile),
            pl.BlockSpec((1, ts, POOL_W), tile),
            pl.BlockSpec((1, POOL_HIST, POOL_W), lambda i, j: (i, jnp.maximum(j * hist_blocks - 1, 0), 0)),
            pl.BlockSpec((Q_W, D_MODEL), fixed2),
            pl.BlockSpec((len(POOL_WINDOWS), POOL_GROUP, POOL_GROUP), lambda i, j: (0, 0, 0)),
            pl.BlockSpec((1, POOL_W), fixed2),
            pl.BlockSpec((POOL_W, D_MODEL), fixed2),
            pl.BlockSpec((D_MODEL, D_MODEL), fixed2),
            pl.BlockSpec((1, D_MODEL), fixed2),
        ],
        out_specs=[pl.BlockSpec((1, ts, D_MODEL), tile), pl.BlockSpec((1, ts, D_MODEL), tile)],
        out_shape=[jax.ShapeDtypeStruct((b, s, D_MODEL), F32), jax.ShapeDtypeStruct((b, s, D_MODEL), F32)],
        compiler_params=_tc_params("parallel", "arbitrary"),
        name="merge",
    )(o3, gate3, x, pz3, pz3, wau, wgrp, scale, wpu, wo, g)


def _first_max(groups, ordered):
    while len(groups) > 1:
        merged = []
        for i in range(0, len(groups) - 1, 2):
            (vl, pl_), (vr, pr) = groups[i], groups[i + 1]
            merged.append((jnp.maximum(vl, vr), jnp.where(vl >= vr, pl_, pr)))
        if len(groups) % 2:
            merged.append(groups[-1])
        groups = merged
    v, p = groups[0]
    rows = SUBLANES
    while rows > 1:
        rows //= 2
        vl, vr, pl_, pr = v[:rows], v[rows:2 * rows], p[:rows], p[rows:2 * rows]
        take = (vl >= vr) if ordered else jnp.logical_or(vl > vr, jnp.logical_and(vl == vr, pl_ < pr))
        v, p = jnp.where(take, vl, vr), jnp.where(take, pl_, pr)
    return v, p


def _top16_rows(groups, ordered):
    vals, poss = [], []
    for _ in range(PEER_TOPK):
        m, am = _first_max(groups, ordered)
        vals.append(m)
        poss.append(am)
        groups = [(jnp.where(p == am, -jnp.inf, v), p) for v, p in groups]
    return jnp.concatenate(vals, axis=0), jnp.concatenate(poss, axis=0)


def _select_row(table, sel, pos16):
    out = []
    for k in range(PEER_TOPK):
        out.append(jnp.sum(jnp.where(pos16 == sel[k:k + 1], table, 0), axis=0, keepdims=True))
    return jnp.concatenate(out, axis=0)


def _route_head(xb, wq_ref, keys_ref, h, idx_scr, gate_scr, tokens):
    c = xb.shape[0]
    sub = lax.broadcasted_iota(I32, (SUBLANES, LANES), 0)
    sub_rev = ((sub & 1) << 2) | (sub & 2) | ((sub & 4) >> 2)
    key_pos = [(sub_rev * (N_KEYS // SUBLANES) + g).astype(F32) for g in range(N_KEYS // SUBLANES)]
    pos16 = lax.broadcasted_iota(I32, (PEER_TOPK, LANES), 0)
    cand_groups = [(a, b0) for a in range(PEER_TOPK) for b0 in range(0, PEER_TOPK // (a + 1), SUBLANES)]

    scores = []
    for half in range(2):
        hp = h * 2 + half
        wq = wq_ref[pl.ds(pl.multiple_of(hp * D_HALF, D_HALF), D_HALF), :]
        q_t = lax.dot_general(wq, xb, (((1,), (1,)), ((), ())), preferred_element_type=F32)
        scores.append(jnp.dot(keys_ref[hp], q_t.astype(BF16), preferred_element_type=F32))

    idx_out, gate_out = [], []
    for l0 in range(0, c, LANES):
        sv, si = [], []
        for sc in scores:
            groups = [(sc[g * SUBLANES:(g + 1) * SUBLANES, l0:l0 + LANES], key_pos[g])
                      for g in range(N_KEYS // SUBLANES)]
            v, p = _top16_rows(groups, ordered=True)
            sv.append(v)
            si.append(p.astype(I32))
        groups = []
        for a, b0 in cand_groups:
            val = sv[0][a:a + 1] + sv[1][b0:b0 + SUBLANES]
            val = jnp.where(sub + b0 < PEER_TOPK // (a + 1), val, -jnp.inf)
            groups.append((val, (sub + (a * PEER_TOPK + b0)).astype(F32)))
        fv, fpos = _top16_rows(groups, ordered=False)
        fpos = fpos.astype(I32)
        i0 = _select_row(si[0], fpos >> 4, pos16)
        i1 = _select_row(si[1], fpos & (PEER_TOPK - 1), pos16)
        e = jnp.exp(fv - fv[0:1])
        idx_out.append(i0 * N_KEYS + i1)
        gate_out.append(e / jnp.sum(e, axis=0, keepdims=True))
    rows = pl.ds(pl.multiple_of(h * PEER_TOPK, PEER_TOPK), PEER_TOPK)
    idx_scr[rows, tokens] = jnp.concatenate(idx_out, axis=1)
    gate_scr[rows, tokens] = jnp.concatenate(gate_out, axis=1)


def _key_rows():
    key = lax.broadcasted_iota(I32, (N_KEYS, PEER_SEL), 0)
    return key, key.astype(F32).astype(BF16)


def _split_keys(idx_rows):
    first, second = idx_rows >> 7, idx_rows & (N_KEYS - 1)
    as_bf16 = lambda v: v.astype(F32).astype(BF16)
    return first, second, as_bf16(first), as_bf16(second)


def _route_act_kernel(xn_ref, ut_ref, wq_ref, keys_ref, idx_ref, gate_ref, a_ref, grid_ref, xb_ref,
                      idx_scr, gate_scr):
    k = pl.program_id(1)
    c = xn_ref.shape[0]

    @pl.when(k == 0)
    def _():
        xb_ref[...] = xn_ref[...].astype(BF16)

    chunk_tokens = pl.ds(pl.multiple_of((k // PEER_HEADS) * TC_ROUTE, TC_ROUTE), TC_ROUTE)
    _route_head(xb_ref[chunk_tokens, :], wq_ref, keys_ref, k % PEER_HEADS, idx_scr, gate_scr, chunk_tokens)

    part = c // ROUTE_SPLIT
    part_tokens = pl.ds(pl.multiple_of((k % ROUTE_SPLIT) * part, part), part)
    dense = jnp.dot(xb_ref[part_tokens, :], ut_ref[...], preferred_element_type=F32)
    by_key = jnp.stack([dense[:, kk * N_KEYS:(kk + 1) * N_KEYS] for kk in range(KEYS_PER_BLOCK)], axis=0)
    second_rows = pl.ds(pl.multiple_of((k // ROUTE_SPLIT) * KEYS_PER_BLOCK, KEYS_PER_BLOCK), KEYS_PER_BLOCK)
    grid_ref[part_tokens, second_rows, :] = jnp.swapaxes(by_key, 0, 1)

    @pl.when(k == pl.num_programs(1) - 1)
    def _():
        idx_ref[...] = idx_scr[...].T
        gate_ref[...] = gate_scr[...].T
        key, key_bf = _key_rows()
        one, zero = jnp.ones((), BF16), jnp.zeros((), BF16)

        def group(gi, carry):
            t0 = pl.multiple_of(gi * TOKENS_PER_TRIP, TOKENS_PER_TRIP)
            _, second, first_bf, _ = _split_keys(idx_ref[pl.ds(t0, TOKENS_PER_TRIP), :])
            rows = []
            for g in range(TOKENS_PER_TRIP):
                pick_first = jnp.where(key_bf == first_bf[g:g + 1], one, zero)
                picked = jnp.dot(grid_ref[t0 + g].astype(BF16), pick_first,
                                 preferred_element_type=F32)
                rows.append(jnp.sum(jnp.where(key == second[g:g + 1], picked, 0.0), axis=0, keepdims=True))
            a_ref[pl.ds(t0, TOKENS_PER_TRIP), :] = jnp.concatenate(rows, axis=0)
            return carry

        lax.fori_loop(0, c // TOKENS_PER_TRIP, group, 0)


def _route_act(xn2, u_t, wq_t, keys):
    t = xn2.shape[0]
    c = TC_EXPERT
    steps = ROUTE_SPLIT * N_EXPERTS // EXPERT_BLOCK
    assert steps == PEER_HEADS * (c // TC_ROUTE)
    sel = pl.BlockSpec((c, PEER_SEL), lambda i, k: (i, 0))
    once = pl.Buffered(1)
    return pl.pallas_call(
        _route_act_kernel,
        grid=(t // c, steps),
        in_specs=[
            pl.BlockSpec((c, D_MODEL), lambda i, k: (i, 0)),
            pl.BlockSpec((D_MODEL, EXPERT_BLOCK), lambda i, k: (0, k // ROUTE_SPLIT)),
            pl.BlockSpec((2 * PEER_HEADS * D_HALF, D_MODEL), lambda i, k: (0, 0), pipeline_mode=once),
            pl.BlockSpec((2 * PEER_HEADS, N_KEYS, D_HALF), lambda i, k: (0, 0, 0), pipeline_mode=once),
        ],
        out_specs=[sel, sel, sel],
        out_shape=[jax.ShapeDtypeStruct((t, PEER_SEL), I32), jax.ShapeDtypeStruct((t, PEER_SEL), F32),
                   jax.ShapeDtypeStruct((t, PEER_SEL), F32)],
        scratch_shapes=[pltpu.VMEM((c, N_KEYS, N_KEYS), F32), pltpu.VMEM((c, D_MODEL), BF16),
                        pltpu.VMEM((PEER_SEL, c), I32), pltpu.VMEM((PEER_SEL, c), F32)],
        compiler_params=_tc_params("parallel", "arbitrary"),
        name="route_act",
    )(xn2, u_t, wq_t, keys)


def _gelu(a):
    return 0.5 * a * (1.0 + lax.erf(a * math.sqrt(0.5)))


def _expert_mix_kernel(normalize, idx_ref, a_ref, gate_ref, v_ref, h_ref, g_ref, o_ref, grid_ref, w_ref, acc_ref):
    k = pl.program_id(1)
    c = idx_ref.shape[0]

    @pl.when(k == 0)
    def _():
        acc_ref[...] = h_ref[...]
        w_ref[...] = gate_ref[...] * _gelu(a_ref[...])

        _, key_bf = _key_rows()
        one, zero = jnp.ones((), BF16), jnp.zeros((), BF16)

        def group(gi, carry):
            t0 = pl.multiple_of(gi * TOKENS_PER_TRIP, TOKENS_PER_TRIP)
            _, _, first_bf, second_bf = _split_keys(idx_ref[pl.ds(t0, TOKENS_PER_TRIP), :])
            w_bf = w_ref[pl.ds(t0, TOKENS_PER_TRIP), :].astype(BF16)
            for g0 in range(0, TOKENS_PER_TRIP, SUBLANES):
                mats = []
                for g in range(g0, g0 + SUBLANES):
                    weighted = jnp.where(key_bf == first_bf[g:g + 1], w_bf[g:g + 1], zero)
                    pick_second = jnp.where(key_bf == second_bf[g:g + 1], one, zero)
                    mats.append(lax.dot_general(weighted, pick_second, (((1,), (1,)), ((), ())),
                                                preferred_element_type=F32))
                grid_ref[:, pl.ds(t0 + g0, SUBLANES), :] = jnp.swapaxes(jnp.stack(mats, axis=0), 0, 1)
            return carry

        lax.fori_loop(0, c // TOKENS_PER_TRIP, group, 0)

    dense = jnp.concatenate([grid_ref[k * KEYS_PER_BLOCK + kk] for kk in range(KEYS_PER_BLOCK)], axis=1)
    acc_ref[...] += jnp.dot(dense.astype(BF16), v_ref[...], preferred_element_type=F32)

    @pl.when(k == pl.num_programs(1) - 1)
    def _():
        hh = acc_ref[...]
        o_ref[...] = hh * _rms_scale(hh) * g_ref[...] if normalize else hh


def _expert_mix(idx, a, gate, v_tab, h2, g, normalize):
    t = idx.shape[0]
    c = TC_EXPERT
    sel = pl.BlockSpec((c, PEER_SEL), lambda i, k: (i, 0))
    tok = pl.BlockSpec((c, D_MODEL), lambda i, k: (i, 0))
    return pl.pallas_call(
        functools.partial(_expert_mix_kernel, normalize),
        grid=(t // c, N_EXPERTS // EXPERT_BLOCK),
        in_specs=[sel, sel, sel, pl.BlockSpec((EXPERT_BLOCK, D_MODEL), lambda i, k: (k, 0)), tok,
                  pl.BlockSpec((1, D_MODEL), lambda i, k: (0, 0))],
        out_specs=tok,
        out_shape=jax.ShapeDtypeStruct((t, D_MODEL), F32),
        scratch_shapes=[pltpu.VMEM((N_KEYS, c, N_KEYS), F32), pltpu.VMEM((c, PEER_SEL), F32),
                        pltpu.VMEM((c, D_MODEL), F32)],
        compiler_params=_tc_params("parallel", "arbitrary"),
        name="expert_mix",
    )(idx, a, gate, v_tab, h2, g)


SC_CORES = 2
SC_SUBCORES = 16
SC_WORKERS = SC_CORES * SC_SUBCORES
SC_LANES = 16
SC_GATHER_ROWS = 16
SC_GATHERS_PER_TOKEN = PEER_SEL // SC_GATHER_ROWS
SC_TOKENS = 8
SC_CHUNKS = D_MODEL // SC_LANES
SC_SHARE = 4
TM_ELEM = 1024

_SC_PARAMS = pltpu.CompilerParams(needs_layout_passes=False)


def _route_kernel(xn_ref, wq_ref, keys_ref, idx_ref, gate_ref, idx_scr, gate_scr):
    xb = xn_ref[...].astype(BF16)
    everything = pl.ds(0, xb.shape[0])

    def head_body(h, carry):
        _route_head(xb, wq_ref, keys_ref, h, idx_scr, gate_scr, everything)
        return carry

    lax.fori_loop(0, PEER_HEADS, head_body, 0)
    idx_ref[...] = idx_scr[...].T
    gate_ref[...] = gate_scr[...].T


def _route(xn2, wq_t, keys):
    t = xn2.shape[0]
    c = TC_ROUTE
    sel = pl.BlockSpec((c, PEER_SEL), lambda i: (i, 0))
    return pl.pallas_call(
        _route_kernel,
        grid=(t // c,),
        in_specs=[
            pl.BlockSpec((c, D_MODEL), lambda i: (i, 0)),
            pl.BlockSpec((2 * PEER_HEADS * D_HALF, D_MODEL), lambda i: (0, 0)),
            pl.BlockSpec((2 * PEER_HEADS, N_KEYS, D_HALF), lambda i: (0, 0, 0)),
        ],
        out_specs=[sel, sel],
        out_shape=[jax.ShapeDtypeStruct((t, PEER_SEL), I32), jax.ShapeDtypeStruct((t, PEER_SEL), F32)],
        scratch_shapes=[pltpu.VMEM((PEER_SEL, c), I32), pltpu.VMEM((PEER_SEL, c), F32)],
        compiler_params=_tc_params("parallel"),
        name="route",
    )(xn2, wq_t, keys)


def _expert_weight_kernel(a_ref, g_ref, w_ref):
    w_ref[...] = g_ref[...] * _gelu(a_ref[...])


def _expert_weight(a, g):
    t = a.shape[0]
    spec = pl.BlockSpec((TM_ELEM, PEER_SEL), lambda i: (i, 0))
    return pl.pallas_call(
        _expert_weight_kernel,
        grid=(t // TM_ELEM,),
        in_specs=[spec, spec],
        out_specs=spec,
        out_shape=jax.ShapeDtypeStruct((t, PEER_SEL), F32),
        compiler_params=_tc_params("parallel"),
        name="expert_weight",
    )(a, g)


def _residual_kernel(normalize, h_ref, y_ref, g_ref, o_ref):
    h = h_ref[...] + y_ref[...]
    o_ref[...] = h * _rms_scale(h) * g_ref[...] if normalize else h


def _residual(h2, y2, g, normalize):
    t = h2.shape[0]
    spec = pl.BlockSpec((TM_ELEM, D_MODEL), lambda i: (i, 0))
    return pl.pallas_call(
        functools.partial(_residual_kernel, normalize),
        grid=(t // TM_ELEM,),
        in_specs=[spec, spec, pl.BlockSpec((1, D_MODEL), lambda i: (0, 0))],
        out_specs=spec,
        out_shape=jax.ShapeDtypeStruct((t, D_MODEL), F32),
        compiler_params=_tc_params("parallel"),
        name="residual",
    )(h2, y2, g)


def _sc_mesh():
    return plsc.VectorSubcoreMesh(core_axis_name="c", subcore_axis_name="s")


def _sc_worker_id():
    return lax.axis_index("s") * SC_CORES + lax.axis_index("c")


def _sc_gather_pipeline(tab_hbm, idx_v, rows_v, sems, compute):
    n_gathers = idx_v.shape[0]

    def gather(n, slot):
        return pltpu.make_async_copy(tab_hbm.at[idx_v[n, :]], rows_v.at[slot], sems.at[slot])

    gather(0, 0).start()

    def step(n2, carry):
        for slot in range(2):
            n = n2 * 2 + slot

            @pl.when(n + 1 < n_gathers)
            def _():
                gather(n + 1, 1 - slot).start()

            gather(n, slot).wait()
            compute(n, slot)
        return carry

    lax.fori_loop(0, n_gathers // 2, step, 0)


def _expert_dots_sc(u_tab, idx, xn2):
    t = idx.shape[0]
    tok_per_worker = t // SC_WORKERS
    idx2 = idx.reshape(t * SC_GATHERS_PER_TOKEN, SC_GATHER_ROWS)

    def body(u_hbm, idx_hbm, x_hbm, a_hbm, idx_v, x_v, rows_v, a_v, sems):
        wid = _sc_worker_id()
        lanes = lax.iota(I32, SC_LANES)

        def compute(n, slot):
            tl = n // SC_GATHERS_PER_TOKEN
            g = n % SC_GATHERS_PER_TOKEN

            def chunk(c, accs):
                off = pl.multiple_of(c * SC_LANES, SC_LANES)
                xv = x_v[tl, pl.ds(off, SC_LANES)]
                return tuple(accs[r] + rows_v[slot, r, pl.ds(off, SC_LANES)] * xv for r in range(SC_GATHER_ROWS))

            accs = lax.fori_loop(0, SC_CHUNKS, chunk,
                                 tuple(jnp.zeros((SC_LANES,), F32) for _ in range(SC_GATHER_ROWS)))
            tot = jnp.zeros((SC_LANES,), F32)
            for r in range(SC_GATHER_ROWS):
                tot = jnp.where(lanes == r, jnp.sum(accs[r]), tot)
            a_v[tl, pl.ds(pl.multiple_of(g * SC_GATHER_ROWS, SC_LANES), SC_LANES)] = tot

        def block(bi, carry):
            tok0 = wid * tok_per_worker + bi * SC_TOKENS
            pltpu.sync_copy(idx_hbm.at[pl.ds(tok0 * SC_GATHERS_PER_TOKEN, SC_TOKENS * SC_GATHERS_PER_TOKEN)], idx_v)
            pltpu.sync_copy(x_hbm.at[pl.ds(tok0, SC_TOKENS)], x_v)
            _sc_gather_pipeline(u_hbm, idx_v, rows_v, sems, compute)
            pltpu.sync_copy(a_v, a_hbm.at[pl.ds(tok0, SC_TOKENS)])
            return carry

        lax.fori_loop(0, tok_per_worker // SC_TOKENS, block, 0)

    return pl.kernel(
        body,
        out_type=jax.ShapeDtypeStruct((t, PEER_SEL), F32),
        mesh=_sc_mesh(),
        scratch_types=[
            pltpu.VMEM((SC_TOKENS * SC_GATHERS_PER_TOKEN, SC_GATHER_ROWS), I32),
            pltpu.VMEM((SC_TOKENS, D_MODEL), F32),
            pltpu.VMEM((2, SC_GATHER_ROWS, D_MODEL), F32),
            pltpu.VMEM((SC_TOKENS, PEER_SEL), F32),
            pltpu.SemaphoreType.DMA((2,)),
        ],
        compiler_params=_SC_PARAMS,
        name="expert_dots_sc",
    )(u_tab, idx2, xn2)


def _expert_mix_sc(v_tab, idx, w):
    t = idx.shape[0]
    tok_per_worker = t // SC_WORKERS
    idx2 = idx.reshape(t * SC_GATHERS_PER_TOKEN, SC_GATHER_ROWS)

    def body(v_hbm, idx_hbm, w_hbm, y_hbm, idx_v, w_v, rows_v, y_v, sems):
        wid = _sc_worker_id()

        def compute(n, slot):
            tl = n // SC_GATHERS_PER_TOKEN
            g = n % SC_GATHERS_PER_TOKEN
            wvec = w_v[tl, pl.ds(pl.multiple_of(g * SC_GATHER_ROWS, SC_LANES), SC_LANES)]

            for half in range(2):
                base = half * (D_MODEL // 2)

                def row(r, accs):
                    wv = jnp.take_along_axis(wvec, jnp.full((SC_LANES,), r, I32), axis=0)
                    return tuple(accs[c] + wv * rows_v[slot, r, pl.ds(base + c * SC_LANES, SC_LANES)]
                                 for c in range(SC_CHUNKS // 2))

                accs = lax.fori_loop(0, SC_GATHER_ROWS, row,
                                     tuple(jnp.zeros((SC_LANES,), F32) for _ in range(SC_CHUNKS // 2)))
                for c in range(SC_CHUNKS // 2):
                    sl = pl.ds(base + c * SC_LANES, SC_LANES)
                    y_v[tl, sl] = y_v[tl, sl] + accs[c]

        def block(bi, carry):
            tok0 = wid * tok_per_worker + bi * SC_TOKENS
            pltpu.sync_copy(idx_hbm.at[pl.ds(tok0 * SC_GATHERS_PER_TOKEN, SC_TOKENS * SC_GATHERS_PER_TOKEN)], idx_v)
            pltpu.sync_copy(w_hbm.at[pl.ds(tok0, SC_TOKENS)], w_v)
            zero = jnp.zeros((SC_LANES,), F32)
            for tl in range(SC_TOKENS):
                for c in range(SC_CHUNKS):
                    y_v[tl, pl.ds(c * SC_LANES, SC_LANES)] = zero
            _sc_gather_pipeline(v_hbm, idx_v, rows_v, sems, compute)
            pltpu.sync_copy(y_v, y_hbm.at[pl.ds(tok0, SC_TOKENS)])
            return carry

        lax.fori_loop(0, tok_per_worker // SC_TOKENS, block, 0)

    return pl.kernel(
        body,
        out_type=jax.ShapeDtypeStruct((t, D_MODEL), F32),
        mesh=_sc_mesh(),
        scratch_types=[
            pltpu.VMEM((SC_TOKENS * SC_GATHERS_PER_TOKEN, SC_GATHER_ROWS), I32),
            pltpu.VMEM((SC_TOKENS, PEER_SEL), F32),
            pltpu.VMEM((2, SC_GATHER_ROWS, D_MODEL), F32),
            pltpu.VMEM((SC_TOKENS, D_MODEL), F32),
            pltpu.SemaphoreType.DMA((2,)),
        ],
        compiler_params=_SC_PARAMS,
        name="expert_mix_sc",
    )(v_tab, idx2, w)


def _rearranged_in_proj(w_in, b_in):
    def cols(a):
        q = a[..., :Q_W]
        k = a[..., Q_W:Q_W + N_KV_HEADS * HEAD_DIM]
        v = a[..., Q_W + N_KV_HEADS * HEAD_DIM:Q_W + 2 * N_KV_HEADS * HEAD_DIM]
        rest = a[..., Q_W + 2 * N_KV_HEADS * HEAD_DIM:]
        dup = lambda m: jnp.concatenate(
            [m[..., hd * HEAD_DIM:(hd + 1) * HEAD_DIM] for hd in range(N_KV_HEADS) for _ in range(2)], axis=-1)
        return jnp.concatenate([q, dup(k), dup(v), rest], axis=-1)
    return cols(w_in).astype(BF16), cols(b_in)[None, :]


def kernel(x, ln_mix_g, w_in, b_in, attn_sinks, w_attn_up, w_pool_grp, pool_scale, w_pool_up, w_o,
           ln_ffn_g, w_query, sub_keys, u_experts, v_experts, ln_final_g):
    b, s, d = x.shape
    t = b * s
    depth = w_in.shape[0]
    h = x
    for l in range(depth):
        w_r, b_r = _rearranged_in_proj(w_in[l], b_in[l])
        q, kv, pz, gate = _inproj(h.reshape(t, d), ln_mix_g[l][None, :], w_r, b_r)
        o = _attention(attn_sinks[l], q.reshape(b, s, Q_W), kv.reshape(b, s, KV_COLS))
        h, xn2 = _merge(o, gate.reshape(b, s, 2 * d), h, pz.reshape(b, s, POOL_W),
                        w_attn_up[l].astype(BF16), w_pool_grp[l].astype(BF16), pool_scale[l][None, :],
                        w_pool_up[l].astype(BF16), w_o[l].astype(BF16), ln_ffn_g[l][None, :])
        xn2 = xn2.reshape(t, d)
        wq_t = w_query[l].T.astype(BF16)
        keys = sub_keys[l].reshape(2 * PEER_HEADS, SUBLANES, N_KEYS // SUBLANES, D_HALF)
        keys = keys[:, jnp.array(SUBLANE_BITREV)].transpose(0, 2, 1, 3)
        keys = keys.reshape(2 * PEER_HEADS, N_KEYS, D_HALF).astype(BF16)
        u_t = u_experts[l].astype(BF16).reshape(N_KEYS, N_KEYS, d).transpose(2, 1, 0).reshape(d, N_EXPERTS)
        last = l + 1 == depth
        g_out = ln_final_g[None, :]
        h2 = h.reshape(t, d)
        t_tc = t - (t // TC_EXPERT) // SC_SHARE * TC_EXPERT
        parts = []
        if t_tc < t:
            xn_sc = xn2[t_tc:]
            idx_sc, g_sc = _route(xn_sc, wq_t, keys)
            a_sc = _expert_dots_sc(u_experts[l], idx_sc, xn_sc)
        idx, g, a = _route_act(xn2[:t_tc], u_t, wq_t, keys)
        if t_tc < t:
            y_sc = _expert_mix_sc(v_experts[l], idx_sc, _expert_weight(a_sc, g_sc))
        parts.append(_expert_mix(idx, a, g, v_experts[l].astype(BF16), h2[:t_tc], g_out, last))
        if t_tc < t:
            parts.append(_residual(h2[t_tc:], y_sc, g_out, last))
        h = jnp.concatenate(parts, axis=0).reshape(b, s, d)
    return h
```

```python
import functools
import math

import jax
import jax.numpy as jnp
from jax import lax
from jax.experimental import pallas as pl
from jax.experimental.pallas import tpu as pltpu
from jax.experimental.pallas import tpu_sc as plsc

F32 = jnp.float32
BF16 = jnp.bfloat16
I32 = jnp.int32

D_MODEL = 1024
N_HEADS = 16
N_KV_HEADS = 2
GROUP = N_HEADS // N_KV_HEADS
HEAD_DIM = 64
BLOCK = 128
ATTN_SCALE = 1.0 / math.sqrt(HEAD_DIM)
NEG_INF = -1e30
POOL_WINDOWS = (2, 4, 8, 16)
POOL_GROUP = 128
POOL_W = len(POOL_WINDOWS) * POOL_GROUP
POOL_HIST = max(POOL_WINDOWS)
Q_W = N_HEADS * HEAD_DIM
PEER_HEADS = 8
N_KEYS = 128
N_EXPERTS = N_KEYS * N_KEYS
D_HALF = 128
PEER_TOPK = 16
PEER_SEL = PEER_HEADS * PEER_TOPK
EPS = 1e-5

VMEM_LIMIT_BYTES = 56 * 1024 * 1024
SUBLANES = 8
LANES = 128
SUBLANE_BITREV = (0, 4, 2, 6, 1, 5, 3, 7)

C_Q = 0
C_K = C_Q + Q_W
C_V = C_K + 2 * BLOCK
C_P = C_V + 2 * BLOCK
C_G = C_P + POOL_W
C_END = C_G + 2 * D_MODEL
KV_COLS = C_P - C_K

TM_PROJ = 512
TS_MERGE = 512
ROUTE_SPLIT = 1
TC_ROUTE = 256
TC_EXPERT = 512
EXPERT_BLOCK = 1024
KEYS_PER_BLOCK = EXPERT_BLOCK // N_KEYS
TOKENS_PER_TRIP = 16


def _tc_params(*sem):
    return pltpu.CompilerParams(dimension_semantics=sem, vmem_limit_bytes=VMEM_LIMIT_BYTES)


def _rms_scale(x):
    return lax.rsqrt(jnp.mean(x * x, axis=-1, keepdims=True) + EPS)


def _inproj_kernel(x_ref, g_ref, w_ref, b_ref, q_ref, kv_ref, pz_ref, gate_ref):
    x = x_ref[...]
    xn = (x * _rms_scale(x) * g_ref[...]).astype(BF16)

    def proj(lo, hi):
        return jnp.dot(xn, w_ref[:, lo:hi], preferred_element_type=F32) + b_ref[:, lo:hi]

    q_ref[...] = proj(C_Q, C_K).astype(BF16)
    kv_ref[...] = proj(C_K, C_P).astype(BF16)
    pz_ref[...] = proj(C_P, C_G)
    gate_ref[...] = proj(C_G, C_END).astype(BF16)


def _inproj(x2, g, w, b):
    t = x2.shape[0]
    row = lambda i: (i, 0)
    fixed = lambda i: (0, 0)
    return pl.pallas_call(
        _inproj_kernel,
        grid=(t // TM_PROJ,),
        in_specs=[
            pl.BlockSpec((TM_PROJ, D_MODEL), row),
            pl.BlockSpec((1, D_MODEL), fixed),
            pl.BlockSpec((D_MODEL, C_END), fixed),
            pl.BlockSpec((1, C_END), fixed),
        ],
        out_specs=[
            pl.BlockSpec((TM_PROJ, Q_W), row),
            pl.BlockSpec((TM_PROJ, KV_COLS), row),
            pl.BlockSpec((TM_PROJ, POOL_W), row),
            pl.BlockSpec((TM_PROJ, 2 * D_MODEL), row),
        ],
        out_shape=[
            jax.ShapeDtypeStruct((t, Q_W), BF16),
            jax.ShapeDtypeStruct((t, KV_COLS), BF16),
            jax.ShapeDtypeStruct((t, POOL_W), F32),
            jax.ShapeDtypeStruct((t, 2 * D_MODEL), BF16),
        ],
        compiler_params=_tc_params("parallel"),
        name="inproj",
    )(x2, g, w, b)


def _attn_kernel(sink_ref, q_ref, kvc_ref, kvp_ref, o_ref):
    j = pl.program_id(1)
    kvc = kvc_ref[0]
    kvp = kvp_ref[0]
    row = lax.broadcasted_iota(I32, (BLOCK, BLOCK), 0)
    col = lax.broadcasted_iota(I32, (BLOCK, BLOCK), 1)
    mask = jnp.concatenate([jnp.logical_and(col > row, j > 0), col <= row], axis=1)
    low = lax.broadcasted_iota(I32, (1, BLOCK), 1) < HEAD_DIM
    zero = jnp.zeros((), BF16)
    for hkv in range(N_KV_HEADS):
        kc = slice(hkv * BLOCK, (hkv + 1) * BLOCK)
        vc = slice(2 * BLOCK + hkv * BLOCK, 2 * BLOCK + (hkv + 1) * BLOCK)
        k2 = jnp.concatenate([kvp[:, kc], kvc[:, kc]], axis=0)
        v2 = jnp.concatenate([kvp[:, vc], kvc[:, vc]], axis=0)
        v_half = (jnp.where(low, v2, zero), jnp.where(low, zero, v2))
        for pair in range(GROUP // 2):
            c0 = (hkv * (GROUP // 2) + pair) * BLOCK
            q2 = q_ref[0, :, c0:c0 + BLOCK]
            o2 = jnp.zeros((BLOCK, BLOCK), F32)
            for half in range(2):
                head = hkv * GROUP + pair * 2 + half
                qm = jnp.where(low, q2, zero) if half == 0 else jnp.where(low, zero, q2)
                s = lax.dot_general(qm, k2, (((1,), (1,)), ((), ())), preferred_element_type=F32)
                s = jnp.where(mask, s * ATTN_SCALE, NEG_INF)
                sink = sink_ref[head]
                m = jnp.maximum(jnp.max(s, axis=-1, keepdims=True), sink)
                e = jnp.exp(s - m)
                denom = jnp.sum(e, axis=-1, keepdims=True) + jnp.exp(sink - m)
                p = (e * (1.0 / denom)).astype(BF16)
                o2 = o2 + jnp.dot(p, v_half[half], preferred_element_type=F32)
            o_ref[0, :, c0:c0 + BLOCK] = o2.astype(BF16)


def _attention(sinks, q3, kv3):
    b, s, _ = q3.shape
    return pl.pallas_call(
        _attn_kernel,
        grid=(b, s // BLOCK),
        in_specs=[
            pl.BlockSpec(memory_space=pltpu.SMEM),
            pl.BlockSpec((1, BLOCK, Q_W), lambda i, j: (i, j, 0)),
            pl.BlockSpec((1, BLOCK, KV_COLS), lambda i, j: (i, j, 0)),
            pl.BlockSpec((1, BLOCK, KV_COLS), lambda i, j: (i, jnp.maximum(j - 1, 0), 0)),
        ],
        out_specs=pl.BlockSpec((1, BLOCK, Q_W), lambda i, j: (i, j, 0)),
        out_shape=jax.ShapeDtypeStruct((b, s, Q_W), BF16),
        compiler_params=_tc_params("parallel", "arbitrary"),
        name="attention",
    )(sinks, q3, kv3, kv3)


def _sigmoid(x):
    return 1.0 / (1.0 + jnp.exp(-x))


def _merge_kernel(o_ref, gate_ref, x_ref, pz_ref, pzp_ref, wau_ref, wgrp_ref, scale_ref, wpu_ref,
                  wo_ref, g_ref, h_ref, xn_ref):
    j = pl.program_id(1)
    ts = pz_ref.shape[1]
    prev = jnp.where(j > 0, pzp_ref[0], 0.0)
    ext = jnp.concatenate([prev, pz_ref[0]], axis=0)
    t1 = (j * ts + 1 + lax.broadcasted_iota(I32, (ts, 1), 0)).astype(F32)
    ys = []
    for g, w in enumerate(POOL_WINDOWS):
        e = ext[:, g * POOL_GROUP:(g + 1) * POOL_GROUP]
        tsum = e
        span = 1
        while span < w:
            tsum = tsum + pltpu.roll(tsum, span, 0)
            span *= 2
        pooled = tsum[POOL_HIST:] / jnp.minimum(t1, float(w)) - e[POOL_HIST:]
        y = jnp.dot(pooled.astype(BF16), wgrp_ref[g], preferred_element_type=F32)
        ys.append((y * scale_ref[:, g * POOL_GROUP:(g + 1) * POOL_GROUP]).astype(BF16))
    y_p = jnp.dot(jnp.concatenate(ys, axis=1), wpu_ref[...], preferred_element_type=F32)
    y_a = jnp.dot(o_ref[0], wau_ref[...], preferred_element_type=F32)
    gate = gate_ref[0]
    merged = (_sigmoid(gate[:, :D_MODEL].astype(F32)) * y_a
              + _sigmoid(gate[:, D_MODEL:].astype(F32)) * y_p)
    h = x_ref[0] + jnp.dot(merged.astype(BF16), wo_ref[...], preferred_element_type=F32)
    h_ref[0] = h
    xn_ref[0] = h * _rms_scale(h) * g_ref[...]


def _merge(o3, gate3, x, pz3, wau, wgrp, scale, wpu, wo, g):
    b, s, _ = x.shape
    ts = TS_MERGE
    hist_blocks = ts // POOL_HIST
    tile = lambda i, j: (i, j, 0)
    fixed2 = lambda i, j: (0, 0)
    return pl.pallas_call(
        _merge_kernel,
        grid=(b, s // ts),
        in_specs=[
            pl.BlockSpec((1, ts, Q_W), tile),
            pl.BlockSpec((1, ts, 2 * D_MODEL), tile),
            pl.BlockSpec((1, ts, D_MODEL), tile),
            pl.BlockSpec((1, ts, POOL_W), tile),
            pl.BlockSpec((1, POOL_HIST, POOL_W), lambda i, j: (i, jnp.maximum(j * hist_blocks - 1, 0), 0)),
            pl.BlockSpec((Q_W, D_MODEL), fixed2),
            pl.BlockSpec((len(POOL_WINDOWS), POOL_GROUP, POOL_GROUP), lambda i, j: (0, 0, 0)),
            pl.BlockSpec((1, POOL_W), fixed2),
            pl.BlockSpec((POOL_W, D_MODEL), fixed2),
            pl.BlockSpec((D_MODEL, D_MODEL), fixed2),
            pl.BlockSpec((1, D_MODEL), fixed2),
        ],
        out_specs=[pl.BlockSpec((1, ts, D_MODEL), tile), pl.BlockSpec((1, ts, D_MODEL), tile)],
        out_shape=[jax.ShapeDtypeStruct((b, s, D_MODEL), F32), jax.ShapeDtypeStruct((b, s, D_MODEL), F32)],
        compiler_params=_tc_params("parallel", "arbitrary"),
        name="merge",
    )(o3, gate3, x, pz3, pz3, wau, wgrp, scale, wpu, wo, g)


def _first_max(groups, ordered):
    while len(groups) > 1:
        merged = []
        for i in range(0, len(groups) - 1, 2):
            (vl, pl_), (vr, pr) = groups[i], groups[i + 1]
            merged.append((jnp.maximum(vl, vr), jnp.where(vl >= vr, pl_, pr)))
        if len(groups) % 2:
            merged.append(groups[-1])
        groups = merged
    v, p = groups[0]
    rows = SUBLANES
    while rows > 1:
        rows //= 2
        vl, vr, pl_, pr = v[:rows], v[rows:2 * rows], p[:rows], p[rows:2 * rows]
        take = (vl >= vr) if ordered else jnp.logical_or(vl > vr, jnp.logical_and(vl == vr, pl_ < pr))
        v, p = jnp.where(take, vl, vr), jnp.where(take, pl_, pr)
    return v, p


def _top16_rows(groups, ordered):
    vals, poss = [], []
    for _ in range(PEER_TOPK):
        m, am = _first_max(groups, ordered)
        vals.append(m)
        poss.append(am)
        groups = [(jnp.where(p == am, -jnp.inf, v), p) for v, p in groups]
    return jnp.concatenate(vals, axis=0), jnp.concatenate(poss, axis=0)


def _select_row(table, sel, pos16):
    out = []
    for k in range(PEER_TOPK):
        out.append(jnp.sum(jnp.where(pos16 == sel[k:k + 1], table, 0), axis=0, keepdims=True))
    return jnp.concatenate(out, axis=0)


def _route_head(xb, wq_ref, keys_ref, h, idx_scr, gate_scr, tokens):
    c = xb.shape[0]
    sub = lax.broadcasted_iota(I32, (SUBLANES, LANES), 0)
    sub_rev = ((sub & 1) << 2) | (sub & 2) | ((sub & 4) >> 2)
    key_pos = [(sub_rev * (N_KEYS // SUBLANES) + g).astype(F32) for g in range(N_KEYS // SUBLANES)]
    pos16 = lax.broadcasted_iota(I32, (PEER_TOPK, LANES), 0)
    cand_groups = [(a, b0) for a in range(PEER_TOPK) for b0 in range(0, PEER_TOPK // (a + 1), SUBLANES)]

    scores = []
    for half in range(2):
        hp = h * 2 + half
        wq = wq_ref[pl.ds(pl.multiple_of(hp * D_HALF, D_HALF), D_HALF), :]
        q_t = lax.dot_general(wq, xb, (((1,), (1,)), ((), ())), preferred_element_type=F32)
        scores.append(jnp.dot(keys_ref[hp], q_t.astype(BF16), preferred_element_type=F32))

    idx_out, gate_out = [], []
    for l0 in range(0, c, LANES):
        sv, si = [], []
        for sc in scores:
            groups = [(sc[g * SUBLANES:(g + 1) * SUBLANES, l0:l0 + LANES], key_pos[g])
                      for g in range(N_KEYS // SUBLANES)]
            v, p = _top16_rows(groups, ordered=True)
            sv.append(v)
            si.append(p.astype(I32))
        groups = []
        for a, b0 in cand_groups:
            val = sv[0][a:a + 1] + sv[1][b0:b0 + SUBLANES]
            val = jnp.where(sub + b0 < PEER_TOPK // (a + 1), val, -jnp.inf)
            groups.append((val, (sub + (a * PEER_TOPK + b0)).astype(F32)))
        fv, fpos = _top16_rows(groups, ordered=False)
        fpos = fpos.astype(I32)
        i0 = _select_row(si[0], fpos >> 4, pos16)
        i1 = _select_row(si[1], fpos & (PEER_TOPK - 1), pos16)
        e = jnp.exp(fv - fv[0:1])
        idx_out.append(i0 * N_KEYS + i1)
        gate_out.append(e / jnp.sum(e, axis=0, keepdims=True))
    rows = pl.ds(pl.multiple_of(h * PEER_TOPK, PEER_TOPK), PEER_TOPK)
    idx_scr[rows, tokens] = jnp.concatenate(idx_out, axis=1)
    gate_scr[rows, tokens] = jnp.concatenate(gate_out, axis=1)


def _key_rows():
    key = lax.broadcasted_iota(I32, (N_KEYS, PEER_SEL), 0)
    return key, key.astype(F32).astype(BF16)


def _split_keys(idx_rows):
    first, second = idx_rows >> 7, idx_rows & (N_KEYS - 1)
    as_bf16 = lambda v: v.astype(F32).astype(BF16)
    return first, second, as_bf16(first), as_bf16(second)


def _route_act_kernel(xn_ref, ut_ref, wq_ref, keys_ref, idx_ref, gate_ref, a_ref, grid_ref, xb_ref,
                      idx_scr, gate_scr):
    k = pl.program_id(1)
    c = xn_ref.shape[0]

    @pl.when(k == 0)
    def _():
        xb_ref[...] = xn_ref[...].astype(BF16)

    chunk_tokens = pl.ds(pl.multiple_of((k // PEER_HEADS) * TC_ROUTE, TC_ROUTE), TC_ROUTE)
    _route_head(xb_ref[chunk_tokens, :], wq_ref, keys_ref, k % PEER_HEADS, idx_scr, gate_scr, chunk_tokens)

    part = c // ROUTE_SPLIT
    part_tokens = pl.ds(pl.multiple_of((k % ROUTE_SPLIT) * part, part), part)
    dense = jnp.dot(xb_ref[part_tokens, :], ut_ref[...], preferred_element_type=F32)
    by_key = jnp.stack([dense[:, kk * N_KEYS:(kk + 1) * N_KEYS] for kk in range(KEYS_PER_BLOCK)], axis=0)
    second_rows = pl.ds(pl.multiple_of((k // ROUTE_SPLIT) * KEYS_PER_BLOCK, KEYS_PER_BLOCK), KEYS_PER_BLOCK)
    grid_ref[part_tokens, second_rows, :] = jnp.swapaxes(by_key, 0, 1)

    @pl.when(k == pl.num_programs(1) - 1)
    def _():
        idx_ref[...] = idx_scr[...].T
        gate_ref[...] = gate_scr[...].T
        key, key_bf = _key_rows()
        one, zero = jnp.ones((), BF16), jnp.zeros((), BF16)

        def group(gi, carry):
            t0 = pl.multiple_of(gi * TOKENS_PER_TRIP, TOKENS_PER_TRIP)
            _, second, first_bf, _ = _split_keys(idx_ref[pl.ds(t0, TOKENS_PER_TRIP), :])
            rows = []
            for g in range(TOKENS_PER_TRIP):
                pick_first = jnp.where(key_bf == first_bf[g:g + 1], one, zero)
                picked = jnp.dot(grid_ref[t0 + g].astype(BF16), pick_first,
                                 preferred_element_type=F32)
                rows.append(jnp.sum(jnp.where(key == second[g:g + 1], picked, 0.0), axis=0, keepdims=True))
            a_ref[pl.ds(t0, TOKENS_PER_TRIP), :] = jnp.concatenate(rows, axis=0)
            return carry

        lax.fori_loop(0, c // TOKENS_PER_TRIP, group, 0)


def _route_act(xn2, u_t, wq_t, keys, t):
    c = TC_EXPERT
    steps = ROUTE_SPLIT * N_EXPERTS // EXPERT_BLOCK
    assert steps == PEER_HEADS * (c // TC_ROUTE)
    sel = pl.BlockSpec((c, PEER_SEL), lambda i, k: (i, 0))
    once = pl.Buffered(1)
    return pl.pallas_call(
        _route_act_kernel,
        grid=(t // c, steps),
        in_specs=[
            pl.BlockSpec((c, D_MODEL), lambda i, k: (i, 0)),
            pl.BlockSpec((D_MODEL, EXPERT_BLOCK), lambda i, k: (0, k // ROUTE_SPLIT)),
            pl.BlockSpec((2 * PEER_HEADS * D_HALF, D_MODEL), lambda i, k: (0, 0), pipeline_mode=once),
            pl.BlockSpec((2 * PEER_HEADS, N_KEYS, D_HALF), lambda i, k: (0, 0, 0), pipeline_mode=once),
        ],
        out_specs=[sel, sel, sel],
        out_shape=[jax.ShapeDtypeStruct((t, PEER_SEL), I32), jax.ShapeDtypeStruct((t, PEER_SEL), F32),
                   jax.ShapeDtypeStruct((t, PEER_SEL), F32)],
        scratch_shapes=[pltpu.VMEM((c, N_KEYS, N_KEYS), F32), pltpu.VMEM((c, D_MODEL), BF16),
                        pltpu.VMEM((PEER_SEL, c), I32), pltpu.VMEM((PEER_SEL, c), F32)],
        compiler_params=_tc_params("parallel", "arbitrary"),
        name="route_act",
    )(xn2, u_t, wq_t, keys)


def _gelu(a):
    return 0.5 * a * (1.0 + lax.erf(a * math.sqrt(0.5)))


def _expert_mix_kernel(normalize, idx_ref, a_ref, gate_ref, v_ref, h_ref, g_ref, o_ref, grid_ref, w_ref, acc_ref):
    k = pl.program_id(1)
    c = idx_ref.shape[0]

    @pl.when(k == 0)
    def _():
        acc_ref[...] = h_ref[...]
        w_ref[...] = gate_ref[...] * _gelu(a_ref[...])

        _, key_bf = _key_rows()
        one, zero = jnp.ones((), BF16), jnp.zeros((), BF16)

        def group(gi, carry):
            t0 = pl.multiple_of(gi * TOKENS_PER_TRIP, TOKENS_PER_TRIP)
            _, _, first_bf, second_bf = _split_keys(idx_ref[pl.ds(t0, TOKENS_PER_TRIP), :])
            w_bf = w_ref[pl.ds(t0, TOKENS_PER_TRIP), :].astype(BF16)
            for g0 in range(0, TOKENS_PER_TRIP, SUBLANES):
                mats = []
                for g in range(g0, g0 + SUBLANES):
                    weighted = jnp.where(key_bf == first_bf[g:g + 1], w_bf[g:g + 1], zero)
                    pick_second = jnp.where(key_bf == second_bf[g:g + 1], one, zero)
                    mats.append(lax.dot_general(weighted, pick_second, (((1,), (1,)), ((), ())),
                                                preferred_element_type=F32))
                grid_ref[:, pl.ds(t0 + g0, SUBLANES), :] = jnp.swapaxes(jnp.stack(mats, axis=0), 0, 1)
            return carry

        lax.fori_loop(0, c // TOKENS_PER_TRIP, group, 0)

    dense = jnp.concatenate([grid_ref[k * KEYS_PER_BLOCK + kk] for kk in range(KEYS_PER_BLOCK)], axis=1)
    acc_ref[...] += jnp.dot(dense.astype(BF16), v_ref[...], preferred_element_type=F32)

    @pl.when(k == pl.num_programs(1) - 1)
    def _():
        hh = acc_ref[...]
        o_ref[...] = hh * _rms_scale(hh) * g_ref[...] if normalize else hh


def _expert_mix(idx, a, gate, v_tab, h2, g, normalize):
    t = idx.shape[0]
    c = TC_EXPERT
    sel = pl.BlockSpec((c, PEER_SEL), lambda i, k: (i, 0))
    tok = pl.BlockSpec((c, D_MODEL), lambda i, k: (i, 0))
    return pl.pallas_call(
        functools.partial(_expert_mix_kernel, normalize),
        grid=(t // c, N_EXPERTS // EXPERT_BLOCK),
        in_specs=[sel, sel, sel, pl.BlockSpec((EXPERT_BLOCK, D_MODEL), lambda i, k: (k, 0)), tok,
                  pl.BlockSpec((1, D_MODEL), lambda i, k: (0, 0))],
        out_specs=tok,
        out_shape=jax.ShapeDtypeStruct(h2.shape, F32),
        scratch_shapes=[pltpu.VMEM((N_KEYS, c, N_KEYS), F32), pltpu.VMEM((c, PEER_SEL), F32),
                        pltpu.VMEM((c, D_MODEL), F32)],
        compiler_params=_tc_params("parallel", "arbitrary"),
        name="expert_mix",
    )(idx, a, gate, v_tab, h2, g)


SC_CORES = 2
SC_SUBCORES = 16
SC_WORKERS = SC_CORES * SC_SUBCORES
SC_LANES = 16
SC_GATHER_ROWS = 16
SC_GATHERS_PER_TOKEN = PEER_SEL // SC_GATHER_ROWS
SC_TOKENS = 8
SC_CHUNKS = D_MODEL // SC_LANES
SC_SHARE = 4
TM_ELEM = 1024

_SC_PARAMS = pltpu.CompilerParams(needs_layout_passes=False)


def _route_kernel(xn_ref, wq_ref, keys_ref, idx_ref, gate_ref, idx_scr, gate_scr):
    xb = xn_ref[...].astype(BF16)
    everything = pl.ds(0, xb.shape[0])

    def head_body(h, carry):
        _route_head(xb, wq_ref, keys_ref, h, idx_scr, gate_scr, everything)
        return carry

    lax.fori_loop(0, PEER_HEADS, head_body, 0)
    idx_ref[...] = idx_scr[...].T
    gate_ref[...] = gate_scr[...].T


def _route(xn2, wq_t, keys, first):
    t = xn2.shape[0] - first
    c = TC_ROUTE
    sel = pl.BlockSpec((c, PEER_SEL), lambda i: (i, 0))
    return pl.pallas_call(
        _route_kernel,
        grid=(t // c,),
        in_specs=[
            pl.BlockSpec((c, D_MODEL), lambda i: (i + first // c, 0)),
            pl.BlockSpec((2 * PEER_HEADS * D_HALF, D_MODEL), lambda i: (0, 0)),
            pl.BlockSpec((2 * PEER_HEADS, N_KEYS, D_HALF), lambda i: (0, 0, 0)),
        ],
        out_specs=[sel, sel],
        out_shape=[jax.ShapeDtypeStruct((t, PEER_SEL), I32), jax.ShapeDtypeStruct((t, PEER_SEL), F32)],
        scratch_shapes=[pltpu.VMEM((PEER_SEL, c), I32), pltpu.VMEM((PEER_SEL, c), F32)],
        compiler_params=_tc_params("parallel"),
        name="route",
    )(xn2, wq_t, keys)


def _expert_weight_kernel(a_ref, g_ref, after_ref, w_ref):
    del after_ref
    w_ref[...] = g_ref[...] * _gelu(a_ref[...])


def _expert_weight(a, g, after):
    t = a.shape[0]
    spec = pl.BlockSpec((TM_ELEM, PEER_SEL), lambda i: (i, 0))
    return pl.pallas_call(
        _expert_weight_kernel,
        grid=(t // TM_ELEM,),
        in_specs=[spec, spec, pl.BlockSpec(memory_space=pl.ANY)],
        out_specs=spec,
        out_shape=jax.ShapeDtypeStruct((t, PEER_SEL), F32),
        compiler_params=_tc_params("parallel"),
        name="expert_weight",
    )(a, g, after)


def _residual_kernel(normalize, out_in_ref, h_ref, y_ref, g_ref, o_ref):
    del out_in_ref
    h = h_ref[...] + y_ref[...]
    o_ref[...] = h * _rms_scale(h) * g_ref[...] if normalize else h


def _residual(out, h2, y2, g, normalize):
    first = h2.shape[0] - y2.shape[0]
    tail = pl.BlockSpec((TM_ELEM, D_MODEL), lambda i: (i + first // TM_ELEM, 0))
    return pl.pallas_call(
        functools.partial(_residual_kernel, normalize),
        grid=(y2.shape[0] // TM_ELEM,),
        in_specs=[pl.BlockSpec(memory_space=pl.ANY), tail, pl.BlockSpec((TM_ELEM, D_MODEL), lambda i: (i, 0)),
                  pl.BlockSpec((1, D_MODEL), lambda i: (0, 0))],
        out_specs=tail,
        out_shape=jax.ShapeDtypeStruct(out.shape, F32),
        input_output_aliases={0: 0},
        compiler_params=_tc_params("parallel"),
        name="residual",
    )(out, h2, y2, g)


def _sc_mesh():
    return plsc.VectorSubcoreMesh(core_axis_name="c", subcore_axis_name="s")


def _sc_worker_id():
    return lax.axis_index("s") * SC_CORES + lax.axis_index("c")


def _sc_gather_pipeline(tab_hbm, idx_v, rows_v, sems, compute):
    n_gathers = idx_v.shape[0]

    def gather(n, slot):
        return pltpu.make_async_copy(tab_hbm.at[idx_v[n, :]], rows_v.at[slot], sems.at[slot])

    gather(0, 0).start()

    def step(n2, carry):
        for slot in range(2):
            n = n2 * 2 + slot

            @pl.when(n + 1 < n_gathers)
            def _():
                gather(n + 1, 1 - slot).start()

            gather(n, slot).wait()
            compute(n, slot)
        return carry

    lax.fori_loop(0, n_gathers // 2, step, 0)


def _expert_dots_sc(u_tab, idx, xn2):
    t = idx.shape[0]
    first = xn2.shape[0] - t
    tok_per_worker = t // SC_WORKERS
    idx2 = idx.reshape(t * SC_GATHERS_PER_TOKEN, SC_GATHER_ROWS)

    def body(u_hbm, idx_hbm, x_hbm, a_hbm, idx_v, x_v, rows_v, a_v, sems):
        wid = _sc_worker_id()
        lanes = lax.iota(I32, SC_LANES)

        def compute(n, slot):
            tl = n // SC_GATHERS_PER_TOKEN
            g = n % SC_GATHERS_PER_TOKEN

            def chunk(c, accs):
                off = pl.multiple_of(c * SC_LANES, SC_LANES)
                xv = x_v[tl, pl.ds(off, SC_LANES)]
                return tuple(accs[r] + rows_v[slot, r, pl.ds(off, SC_LANES)] * xv for r in range(SC_GATHER_ROWS))

            accs = lax.fori_loop(0, SC_CHUNKS, chunk,
                                 tuple(jnp.zeros((SC_LANES,), F32) for _ in range(SC_GATHER_ROWS)))
            tot = jnp.zeros((SC_LANES,), F32)
            for r in range(SC_GATHER_ROWS):
                tot = jnp.where(lanes == r, jnp.sum(accs[r]), tot)
            a_v[tl, pl.ds(pl.multiple_of(g * SC_GATHER_ROWS, SC_LANES), SC_LANES)] = tot

        def block(bi, carry):
            tok0 = wid * tok_per_worker + bi * SC_TOKENS
            pltpu.sync_copy(idx_hbm.at[pl.ds(tok0 * SC_GATHERS_PER_TOKEN, SC_TOKENS * SC_GATHERS_PER_TOKEN)], idx_v)
            pltpu.sync_copy(x_hbm.at[pl.ds(first + tok0, SC_TOKENS)], x_v)
            _sc_gather_pipeline(u_hbm, idx_v, rows_v, sems, compute)
            pltpu.sync_copy(a_v, a_hbm.at[pl.ds(tok0, SC_TOKENS)])
            return carry

        lax.fori_loop(0, tok_per_worker // SC_TOKENS, block, 0)

    return pl.kernel(
        body,
        out_type=jax.ShapeDtypeStruct((t, PEER_SEL), F32),
        mesh=_sc_mesh(),
        scratch_types=[
            pltpu.VMEM((SC_TOKENS * SC_GATHERS_PER_TOKEN, SC_GATHER_ROWS), I32),
            pltpu.VMEM((SC_TOKENS, D_MODEL), F32),
            pltpu.VMEM((2, SC_GATHER_ROWS, D_MODEL), F32),
            pltpu.VMEM((SC_TOKENS, PEER_SEL), F32),
            pltpu.SemaphoreType.DMA((2,)),
        ],
        compiler_params=_SC_PARAMS,
        name="expert_dots_sc",
    )(u_tab, idx2, xn2)


def _expert_mix_sc(v_tab, idx, w):
    t = idx.shape[0]
    tok_per_worker = t // SC_WORKERS
    idx2 = idx.reshape(t * SC_GATHERS_PER_TOKEN, SC_GATHER_ROWS)

    def body(v_hbm, idx_hbm, w_hbm, y_hbm, idx_v, w_v, rows_v, y_v, sems):
        wid = _sc_worker_id()

        def compute(n, slot):
            tl = n // SC_GATHERS_PER_TOKEN
            g = n % SC_GATHERS_PER_TOKEN
            wvec = w_v[tl, pl.ds(pl.multiple_of(g * SC_GATHER_ROWS, SC_LANES), SC_LANES)]

            for half in range(2):
                base = half * (D_MODEL // 2)

                def row(r, accs):
                    wv = jnp.take_along_axis(wvec, jnp.full((SC_LANES,), r, I32), axis=0)
                    return tuple(accs[c] + wv * rows_v[slot, r, pl.ds(base + c * SC_LANES, SC_LANES)]
                                 for c in range(SC_CHUNKS // 2))

                accs = lax.fori_loop(0, SC_GATHER_ROWS, row,
                                     tuple(jnp.zeros((SC_LANES,), F32) for _ in range(SC_CHUNKS // 2)))
                for c in range(SC_CHUNKS // 2):
                    sl = pl.ds(base + c * SC_LANES, SC_LANES)
                    y_v[tl, sl] = y_v[tl, sl] + accs[c]

        def block(bi, carry):
            tok0 = wid * tok_per_worker + bi * SC_TOKENS
            pltpu.sync_copy(idx_hbm.at[pl.ds(tok0 * SC_GATHERS_PER_TOKEN, SC_TOKENS * SC_GATHERS_PER_TOKEN)], idx_v)
            pltpu.sync_copy(w_hbm.at[pl.ds(tok0, SC_TOKENS)], w_v)
            zero = jnp.zeros((SC_LANES,), F32)
            for tl in range(SC_TOKENS):
                for c in range(SC_CHUNKS):
                    y_v[tl, pl.ds(c * SC_LANES, SC_LANES)] = zero
            _sc_gather_pipeline(v_hbm, idx_v, rows_v, sems, compute)
            pltpu.sync_copy(y_v, y_hbm.at[pl.ds(tok0, SC_TOKENS)])
            return carry

        lax.fori_loop(0, tok_per_worker // SC_TOKENS, block, 0)

    return pl.kernel(
        body,
        out_type=jax.ShapeDtypeStruct((t, D_MODEL), F32),
        mesh=_sc_mesh(),
        scratch_types=[
            pltpu.VMEM((SC_TOKENS * SC_GATHERS_PER_TOKEN, SC_GATHER_ROWS), I32),
            pltpu.VMEM((SC_TOKENS, PEER_SEL), F32),
            pltpu.VMEM((2, SC_GATHER_ROWS, D_MODEL), F32),
            pltpu.VMEM((SC_TOKENS, D_MODEL), F32),
            pltpu.SemaphoreType.DMA((2,)),
        ],
        compiler_params=_SC_PARAMS,
        name="expert_mix_sc",
    )(v_tab, idx2, w)


def _rearranged_in_proj(w_in, b_in):
    def cols(a):
        q = a[..., :Q_W]
        k = a[..., Q_W:Q_W + N_KV_HEADS * HEAD_DIM]
        v = a[..., Q_W + N_KV_HEADS * HEAD_DIM:Q_W + 2 * N_KV_HEADS * HEAD_DIM]
        rest = a[..., Q_W + 2 * N_KV_HEADS * HEAD_DIM:]
        dup = lambda m: jnp.concatenate(
            [m[..., hd * HEAD_DIM:(hd + 1) * HEAD_DIM] for hd in range(N_KV_HEADS) for _ in range(2)], axis=-1)
        return jnp.concatenate([q, dup(k), dup(v), rest], axis=-1)
    return cols(w_in).astype(BF16), cols(b_in)[None, :]


def kernel(x, ln_mix_g, w_in, b_in, attn_sinks, w_attn_up, w_pool_grp, pool_scale, w_pool_up, w_o,
           ln_ffn_g, w_query, sub_keys, u_experts, v_experts, ln_final_g):
    b, s, d = x.shape
    t = b * s
    depth = w_in.shape[0]
    h = x
    for l in range(depth):
        w_r, b_r = _rearranged_in_proj(w_in[l], b_in[l])
        q, kv, pz, gate = _inproj(h.reshape(t, d), ln_mix_g[l][None, :], w_r, b_r)
        o = _attention(attn_sinks[l], q.reshape(b, s, Q_W), kv.reshape(b, s, KV_COLS))
        h, xn2 = _merge(o, gate.reshape(b, s, 2 * d), h, pz.reshape(b, s, POOL_W),
                        w_attn_up[l].astype(BF16), w_pool_grp[l].astype(BF16), pool_scale[l][None, :],
                        w_pool_up[l].astype(BF16), w_o[l].astype(BF16), ln_ffn_g[l][None, :])
        xn2 = xn2.reshape(t, d)
        wq_t = w_query[l].T.astype(BF16)
        keys = sub_keys[l].reshape(2 * PEER_HEADS, SUBLANES, N_KEYS // SUBLANES, D_HALF)
        keys = keys[:, jnp.array(SUBLANE_BITREV)].transpose(0, 2, 1, 3)
        keys = keys.reshape(2 * PEER_HEADS, N_KEYS, D_HALF).astype(BF16)
        u_t = u_experts[l].astype(BF16).reshape(N_KEYS, N_KEYS, d).transpose(2, 1, 0).reshape(d, N_EXPERTS)
        last = l + 1 == depth
        g_out = ln_final_g[None, :]
        h2 = h.reshape(t, d)
        t_tc = t - (t // TC_EXPERT) // SC_SHARE * TC_EXPERT
        if t_tc < t:
            idx_sc, g_sc = _route(xn2, wq_t, keys, t_tc)
            a_sc = _expert_dots_sc(u_experts[l], idx_sc, xn2)
        idx, g, a = _route_act(xn2, u_t, wq_t, keys, t_tc)
        if t_tc < t:
            y_sc = _expert_mix_sc(v_experts[l], idx_sc, _expert_weight(a_sc, g_sc, a))
        h2 = _expert_mix(idx, a, g, v_experts[l].astype(BF16), h2, g_out, last)
        if t_tc < t:
            h2 = _residual(h2, h.reshape(t, d), y_sc, g_out, last)
        h = h2.reshape(b, s, d)
    return h
```

```python
import functools
import math

import jax
import jax.numpy as jnp
from jax import lax
from jax.experimental import pallas as pl
from jax.experimental.pallas import tpu as pltpu
from jax.experimental.pallas import tpu_sc as plsc

F32 = jnp.float32
BF16 = jnp.bfloat16
I32 = jnp.int32

D_MODEL = 1024
N_HEADS = 16
N_KV_HEADS = 2
GROUP = N_HEADS // N_KV_HEADS
HEAD_DIM = 64
BLOCK = 128
ATTN_SCALE = 1.0 / math.sqrt(HEAD_DIM)
NEG_INF = -1e30
POOL_WINDOWS = (2, 4, 8, 16)
POOL_GROUP = 128
POOL_W = len(POOL_WINDOWS) * POOL_GROUP
POOL_HIST = max(POOL_WINDOWS)
Q_W = N_HEADS * HEAD_DIM
PEER_HEADS = 8
N_KEYS = 128
N_EXPERTS = N_KEYS * N_KEYS
D_HALF = 128
PEER_TOPK = 16
PEER_SEL = PEER_HEADS * PEER_TOPK
EPS = 1e-5

VMEM_LIMIT_BYTES = 56 * 1024 * 1024
SUBLANES = 8
LANES = 128
SUBLANE_BITREV = (0, 4, 2, 6, 1, 5, 3, 7)

C_Q = 0
C_K = C_Q + Q_W
C_V = C_K + 2 * BLOCK
C_P = C_V + 2 * BLOCK
C_G = C_P + POOL_W
C_END = C_G + 2 * D_MODEL
KV_COLS = C_P - C_K

TM_PROJ = 512
ATTN_Q_BLOCKS = 2
TS_MERGE = 512
ROUTE_SPLIT = 1
TC_ROUTE = 256
TC_EXPERT = 512
EXPERT_BLOCK = 1024
KEYS_PER_BLOCK = EXPERT_BLOCK // N_KEYS
TOKENS_PER_TRIP = 16


def _tc_params(*sem):
    return pltpu.CompilerParams(dimension_semantics=sem, vmem_limit_bytes=VMEM_LIMIT_BYTES)


def _rms_scale(x):
    return lax.rsqrt(jnp.mean(x * x, axis=-1, keepdims=True) + EPS)


def _inproj_kernel(x_ref, g_ref, w_ref, b_ref, q_ref, kv_ref, pz_ref, gate_ref):
    x = x_ref[...]
    xn = (x * _rms_scale(x) * g_ref[...]).astype(BF16)

    def proj(lo, hi):
        return jnp.dot(xn, w_ref[:, lo:hi], preferred_element_type=F32) + b_ref[:, lo:hi]

    q_ref[...] = proj(C_Q, C_K).astype(BF16)
    kv_ref[...] = proj(C_K, C_P).astype(BF16)
    pz_ref[...] = proj(C_P, C_G)
    gate_ref[...] = proj(C_G, C_END).astype(BF16)


def _inproj(x2, g, w, b):
    t = x2.shape[0]
    row = lambda i: (i, 0)
    fixed = lambda i: (0, 0)
    return pl.pallas_call(
        _inproj_kernel,
        grid=(t // TM_PROJ,),
        in_specs=[
            pl.BlockSpec((TM_PROJ, D_MODEL), row),
            pl.BlockSpec((1, D_MODEL), fixed),
            pl.BlockSpec((D_MODEL, C_END), fixed),
            pl.BlockSpec((1, C_END), fixed),
        ],
        out_specs=[
            pl.BlockSpec((TM_PROJ, Q_W), row),
            pl.BlockSpec((TM_PROJ, KV_COLS), row),
            pl.BlockSpec((TM_PROJ, POOL_W), row),
            pl.BlockSpec((TM_PROJ, 2 * D_MODEL), row),
        ],
        out_shape=[
            jax.ShapeDtypeStruct((t, Q_W), BF16),
            jax.ShapeDtypeStruct((t, KV_COLS), BF16),
            jax.ShapeDtypeStruct((t, POOL_W), F32),
            jax.ShapeDtypeStruct((t, 2 * D_MODEL), BF16),
        ],
        compiler_params=_tc_params("parallel"),
        name="inproj",
    )(x2, g, w, b)


def _attn_kernel(sink_ref, q_ref, kvc_ref, kvp_ref, o_ref, s_scr, p_scr):
    j = pl.program_id(1)
    row = lax.broadcasted_iota(I32, (BLOCK, BLOCK), 0)
    col = lax.broadcasted_iota(I32, (BLOCK, BLOCK), 1)
    low = lax.broadcasted_iota(I32, (1, BLOCK), 1) < HEAD_DIM
    zero = jnp.zeros((), BF16)
    for qb in range(ATTN_Q_BLOCKS):
        rows = slice(qb * BLOCK, (qb + 1) * BLOCK)
        kvc = kvc_ref[0, rows, :]
        kvp = kvp_ref[0] if qb == 0 else kvc_ref[0, (qb - 1) * BLOCK:qb * BLOCK, :]
        in_window = jnp.logical_and(col > row, j > 0) if qb == 0 else col > row
        mask = jnp.concatenate([in_window, col <= row], axis=1)
        for hkv in range(N_KV_HEADS):
            kc = slice(hkv * BLOCK, (hkv + 1) * BLOCK)
            k2 = jnp.concatenate([kvp[:, kc], kvc[:, kc]], axis=0)
            for pair in range(GROUP // 2):
                c0 = (hkv * (GROUP // 2) + pair) * BLOCK
                q2 = q_ref[0, rows, c0:c0 + BLOCK]
                for half in range(2):
                    qm = jnp.where(low, q2, zero) if half == 0 else jnp.where(low, zero, q2)
                    s = lax.dot_general(qm, k2, (((1,), (1,)), ((), ())), preferred_element_type=F32)
                    s_scr[hkv * GROUP + pair * 2 + half] = jnp.where(mask, s * ATTN_SCALE, NEG_INF)
        recip = []
        for head in range(N_HEADS):
            s = s_scr[head]
            sink = sink_ref[head]
            m = jnp.maximum(jnp.max(s, axis=-1, keepdims=True), sink)
            e = jnp.exp(s - m)
            p_scr[head] = e.astype(BF16)
            recip.append(1.0 / (jnp.sum(e, axis=-1, keepdims=True) + jnp.exp(sink - m)))
        for hkv in range(N_KV_HEADS):
            vc = slice(2 * BLOCK + hkv * BLOCK, 2 * BLOCK + (hkv + 1) * BLOCK)
            v2 = jnp.concatenate([kvp[:, vc], kvc[:, vc]], axis=0)
            v_half = (jnp.where(low, v2, zero), jnp.where(low, zero, v2))
            for pair in range(GROUP // 2):
                c0 = (hkv * (GROUP // 2) + pair) * BLOCK
                head = hkv * GROUP + pair * 2
                o2 = (jnp.dot(p_scr[head], v_half[0], preferred_element_type=F32)
                      + jnp.dot(p_scr[head + 1], v_half[1], preferred_element_type=F32))
                o2 = o2 * jnp.where(low, recip[head], recip[head + 1])
                o_ref[0, rows, c0:c0 + BLOCK] = o2.astype(BF16)


def _attention(sinks, q3, kv3):
    b, s, _ = q3.shape
    tq = ATTN_Q_BLOCKS * BLOCK
    return pl.pallas_call(
        _attn_kernel,
        grid=(b, s // tq),
        in_specs=[
            pl.BlockSpec(memory_space=pltpu.SMEM),
            pl.BlockSpec((1, tq, Q_W), lambda i, j: (i, j, 0)),
            pl.BlockSpec((1, tq, KV_COLS), lambda i, j: (i, j, 0)),
            pl.BlockSpec((1, BLOCK, KV_COLS), lambda i, j: (i, jnp.maximum(j * ATTN_Q_BLOCKS - 1, 0), 0)),
        ],
        out_specs=pl.BlockSpec((1, tq, Q_W), lambda i, j: (i, j, 0)),
        out_shape=jax.ShapeDtypeStruct((b, s, Q_W), BF16),
        scratch_shapes=[pltpu.VMEM((N_HEADS, BLOCK, 2 * BLOCK), F32), pltpu.VMEM((N_HEADS, BLOCK, 2 * BLOCK), BF16)],
        compiler_params=_tc_params("parallel", "arbitrary"),
        name="attention",
    )(sinks, q3, kv3, kv3)


def _sigmoid(x):
    return 1.0 / (1.0 + jnp.exp(-x))


def _merge_kernel(o_ref, gate_ref, x_ref, pz_ref, pzp_ref, wau_ref, wgrp_ref, scale_ref, wpu_ref,
                  wo_ref, g_ref, h_ref, xn_ref):
    j = pl.program_id(1)
    ts = pz_ref.shape[1]
    prev = jnp.where(j > 0, pzp_ref[0], 0.0)
    ext = jnp.concatenate([prev, pz_ref[0]], axis=0)
    t1 = (j * ts + 1 + lax.broadcasted_iota(I32, (ts, 1), 0)).astype(F32)
    ys = []
    for g, w in enumerate(POOL_WINDOWS):
        e = ext[:, g * POOL_GROUP:(g + 1) * POOL_GROUP]
        tsum = e
        span = 1
        while span < w:
            tsum = tsum + pltpu.roll(tsum, span, 0)
            span *= 2
        pooled = tsum[POOL_HIST:] / jnp.minimum(t1, float(w)) - e[POOL_HIST:]
        y = jnp.dot(pooled.astype(BF16), wgrp_ref[g], preferred_element_type=F32)
        ys.append((y * scale_ref[:, g * POOL_GROUP:(g + 1) * POOL_GROUP]).astype(BF16))
    y_p = jnp.dot(jnp.concatenate(ys, axis=1), wpu_ref[...], preferred_element_type=F32)
    y_a = jnp.dot(o_ref[0], wau_ref[...], preferred_element_type=F32)
    gate = gate_ref[0]
    merged = (_sigmoid(gate[:, :D_MODEL].astype(F32)) * y_a
              + _sigmoid(gate[:, D_MODEL:].astype(F32)) * y_p)
    h = x_ref[0] + jnp.dot(merged.astype(BF16), wo_ref[...], preferred_element_type=F32)
    h_ref[0] = h
    xn_ref[0] = h * _rms_scale(h) * g_ref[...]


def _merge(o3, gate3, x, pz3, wau, wgrp, scale, wpu, wo, g):
    b, s, _ = x.shape
    ts = TS_MERGE
    hist_blocks = ts // POOL_HIST
    tile = lambda i, j: (i, j, 0)
    fixed2 = lambda i, j: (0, 0)
    return pl.pallas_call(
        _merge_kernel,
        grid=(b, s // ts),
        in_specs=[
            pl.BlockSpec((1, ts, Q_W), tile),
            pl.BlockSpec((1, ts, 2 * D_MODEL), tile),
            pl.BlockSpec((1, ts, D_MODEL), tile),
            pl.BlockSpec((1, ts, POOL_W), tile),
            pl.BlockSpec((1, POOL_HIST, POOL_W), lambda i, j: (i, jnp.maximum(j * hist_blocks - 1, 0), 0)),
            pl.BlockSpec((Q_W, D_MODEL), fixed2),
            pl.BlockSpec((len(POOL_WINDOWS), POOL_GROUP, POOL_GROUP), lambda i, j: (0, 0, 0)),
            pl.BlockSpec((1, POOL_W), fixed2),
            pl.BlockSpec((POOL_W, D_MODEL), fixed2),
            pl.BlockSpec((D_MODEL, D_MODEL), fixed2),
            pl.BlockSpec((1, D_MODEL), fixed2),
        ],
        out_specs=[pl.BlockSpec((1, ts, D_MODEL), tile), pl.BlockSpec((1, ts, D_MODEL), tile)],
        out_shape=[jax.ShapeDtypeStruct((b, s, D_MODEL), F32), jax.ShapeDtypeStruct((b, s, D_MODEL), F32)],
        compiler_params=_tc_params("parallel", "arbitrary"),
        name="merge",
    )(o3, gate3, x, pz3, pz3, wau, wgrp, scale, wpu, wo, g)


def _first_max(groups, ordered):
    while len(groups) > 1:
        merged = []
        for i in range(0, len(groups) - 1, 2):
            (vl, pl_), (vr, pr) = groups[i], groups[i + 1]
            merged.append((jnp.maximum(vl, vr), jnp.where(vl >= vr, pl_, pr)))
        if len(groups) % 2:
            merged.append(groups[-1])
        groups = merged
    v, p = groups[0]
    rows = SUBLANES
    while rows > 1:
        rows //= 2
        vl, vr, pl_, pr = v[:rows], v[rows:2 * rows], p[:rows], p[rows:2 * rows]
        take = (vl >= vr) if ordered else jnp.logical_or(vl > vr, jnp.logical_and(vl == vr, pl_ < pr))
        v, p = jnp.where(take, vl, vr), jnp.where(take, pl_, pr)
    return v, p


def _top16_rows(groups, ordered):
    vals, poss = [], []
    for _ in range(PEER_TOPK):
        m, am = _first_max(groups, ordered)
        vals.append(m)
        poss.append(am)
        groups = [(jnp.where(p == am, -jnp.inf, v), p) for v, p in groups]
    return jnp.concatenate(vals, axis=0), jnp.concatenate(poss, axis=0)


def _select_row(table, sel, pos16):
    out = []
    for k in range(PEER_TOPK):
        out.append(jnp.sum(jnp.where(pos16 == sel[k:k + 1], table, 0), axis=0, keepdims=True))
    return jnp.concatenate(out, axis=0)


def _route_head(xb, wq_ref, keys_ref, h, idx_scr, gate_scr, tokens):
    c = xb.shape[0]
    sub = lax.broadcasted_iota(I32, (SUBLANES, LANES), 0)
    sub_rev = ((sub & 1) << 2) | (sub & 2) | ((sub & 4) >> 2)
    key_pos = [(sub_rev * (N_KEYS // SUBLANES) + g).astype(F32) for g in range(N_KEYS // SUBLANES)]
    pos16 = lax.broadcasted_iota(I32, (PEER_TOPK, LANES), 0)
    cand_groups = [(a, b0) for a in range(PEER_TOPK) for b0 in range(0, PEER_TOPK // (a + 1), SUBLANES)]

    scores = []
    for half in range(2):
        hp = h * 2 + half
        wq = wq_ref[pl.ds(pl.multiple_of(hp * D_HALF, D_HALF), D_HALF), :]
        q_t = lax.dot_general(wq, xb, (((1,), (1,)), ((), ())), preferred_element_type=F32)
        scores.append(jnp.dot(keys_ref[hp], q_t.astype(BF16), preferred_element_type=F32))

    idx_out, gate_out = [], []
    for l0 in range(0, c, LANES):
        sv, si = [], []
        for sc in scores:
            groups = [(sc[g * SUBLANES:(g + 1) * SUBLANES, l0:l0 + LANES], key_pos[g])
                      for g in range(N_KEYS // SUBLANES)]
            v, p = _top16_rows(groups, ordered=True)
            sv.append(v)
            si.append(p.astype(I32))
        groups = []
        for a, b0 in cand_groups:
            val = sv[0][a:a + 1] + sv[1][b0:b0 + SUBLANES]
            val = jnp.where(sub + b0 < PEER_TOPK // (a + 1), val, -jnp.inf)
            groups.append((val, (sub + (a * PEER_TOPK + b0)).astype(F32)))
        fv, fpos = _top16_rows(groups, ordered=False)
        fpos = fpos.astype(I32)
        i0 = _select_row(si[0], fpos >> 4, pos16)
        i1 = _select_row(si[1], fpos & (PEER_TOPK - 1), pos16)
        e = jnp.exp(fv - fv[0:1])
        idx_out.append(i0 * N_KEYS + i1)
        gate_out.append(e / jnp.sum(e, axis=0, keepdims=True))
    rows = pl.ds(pl.multiple_of(h * PEER_TOPK, PEER_TOPK), PEER_TOPK)
    idx_scr[rows, tokens] = jnp.concatenate(idx_out, axis=1)
    gate_scr[rows, tokens] = jnp.concatenate(gate_out, axis=1)


def _key_rows():
    key = lax.broadcasted_iota(I32, (N_KEYS, PEER_SEL), 0)
    return key, key.astype(F32).astype(BF16)


def _split_keys(idx_rows):
    first, second = idx_rows >> 7, idx_rows & (N_KEYS - 1)
    as_bf16 = lambda v: v.astype(F32).astype(BF16)
    return first, second, as_bf16(first), as_bf16(second)


def _route_act_kernel(xn_ref, ut_ref, wq_ref, keys_ref, idx_ref, gate_ref, a_ref, grid_ref, xb_ref,
                      idx_scr, gate_scr):
    k = pl.program_id(1)
    c = xn_ref.shape[0]

    @pl.when(k == 0)
    def _():
        xb_ref[...] = xn_ref[...].astype(BF16)

    chunk_tokens = pl.ds(pl.multiple_of((k // PEER_HEADS) * TC_ROUTE, TC_ROUTE), TC_ROUTE)
    _route_head(xb_ref[chunk_tokens, :], wq_ref, keys_ref, k % PEER_HEADS, idx_scr, gate_scr, chunk_tokens)

    part = c // ROUTE_SPLIT
    part_tokens = pl.ds(pl.multiple_of((k % ROUTE_SPLIT) * part, part), part)
    dense = jnp.dot(xb_ref[part_tokens, :], ut_ref[...], preferred_element_type=F32)
    by_key = jnp.stack([dense[:, kk * N_KEYS:(kk + 1) * N_KEYS] for kk in range(KEYS_PER_BLOCK)], axis=0)
    second_rows = pl.ds(pl.multiple_of((k // ROUTE_SPLIT) * KEYS_PER_BLOCK, KEYS_PER_BLOCK), KEYS_PER_BLOCK)
    grid_ref[part_tokens, second_rows, :] = jnp.swapaxes(by_key, 0, 1)

    @pl.when(k == pl.num_programs(1) - 1)
    def _():
        idx_ref[...] = idx_scr[...].T
        gate_ref[...] = gate_scr[...].T
        key, key_bf = _key_rows()
        one, zero = jnp.ones((), BF16), jnp.zeros((), BF16)

        def group(gi, carry):
            t0 = pl.multiple_of(gi * TOKENS_PER_TRIP, TOKENS_PER_TRIP)
            _, second, first_bf, _ = _split_keys(idx_ref[pl.ds(t0, TOKENS_PER_TRIP), :])
            rows = []
            for g in range(TOKENS_PER_TRIP):
                pick_first = jnp.where(key_bf == first_bf[g:g + 1], one, zero)
                picked = jnp.dot(grid_ref[t0 + g].astype(BF16), pick_first,
                                 preferred_element_type=F32)
                rows.append(jnp.sum(jnp.where(key == second[g:g + 1], picked, 0.0), axis=0, keepdims=True))
            a_ref[pl.ds(t0, TOKENS_PER_TRIP), :] = jnp.concatenate(rows, axis=0)
            return carry

        lax.fori_loop(0, c // TOKENS_PER_TRIP, group, 0)


def _route_act(xn2, u_t, wq_t, keys, t):
    c = TC_EXPERT
    steps = ROUTE_SPLIT * N_EXPERTS // EXPERT_BLOCK
    assert steps == PEER_HEADS * (c // TC_ROUTE)
    sel = pl.BlockSpec((c, PEER_SEL), lambda i, k: (i, 0))
    once = pl.Buffered(1)
    return pl.pallas_call(
        _route_act_kernel,
        grid=(t // c, steps),
        in_specs=[
            pl.BlockSpec((c, D_MODEL), lambda i, k: (i, 0)),
            pl.BlockSpec((D_MODEL, EXPERT_BLOCK), lambda i, k: (0, k // ROUTE_SPLIT)),
            pl.BlockSpec((2 * PEER_HEADS * D_HALF, D_MODEL), lambda i, k: (0, 0), pipeline_mode=once),
            pl.BlockSpec((2 * PEER_HEADS, N_KEYS, D_HALF), lambda i, k: (0, 0, 0), pipeline_mode=once),
        ],
        out_specs=[sel, sel, sel],
        out_shape=[jax.ShapeDtypeStruct((t, PEER_SEL), I32), jax.ShapeDtypeStruct((t, PEER_SEL), F32),
                   jax.ShapeDtypeStruct((t, PEER_SEL), F32)],
        scratch_shapes=[pltpu.VMEM((c, N_KEYS, N_KEYS), F32), pltpu.VMEM((c, D_MODEL), BF16),
                        pltpu.VMEM((PEER_SEL, c), I32), pltpu.VMEM((PEER_SEL, c), F32)],
        compiler_params=_tc_params("parallel", "arbitrary"),
        name="route_act",
    )(xn2, u_t, wq_t, keys)


def _gelu(a):
    return 0.5 * a * (1.0 + lax.erf(a * math.sqrt(0.5)))


def _expert_mix_kernel(normalize, idx_ref, a_ref, gate_ref, v_ref, h_ref, g_ref, o_ref, grid_ref, w_ref, acc_ref):
    k = pl.program_id(1)
    c = idx_ref.shape[0]

    @pl.when(k == 0)
    def _():
        acc_ref[...] = h_ref[...]
        w_ref[...] = gate_ref[...] * _gelu(a_ref[...])

        _, key_bf = _key_rows()
        one, zero = jnp.ones((), BF16), jnp.zeros((), BF16)

        def group(gi, carry):
            t0 = pl.multiple_of(gi * TOKENS_PER_TRIP, TOKENS_PER_TRIP)
            _, _, first_bf, second_bf = _split_keys(idx_ref[pl.ds(t0, TOKENS_PER_TRIP), :])
            w_bf = w_ref[pl.ds(t0, TOKENS_PER_TRIP), :].astype(BF16)
            for g0 in range(0, TOKENS_PER_TRIP, SUBLANES):
                mats = []
                for g in range(g0, g0 + SUBLANES):
                    weighted = jnp.where(key_bf == first_bf[g:g + 1], w_bf[g:g + 1], zero)
                    pick_second = jnp.where(key_bf == second_bf[g:g + 1], one, zero)
                    mats.append(lax.dot_general(weighted, pick_second, (((1,), (1,)), ((), ())),
                                                preferred_element_type=F32))
                grid_ref[:, pl.ds(t0 + g0, SUBLANES), :] = jnp.swapaxes(jnp.stack(mats, axis=0), 0, 1)
            return carry

        lax.fori_loop(0, c // TOKENS_PER_TRIP, group, 0)

    dense = jnp.concatenate([grid_ref[k * KEYS_PER_BLOCK + kk] for kk in range(KEYS_PER_BLOCK)], axis=1)
    acc_ref[...] += jnp.dot(dense.astype(BF16), v_ref[...], preferred_element_type=F32)

    @pl.when(k == pl.num_programs(1) - 1)
    def _():
        hh = acc_ref[...]
        o_ref[...] = hh * _rms_scale(hh) * g_ref[...] if normalize else hh


def _expert_mix(idx, a, gate, v_tab, h2, g, normalize):
    t = idx.shape[0]
    c = TC_EXPERT
    sel = pl.BlockSpec((c, PEER_SEL), lambda i, k: (i, 0))
    tok = pl.BlockSpec((c, D_MODEL), lambda i, k: (i, 0))
    return pl.pallas_call(
        functools.partial(_expert_mix_kernel, normalize),
        grid=(t // c, N_EXPERTS // EXPERT_BLOCK),
        in_specs=[sel, sel, sel, pl.BlockSpec((EXPERT_BLOCK, D_MODEL), lambda i, k: (k, 0)), tok,
                  pl.BlockSpec((1, D_MODEL), lambda i, k: (0, 0))],
        out_specs=tok,
        out_shape=jax.ShapeDtypeStruct(h2.shape, F32),
        scratch_shapes=[pltpu.VMEM((N_KEYS, c, N_KEYS), F32), pltpu.VMEM((c, PEER_SEL), F32),
                        pltpu.VMEM((c, D_MODEL), F32)],
        compiler_params=_tc_params("parallel", "arbitrary"),
        name="expert_mix",
    )(idx, a, gate, v_tab, h2, g)


SC_CORES = 2
SC_SUBCORES = 16
SC_WORKERS = SC_CORES * SC_SUBCORES
SC_LANES = 16
SC_GATHER_ROWS = 16
SC_GATHERS_PER_TOKEN = PEER_SEL // SC_GATHER_ROWS
SC_TOKENS = 8
SC_CHUNKS = D_MODEL // SC_LANES
SC_SHARE = 4
TM_ELEM = 1024

_SC_PARAMS = pltpu.CompilerParams(needs_layout_passes=False)


def _route_kernel(xn_ref, wq_ref, keys_ref, idx_ref, gate_ref, idx_scr, gate_scr):
    xb = xn_ref[...].astype(BF16)
    everything = pl.ds(0, xb.shape[0])

    def head_body(h, carry):
        _route_head(xb, wq_ref, keys_ref, h, idx_scr, gate_scr, everything)
        return carry

    lax.fori_loop(0, PEER_HEADS, head_body, 0)
    idx_ref[...] = idx_scr[...].T
    gate_ref[...] = gate_scr[...].T


def _route(xn2, wq_t, keys, first):
    t = xn2.shape[0] - first
    c = TC_ROUTE
    sel = pl.BlockSpec((c, PEER_SEL), lambda i: (i, 0))
    return pl.pallas_call(
        _route_kernel,
        grid=(t // c,),
        in_specs=[
            pl.BlockSpec((c, D_MODEL), lambda i: (i + first // c, 0)),
            pl.BlockSpec((2 * PEER_HEADS * D_HALF, D_MODEL), lambda i: (0, 0)),
            pl.BlockSpec((2 * PEER_HEADS, N_KEYS, D_HALF), lambda i: (0, 0, 0)),
        ],
        out_specs=[sel, sel],
        out_shape=[jax.ShapeDtypeStruct((t, PEER_SEL), I32), jax.ShapeDtypeStruct((t, PEER_SEL), F32)],
        scratch_shapes=[pltpu.VMEM((PEER_SEL, c), I32), pltpu.VMEM((PEER_SEL, c), F32)],
        compiler_params=_tc_params("parallel"),
        name="route",
    )(xn2, wq_t, keys)


def _expert_weight_kernel(a_ref, g_ref, after_ref, w_ref):
    del after_ref
    w_ref[...] = g_ref[...] * _gelu(a_ref[...])


def _expert_weight(a, g, after):
    t = a.shape[0]
    spec = pl.BlockSpec((TM_ELEM, PEER_SEL), lambda i: (i, 0))
    return pl.pallas_call(
        _expert_weight_kernel,
        grid=(t // TM_ELEM,),
        in_specs=[spec, spec, pl.BlockSpec(memory_space=pl.ANY)],
        out_specs=spec,
        out_shape=jax.ShapeDtypeStruct((t, PEER_SEL), F32),
        compiler_params=_tc_params("parallel"),
        name="expert_weight",
    )(a, g, after)


def _residual_kernel(normalize, out_in_ref, h_ref, y_ref, g_ref, o_ref):
    del out_in_ref
    h = h_ref[...] + y_ref[...]
    o_ref[...] = h * _rms_scale(h) * g_ref[...] if normalize else h


def _residual(out, h2, y2, g, normalize):
    first = h2.shape[0] - y2.shape[0]
    tail = pl.BlockSpec((TM_ELEM, D_MODEL), lambda i: (i + first // TM_ELEM, 0))
    return pl.pallas_call(
        functools.partial(_residual_kernel, normalize),
        grid=(y2.shape[0] // TM_ELEM,),
        in_specs=[pl.BlockSpec(memory_space=pl.ANY), tail, pl.BlockSpec((TM_ELEM, D_MODEL), lambda i: (i, 0)),
                  pl.BlockSpec((1, D_MODEL), lambda i: (0, 0))],
        out_specs=tail,
        out_shape=jax.ShapeDtypeStruct(out.shape, F32),
        input_output_aliases={0: 0},
        compiler_params=_tc_params("parallel"),
        name="residual",
    )(out, h2, y2, g)


def _sc_mesh():
    return plsc.VectorSubcoreMesh(core_axis_name="c", subcore_axis_name="s")


def _sc_worker_id():
    return lax.axis_index("s") * SC_CORES + lax.axis_index("c")


def _sc_gather_pipeline(tab_hbm, idx_v, rows_v, sems, compute):
    n_gathers = idx_v.shape[0]

    def gather(n, slot):
        return pltpu.make_async_copy(tab_hbm.at[idx_v[n, :]], rows_v.at[slot], sems.at[slot])

    gather(0, 0).start()

    def step(n2, carry):
        for slot in range(2):
            n = n2 * 2 + slot

            @pl.when(n + 1 < n_gathers)
            def _():
                gather(n + 1, 1 - slot).start()

            gather(n, slot).wait()
            compute(n, slot)
        return carry

    lax.fori_loop(0, n_gathers // 2, step, 0)


def _expert_dots_sc(u_tab, idx, xn2):
    t = idx.shape[0]
    first = xn2.shape[0] - t
    tok_per_worker = t // SC_WORKERS
    idx2 = idx.reshape(t * SC_GATHERS_PER_TOKEN, SC_GATHER_ROWS)

    def body(u_hbm, idx_hbm, x_hbm, a_hbm, idx_v, x_v, rows_v, a_v, sems):
        wid = _sc_worker_id()
        lanes = lax.iota(I32, SC_LANES)

        def compute(n, slot):
            tl = n // SC_GATHERS_PER_TOKEN
            g = n % SC_GATHERS_PER_TOKEN

            def chunk(c, accs):
                off = pl.multiple_of(c * SC_LANES, SC_LANES)
                xv = x_v[tl, pl.ds(off, SC_LANES)]
                return tuple(accs[r] + rows_v[slot, r, pl.ds(off, SC_LANES)] * xv for r in range(SC_GATHER_ROWS))

            accs = lax.fori_loop(0, SC_CHUNKS, chunk,
                                 tuple(jnp.zeros((SC_LANES,), F32) for _ in range(SC_GATHER_ROWS)))
            tot = jnp.zeros((SC_LANES,), F32)
            for r in range(SC_GATHER_ROWS):
                tot = jnp.where(lanes == r, jnp.sum(accs[r]), tot)
            a_v[tl, pl.ds(pl.multiple_of(g * SC_GATHER_ROWS, SC_LANES), SC_LANES)] = tot

        def block(bi, carry):
            tok0 = wid * tok_per_worker + bi * SC_TOKENS
            pltpu.sync_copy(idx_hbm.at[pl.ds(tok0 * SC_GATHERS_PER_TOKEN, SC_TOKENS * SC_GATHERS_PER_TOKEN)], idx_v)
            pltpu.sync_copy(x_hbm.at[pl.ds(first + tok0, SC_TOKENS)], x_v)
            _sc_gather_pipeline(u_hbm, idx_v, rows_v, sems, compute)
            pltpu.sync_copy(a_v, a_hbm.at[pl.ds(tok0, SC_TOKENS)])
            return carry

        lax.fori_loop(0, tok_per_worker // SC_TOKENS, block, 0)

    return pl.kernel(
        body,
        out_type=jax.ShapeDtypeStruct((t, PEER_SEL), F32),
        mesh=_sc_mesh(),
        scratch_types=[
            pltpu.VMEM((SC_TOKENS * SC_GATHERS_PER_TOKEN, SC_GATHER_ROWS), I32),
            pltpu.VMEM((SC_TOKENS, D_MODEL), F32),
            pltpu.VMEM((2, SC_GATHER_ROWS, D_MODEL), F32),
            pltpu.VMEM((SC_TOKENS, PEER_SEL), F32),
            pltpu.SemaphoreType.DMA((2,)),
        ],
        compiler_params=_SC_PARAMS,
        name="expert_dots_sc",
    )(u_tab, idx2, xn2)


def _expert_mix_sc(v_tab, idx, w):
    t = idx.shape[0]
    tok_per_worker = t // SC_WORKERS
    idx2 = idx.reshape(t * SC_GATHERS_PER_TOKEN, SC_GATHER_ROWS)

    def body(v_hbm, idx_hbm, w_hbm, y_hbm, idx_v, w_v, rows_v, y_v, sems):
        wid = _sc_worker_id()

        def compute(n, slot):
            tl = n // SC_GATHERS_PER_TOKEN
            g = n % SC_GATHERS_PER_TOKEN
            wvec = w_v[tl, pl.ds(pl.multiple_of(g * SC_GATHER_ROWS, SC_LANES), SC_LANES)]

            for half in range(2):
                base = half * (D_MODEL // 2)

                def row(r, accs):
                    wv = jnp.take_along_axis(wvec, jnp.full((SC_LANES,), r, I32), axis=0)
                    return tuple(accs[c] + wv * rows_v[slot, r, pl.ds(base + c * SC_LANES, SC_LANES)]
                                 for c in range(SC_CHUNKS // 2))

                accs = lax.fori_loop(0, SC_GATHER_ROWS, row,
                                     tuple(jnp.zeros((SC_LANES,), F32) for _ in range(SC_CHUNKS // 2)))
                for c in range(SC_CHUNKS // 2):
                    sl = pl.ds(base + c * SC_LANES, SC_LANES)
                    y_v[tl, sl] = y_v[tl, sl] + accs[c]

        def block(bi, carry):
            tok0 = wid * tok_per_worker + bi * SC_TOKENS
            pltpu.sync_copy(idx_hbm.at[pl.ds(tok0 * SC_GATHERS_PER_TOKEN, SC_TOKENS * SC_GATHERS_PER_TOKEN)], idx_v)
            pltpu.sync_copy(w_hbm.at[pl.ds(tok0, SC_TOKENS)], w_v)
            zero = jnp.zeros((SC_LANES,), F32)
            for tl in range(SC_TOKENS):
                for c in range(SC_CHUNKS):
                    y_v[tl, pl.ds(c * SC_LANES, SC_LANES)] = zero
            _sc_gather_pipeline(v_hbm, idx_v, rows_v, sems, compute)
            pltpu.sync_copy(y_v, y_hbm.at[pl.ds(tok0, SC_TOKENS)])
            return carry

        lax.fori_loop(0, tok_per_worker // SC_TOKENS, block, 0)

    return pl.kernel(
        body,
        out_type=jax.ShapeDtypeStruct((t, D_MODEL), F32),
        mesh=_sc_mesh(),
        scratch_types=[
            pltpu.VMEM((SC_TOKENS * SC_GATHERS_PER_TOKEN, SC_GATHER_ROWS), I32),
            pltpu.VMEM((SC_TOKENS, PEER_SEL), F32),
            pltpu.VMEM((2, SC_GATHER_ROWS, D_MODEL), F32),
            pltpu.VMEM((SC_TOKENS, D_MODEL), F32),
            pltpu.SemaphoreType.DMA((2,)),
        ],
        compiler_params=_SC_PARAMS,
        name="expert_mix_sc",
    )(v_tab, idx2, w)


def _rearranged_in_proj(w_in, b_in):
    def cols(a):
        q = a[..., :Q_W]
        k = a[..., Q_W:Q_W + N_KV_HEADS * HEAD_DIM]
        v = a[..., Q_W + N_KV_HEADS * HEAD_DIM:Q_W + 2 * N_KV_HEADS * HEAD_DIM]
        rest = a[..., Q_W + 2 * N_KV_HEADS * HEAD_DIM:]
        dup = lambda m: jnp.concatenate(
            [m[..., hd * HEAD_DIM:(hd + 1) * HEAD_DIM] for hd in range(N_KV_HEADS) for _ in range(2)], axis=-1)
        return jnp.concatenate([q, dup(k), dup(v), rest], axis=-1)
    return cols(w_in).astype(BF16), cols(b_in)[None, :]


def kernel(x, ln_mix_g, w_in, b_in, attn_sinks, w_attn_up, w_pool_grp, pool_scale, w_pool_up, w_o,
           ln_ffn_g, w_query, sub_keys, u_experts, v_experts, ln_final_g):
    b, s, d = x.shape
    t = b * s
    depth = w_in.shape[0]
    h = x
    for l in range(depth):
        w_r, b_r = _rearranged_in_proj(w_in[l], b_in[l])
        q, kv, pz, gate = _inproj(h.reshape(t, d), ln_mix_g[l][None, :], w_r, b_r)
        o = _attention(attn_sinks[l], q.reshape(b, s, Q_W), kv.reshape(b, s, KV_COLS))
        h, xn2 = _merge(o, gate.reshape(b, s, 2 * d), h, pz.reshape(b, s, POOL_W),
                        w_attn_up[l].astype(BF16), w_pool_grp[l].astype(BF16), pool_scale[l][None, :],
                        w_pool_up[l].astype(BF16), w_o[l].astype(BF16), ln_ffn_g[l][None, :])
        xn2 = xn2.reshape(t, d)
        wq_t = w_query[l].T.astype(BF16)
        keys = sub_keys[l].reshape(2 * PEER_HEADS, SUBLANES, N_KEYS // SUBLANES, D_HALF)
        keys = keys[:, jnp.array(SUBLANE_BITREV)].transpose(0, 2, 1, 3)
        keys = keys.reshape(2 * PEER_HEADS, N_KEYS, D_HALF).astype(BF16)
        u_t = u_experts[l].astype(BF16).reshape(N_KEYS, N_KEYS, d).transpose(2, 1, 0).reshape(d, N_EXPERTS)
        last = l + 1 == depth
        g_out = ln_final_g[None, :]
        h2 = h.reshape(t, d)
        t_tc = t - (t // TC_EXPERT) // SC_SHARE * TC_EXPERT
        if t_tc < t:
            idx_sc, g_sc = _route(xn2, wq_t, keys, t_tc)
            a_sc = _expert_dots_sc(u_experts[l], idx_sc, xn2)
        idx, g, a = _route_act(xn2, u_t, wq_t, keys, t_tc)
        if t_tc < t:
            y_sc = _expert_mix_sc(v_experts[l], idx_sc, _expert_weight(a_sc, g_sc, a))
        h2 = _expert_mix(idx, a, g, v_experts[l].astype(BF16), h2, g_out, last)
        if t_tc < t:
            h2 = _residual(h2, h.reshape(t, d), y_sc, g_out, last)
        h = h2.reshape(b, s, d)
    return h
```

```python
import functools
import math

import jax
import jax.numpy as jnp
import numpy as np
from jax import lax
from jax.experimental import pallas as pl
from jax.experimental.pallas import tpu as pltpu
from jax.experimental.pallas import tpu_sc as plsc

F32 = jnp.float32
BF16 = jnp.bfloat16
I32 = jnp.int32

D_MODEL = 1024
N_HEADS = 16
N_KV_HEADS = 2
GROUP = N_HEADS // N_KV_HEADS
HEAD_DIM = 64
BLOCK = 128
ATTN_SCALE = 1.0 / math.sqrt(HEAD_DIM)
NEG_INF = -1e30
POOL_WINDOWS = (2, 4, 8, 16)
POOL_GROUP = 128
POOL_W = len(POOL_WINDOWS) * POOL_GROUP
POOL_HIST = max(POOL_WINDOWS)
Q_W = N_HEADS * HEAD_DIM
PEER_HEADS = 8
N_KEYS = 128
N_EXPERTS = N_KEYS * N_KEYS
D_HALF = 128
PEER_TOPK = 16
PEER_SEL = PEER_HEADS * PEER_TOPK
EPS = 1e-5

VMEM_LIMIT_BYTES = 56 * 1024 * 1024
SUBLANES = 8
LANES = 128
SUBLANE_BITREV = (0, 4, 2, 6, 1, 5, 3, 7)

C_Q = 0
C_K = C_Q + Q_W
C_V = C_K + 2 * BLOCK
C_P = C_V + 2 * BLOCK
C_G = C_P + POOL_W
C_END = C_G + 2 * D_MODEL
KV_COLS = C_P - C_K

TM_PROJ = 512
ATTN_Q_BLOCKS = 2
TS_MERGE = 512
ROUTE_SPLIT = 1
TC_ROUTE = 256
TC_EXPERT = 512
EXPERT_BLOCK = 1024
KEYS_PER_BLOCK = EXPERT_BLOCK // N_KEYS
TOKENS_PER_TRIP = 16


def _tc_params(*sem):
    return pltpu.CompilerParams(dimension_semantics=sem, vmem_limit_bytes=VMEM_LIMIT_BYTES)


def _rms_scale(x):
    return lax.rsqrt(jnp.mean(x * x, axis=-1, keepdims=True) + EPS)


def _inproj_kernel(x_ref, g_ref, w_ref, b_ref, q_ref, kv_ref, pz_ref, gate_ref):
    x = x_ref[...]
    xn = (x * _rms_scale(x) * g_ref[...]).astype(BF16)

    def proj(lo, hi):
        return jnp.dot(xn, w_ref[:, lo:hi], preferred_element_type=F32) + b_ref[:, lo:hi]

    q_ref[...] = proj(C_Q, C_K).astype(BF16)
    kv_ref[...] = proj(C_K, C_P).astype(BF16)
    pz_ref[...] = proj(C_P, C_G)
    gate_ref[...] = proj(C_G, C_END).astype(BF16)


def _inproj(x2, g, w, b):
    t = x2.shape[0]
    row = lambda i: (i, 0)
    fixed = lambda i: (0, 0)
    return pl.pallas_call(
        _inproj_kernel,
        grid=(t // TM_PROJ,),
        in_specs=[
            pl.BlockSpec((TM_PROJ, D_MODEL), row),
            pl.BlockSpec((1, D_MODEL), fixed),
            pl.BlockSpec((D_MODEL, C_END), fixed),
            pl.BlockSpec((1, C_END), fixed),
        ],
        out_specs=[
            pl.BlockSpec((TM_PROJ, Q_W), row),
            pl.BlockSpec((TM_PROJ, KV_COLS), row),
            pl.BlockSpec((TM_PROJ, POOL_W), row),
            pl.BlockSpec((TM_PROJ, 2 * D_MODEL), row),
        ],
        out_shape=[
            jax.ShapeDtypeStruct((t, Q_W), BF16),
            jax.ShapeDtypeStruct((t, KV_COLS), BF16),
            jax.ShapeDtypeStruct((t, POOL_W), F32),
            jax.ShapeDtypeStruct((t, 2 * D_MODEL), BF16),
        ],
        compiler_params=_tc_params("parallel"),
        name="inproj",
    )(x2, g, w, b)


def _attn_kernel(sink_ref, q_ref, kvc_ref, kvp_ref, o_ref, s_scr, p_scr):
    j = pl.program_id(1)
    row = lax.broadcasted_iota(I32, (BLOCK, BLOCK), 0)
    col = lax.broadcasted_iota(I32, (BLOCK, BLOCK), 1)
    low = lax.broadcasted_iota(I32, (1, BLOCK), 1) < HEAD_DIM
    zero = jnp.zeros((), BF16)
    for qb in range(ATTN_Q_BLOCKS):
        rows = slice(qb * BLOCK, (qb + 1) * BLOCK)
        kvc = kvc_ref[0, rows, :]
        kvp = kvp_ref[0] if qb == 0 else kvc_ref[0, (qb - 1) * BLOCK:qb * BLOCK, :]
        in_window = jnp.logical_and(col > row, j > 0) if qb == 0 else col > row
        mask = jnp.concatenate([in_window, col <= row], axis=1)
        for hkv in range(N_KV_HEADS):
            kc = slice(hkv * BLOCK, (hkv + 1) * BLOCK)
            k2 = jnp.concatenate([kvp[:, kc], kvc[:, kc]], axis=0)
            for pair in range(GROUP // 2):
                c0 = (hkv * (GROUP // 2) + pair) * BLOCK
                q2 = q_ref[0, rows, c0:c0 + BLOCK]
                for half in range(2):
                    qm = jnp.where(low, q2, zero) if half == 0 else jnp.where(low, zero, q2)
                    s = lax.dot_general(qm, k2, (((1,), (1,)), ((), ())), preferred_element_type=F32)
                    s_scr[hkv * GROUP + pair * 2 + half] = jnp.where(mask, s * ATTN_SCALE, NEG_INF)
        recip = []
        for head in range(N_HEADS):
            s = s_scr[head]
            sink = sink_ref[head]
            m = jnp.maximum(jnp.max(s, axis=-1, keepdims=True), sink)
            e = jnp.exp(s - m)
            p_scr[head] = e.astype(BF16)
            recip.append(1.0 / (jnp.sum(e, axis=-1, keepdims=True) + jnp.exp(sink - m)))
        for hkv in range(N_KV_HEADS):
            vc = slice(2 * BLOCK + hkv * BLOCK, 2 * BLOCK + (hkv + 1) * BLOCK)
            v2 = jnp.concatenate([kvp[:, vc], kvc[:, vc]], axis=0)
            v_half = (jnp.where(low, v2, zero), jnp.where(low, zero, v2))
            for pair in range(GROUP // 2):
                c0 = (hkv * (GROUP // 2) + pair) * BLOCK
                head = hkv * GROUP + pair * 2
                o2 = (jnp.dot(p_scr[head], v_half[0], preferred_element_type=F32)
                      + jnp.dot(p_scr[head + 1], v_half[1], preferred_element_type=F32))
                o2 = o2 * jnp.where(low, recip[head], recip[head + 1])
                o_ref[0, rows, c0:c0 + BLOCK] = o2.astype(BF16)


def _attention(sinks, q3, kv3):
    b, s, _ = q3.shape
    tq = ATTN_Q_BLOCKS * BLOCK
    return pl.pallas_call(
        _attn_kernel,
        grid=(b, s // tq),
        in_specs=[
            pl.BlockSpec(memory_space=pltpu.SMEM),
            pl.BlockSpec((1, tq, Q_W), lambda i, j: (i, j, 0)),
            pl.BlockSpec((1, tq, KV_COLS), lambda i, j: (i, j, 0)),
            pl.BlockSpec((1, BLOCK, KV_COLS), lambda i, j: (i, jnp.maximum(j * ATTN_Q_BLOCKS - 1, 0), 0)),
        ],
        out_specs=pl.BlockSpec((1, tq, Q_W), lambda i, j: (i, j, 0)),
        out_shape=jax.ShapeDtypeStruct((b, s, Q_W), BF16),
        scratch_shapes=[pltpu.VMEM((N_HEADS, BLOCK, 2 * BLOCK), F32), pltpu.VMEM((N_HEADS, BLOCK, 2 * BLOCK), BF16)],
        compiler_params=_tc_params("parallel", "arbitrary"),
        name="attention",
    )(sinks, q3, kv3, kv3)


def _sigmoid(x):
    return 1.0 / (1.0 + jnp.exp(-x))


def _merge_kernel(o_ref, gate_ref, x_ref, pz_ref, pzp_ref, wau_ref, wgrp_ref, scale_ref, wpu_ref,
                  wo_ref, g_ref, h_ref, xn_ref, xb_ref, xt_ref):
    j = pl.program_id(1)
    ts = pz_ref.shape[1]
    prev = jnp.where(j > 0, pzp_ref[0], 0.0)
    ext = jnp.concatenate([prev, pz_ref[0]], axis=0)
    t1 = (j * ts + 1 + lax.broadcasted_iota(I32, (ts, 1), 0)).astype(F32)
    ys = []
    for g, w in enumerate(POOL_WINDOWS):
        e = ext[:, g * POOL_GROUP:(g + 1) * POOL_GROUP]
        tsum = e
        span = 1
        while span < w:
            tsum = tsum + pltpu.roll(tsum, span, 0)
            span *= 2
        pooled = tsum[POOL_HIST:] / jnp.minimum(t1, float(w)) - e[POOL_HIST:]
        y = jnp.dot(pooled.astype(BF16), wgrp_ref[g], preferred_element_type=F32)
        ys.append((y * scale_ref[:, g * POOL_GROUP:(g + 1) * POOL_GROUP]).astype(BF16))
    y_p = jnp.dot(jnp.concatenate(ys, axis=1), wpu_ref[...], preferred_element_type=F32)
    y_a = jnp.dot(o_ref[0], wau_ref[...], preferred_element_type=F32)
    gate = gate_ref[0]
    merged = (_sigmoid(gate[:, :D_MODEL].astype(F32)) * y_a
              + _sigmoid(gate[:, D_MODEL:].astype(F32)) * y_p)
    h = x_ref[0] + jnp.dot(merged.astype(BF16), wo_ref[...], preferred_element_type=F32)
    h_ref[0] = h
    xn = h * _rms_scale(h) * g_ref[...]
    xn_ref[0] = xn
    xb_ref[0] = xn.astype(BF16)
    xt_ref[...] = xn.T.astype(BF16)


def _merge(o3, gate3, x, pz3, wau, wgrp, scale, wpu, wo, g):
    b, s, _ = x.shape
    ts = TS_MERGE
    hist_blocks = ts // POOL_HIST
    tile = lambda i, j: (i, j, 0)
    fixed2 = lambda i, j: (0, 0)
    return pl.pallas_call(
        _merge_kernel,
        grid=(b, s // ts),
        in_specs=[
            pl.BlockSpec((1, ts, Q_W), tile),
            pl.BlockSpec((1, ts, 2 * D_MODEL), tile),
            pl.BlockSpec((1, ts, D_MODEL), tile),
            pl.BlockSpec((1, ts, POOL_W), tile),
            pl.BlockSpec((1, POOL_HIST, POOL_W), lambda i, j: (i, jnp.maximum(j * hist_blocks - 1, 0), 0)),
            pl.BlockSpec((Q_W, D_MODEL), fixed2),
            pl.BlockSpec((len(POOL_WINDOWS), POOL_GROUP, POOL_GROUP), lambda i, j: (0, 0, 0)),
            pl.BlockSpec((1, POOL_W), fixed2),
            pl.BlockSpec((POOL_W, D_MODEL), fixed2),
            pl.BlockSpec((D_MODEL, D_MODEL), fixed2),
            pl.BlockSpec((1, D_MODEL), fixed2),
        ],
        out_specs=[pl.BlockSpec((1, ts, D_MODEL), tile), pl.BlockSpec((1, ts, D_MODEL), tile),
                   pl.BlockSpec((1, ts, D_MODEL), tile),
                   pl.BlockSpec((D_MODEL, ts), lambda i, j: (0, i * (s // ts) + j))],
        out_shape=[jax.ShapeDtypeStruct((b, s, D_MODEL), F32), jax.ShapeDtypeStruct((b, s, D_MODEL), F32),
                   jax.ShapeDtypeStruct((b, s, D_MODEL), BF16), jax.ShapeDtypeStruct((D_MODEL, b * s), BF16)],
        compiler_params=_tc_params("parallel", "arbitrary"),
        name="merge",
    )(o3, gate3, x, pz3, pz3, wau, wgrp, scale, wpu, wo, g)


def _rows(ref, g):
    return ref[g * SUBLANES:(g + 1) * SUBLANES, :]


def _better_left(left, right):
    (vl, pl_), (vr, pr) = left, right
    return jnp.maximum(vl, vr), jnp.where(vl >= vr, pl_, pr)


def _best_sublane(v, p, ordered):
    rows = SUBLANES
    while rows > 1:
        rows //= 2
        vl, vr, pl_, pr = v[:rows], v[rows:2 * rows], p[:rows], p[rows:2 * rows]
        take = (vl >= vr) if ordered else jnp.logical_or(vl > vr, jnp.logical_and(vl == vr, pl_ < pr))
        v, p = jnp.where(take, vl, vr), jnp.where(take, pl_, pr)
    return v, p


def _record(found, k, m, am):
    vals, poss = found
    rank = lax.broadcasted_iota(I32, vals.shape, 0)
    return jnp.where(rank == k, m, vals), jnp.where(rank == k, am, poss)


KEY_COLUMN = 4


def _rounds(one_round, init, rolled):
    if rolled:
        return lax.fori_loop(0, PEER_TOPK, one_round, init)
    carry = init
    for k in range(PEER_TOPK):
        carry = one_round(k, carry)
    return carry


def _top16_keys(problems, key_pos_ref, rolled):
    n_cols = N_KEYS // SUBLANES // KEY_COLUMN
    for val_ref, pos_ref in problems:
        for q in range(n_cols):
            lv = [_rows(val_ref, q * KEY_COLUMN + l) for l in range(KEY_COLUMN)]
            lp = [_rows(key_pos_ref, q * KEY_COLUMN + l) for l in range(KEY_COLUMN)]
            for span in range(KEY_COLUMN - 1, 0, -1):
                for i in range(span):
                    swap = lv[i + 1] > lv[i]
                    lv[i], lv[i + 1] = jnp.maximum(lv[i], lv[i + 1]), jnp.minimum(lv[i], lv[i + 1])
                    lp[i], lp[i + 1] = jnp.where(swap, lp[i + 1], lp[i]), jnp.where(swap, lp[i], lp[i + 1])
            for l in range(KEY_COLUMN):
                val_ref[(q * KEY_COLUMN + l) * SUBLANES:(q * KEY_COLUMN + l + 1) * SUBLANES, :] = lv[l]
                pos_ref[(q * KEY_COLUMN + l) * SUBLANES:(q * KEY_COLUMN + l + 1) * SUBLANES, :] = lp[l]

    def one_round(k, found):
        out = []
        for (val_ref, pos_ref), best in zip(problems, found):
            heads = [(_rows(val_ref, q * KEY_COLUMN), _rows(pos_ref, q * KEY_COLUMN)) for q in range(n_cols)]
            top = heads
            while len(top) > 1:
                top = [_better_left(top[i], top[i + 1]) for i in range(0, len(top), 2)]
            m, am = _best_sublane(*top[0], ordered=True)
            for q in range(n_cols):
                popped = heads[q][1] == am
                for l in range(KEY_COLUMN):
                    g = q * KEY_COLUMN + l
                    rows = slice(g * SUBLANES, (g + 1) * SUBLANES)
                    if l + 1 < KEY_COLUMN:
                        val_ref[rows, :] = jnp.where(popped, _rows(val_ref, g + 1), _rows(val_ref, g))
                        pos_ref[rows, :] = jnp.where(popped, _rows(pos_ref, g + 1), _rows(pos_ref, g))
                    else:
                        val_ref[rows, :] = jnp.where(popped, -jnp.inf, _rows(val_ref, g))
            out.append(_record(best, k, m, am))
        return tuple(out)

    blank = jnp.zeros((PEER_TOPK, key_pos_ref.shape[1]), F32)
    return _rounds(one_round, tuple((blank, blank) for _ in problems), rolled)


def _top16_pairs(sv0, sv1, cand_ref, extra_ref, rolled):
    c = sv0.shape[1]
    sub = lax.broadcasted_iota(I32, (SUBLANES, c), 0)
    for a in range(PEER_TOPK):
        cand_ref[a * SUBLANES:(a + 1) * SUBLANES, :] = jnp.where(
            sub < PEER_TOPK // (a + 1), sv0[a:a + 1] + sv1[:SUBLANES], -jnp.inf)
    extra_ref[...] = sv0[0:1] + sv1[SUBLANES:]
    sub_f = sub.astype(F32)
    extra_pos = sub_f + SUBLANES

    def one_round(k, carry):
        best, popped_count = carry
        head_pos = popped_count * PEER_TOPK + sub_f
        top = (_rows(cand_ref, 0), head_pos), (extra_ref[...], extra_pos)
        (vl, pl_), (vr, pr) = top
        take = jnp.logical_or(vl > vr, jnp.logical_and(vl == vr, pl_ < pr))
        m, am = _best_sublane(jnp.where(take, vl, vr), jnp.where(take, pl_, pr), ordered=False)
        popped = head_pos == am
        for a in range(PEER_TOPK):
            below = _rows(cand_ref, a + 1) if a + 1 < PEER_TOPK else -jnp.inf
            cand_ref[a * SUBLANES:(a + 1) * SUBLANES, :] = jnp.where(popped, below, _rows(cand_ref, a))
        extra_ref[...] = jnp.where(extra_pos == am, -jnp.inf, extra_ref[...])
        return _record(best, k, m, am), popped_count + jnp.where(popped, 1.0, 0.0)

    blank = jnp.zeros((PEER_TOPK, c), F32)
    (fv, fpos), _ = _rounds(one_round, ((blank, blank), jnp.zeros((SUBLANES, c), F32)), rolled)
    return fv, fpos


def _select_row(table, sel, pos16):
    out = []
    for k in range(PEER_TOPK):
        out.append(jnp.sum(jnp.where(pos16 == sel[k:k + 1], table, 0), axis=0, keepdims=True))
    return jnp.concatenate(out, axis=0)


def _route_positions(c):
    s = np.arange(SUBLANES)
    key_pos = np.concatenate([np.array(SUBLANE_BITREV)[s] * (N_KEYS // SUBLANES) + g
                              for g in range(N_KEYS // SUBLANES)])
    return jnp.asarray(np.broadcast_to(key_pos[:, None], (N_KEYS, c)), F32)


def _route_scratch(c):
    return [pltpu.VMEM((2, N_KEYS, c), F32), pltpu.VMEM((2, N_KEYS, c), F32),
            pltpu.VMEM((PEER_TOPK * SUBLANES, c), F32), pltpu.VMEM((SUBLANES, c), F32)]


def _route_head(xt, wq_ref, keys_ref, key_pos_ref, h, idx_scr, gate_scr, scratch, tokens, rolled):
    score_scr, pos_scr, cand_scr, extra_scr = scratch
    c = xt.shape[1]
    pos16 = lax.broadcasted_iota(I32, (PEER_TOPK, c), 0)
    wq = wq_ref[pl.ds(pl.multiple_of(h * 2 * D_HALF, 2 * D_HALF), 2 * D_HALF), :]
    q_t = jnp.dot(wq, xt, preferred_element_type=F32).astype(BF16)
    for half in range(2):
        score_scr[half] = jnp.dot(keys_ref[h * 2 + half], q_t[half * D_HALF:(half + 1) * D_HALF],
                                  preferred_element_type=F32)
    sv, si = [], []
    for v, p in _top16_keys([(score_scr.at[half], pos_scr.at[half]) for half in range(2)], key_pos_ref, rolled):
        sv.append(v)
        si.append(p.astype(I32))
    fv, fpos = _top16_pairs(sv[0], sv[1], cand_scr, extra_scr, rolled)
    fpos = fpos.astype(I32)
    i0 = _select_row(si[0], fpos >> 4, pos16)
    i1 = _select_row(si[1], fpos & (PEER_TOPK - 1), pos16)
    e = jnp.exp(fv - fv[0:1])
    rows = pl.ds(pl.multiple_of(h * PEER_TOPK, PEER_TOPK), PEER_TOPK)
    idx_scr[rows, tokens] = i0 * N_KEYS + i1
    gate_scr[rows, tokens] = e / jnp.sum(e, axis=0, keepdims=True)


def _key_rows():
    key = lax.broadcasted_iota(I32, (N_KEYS, PEER_SEL), 0)
    return key, key.astype(F32).astype(BF16)


def _split_keys(idx_rows):
    first, second = idx_rows >> 7, idx_rows & (N_KEYS - 1)
    as_bf16 = lambda v: v.astype(F32).astype(BF16)
    return first, second, as_bf16(first), as_bf16(second)


def _route_act_kernel(xb_ref, xt_ref, ut_ref, wq_ref, keys_ref, key_pos_ref, idx_ref, gate_ref, a_ref,
                      grid_ref, idx_scr, gate_scr, *route_scratch):
    k = pl.program_id(1)
    c = xb_ref.shape[0]

    chunk_tokens = pl.ds(pl.multiple_of((k // PEER_HEADS) * TC_ROUTE, TC_ROUTE), TC_ROUTE)
    _route_head(xt_ref[:, chunk_tokens], wq_ref, keys_ref, key_pos_ref, k % PEER_HEADS,
                idx_scr, gate_scr, route_scratch, chunk_tokens, rolled=False)

    part = c // ROUTE_SPLIT
    part_tokens = pl.ds(pl.multiple_of((k % ROUTE_SPLIT) * part, part), part)
    dense = jnp.dot(xb_ref[part_tokens, :], ut_ref[...], preferred_element_type=F32)
    by_key = jnp.stack([dense[:, kk * N_KEYS:(kk + 1) * N_KEYS] for kk in range(KEYS_PER_BLOCK)], axis=0)
    second_rows = pl.ds(pl.multiple_of((k // ROUTE_SPLIT) * KEYS_PER_BLOCK, KEYS_PER_BLOCK), KEYS_PER_BLOCK)
    grid_ref[part_tokens, second_rows, :] = jnp.swapaxes(by_key, 0, 1)

    @pl.when(k == pl.num_programs(1) - 1)
    def _():
        idx_ref[...] = idx_scr[...].T
        gate_ref[...] = gate_scr[...].T
        key, key_bf = _key_rows()
        one, zero = jnp.ones((), BF16), jnp.zeros((), BF16)

        def group(gi, carry):
            t0 = pl.multiple_of(gi * TOKENS_PER_TRIP, TOKENS_PER_TRIP)
            _, second, first_bf, _ = _split_keys(idx_ref[pl.ds(t0, TOKENS_PER_TRIP), :])
            rows = []
            for g in range(TOKENS_PER_TRIP):
                pick_first = jnp.where(key_bf == first_bf[g:g + 1], one, zero)
                picked = jnp.dot(grid_ref[t0 + g].astype(BF16), pick_first,
                                 preferred_element_type=F32)
                rows.append(jnp.sum(jnp.where(key == second[g:g + 1], picked, 0.0), axis=0, keepdims=True))
            a_ref[pl.ds(t0, TOKENS_PER_TRIP), :] = jnp.concatenate(rows, axis=0)
            return carry

        lax.fori_loop(0, c // TOKENS_PER_TRIP, group, 0)


def _route_act(xb, xt, u_t, wq_t, keys, t):
    c = TC_EXPERT
    steps = ROUTE_SPLIT * N_EXPERTS // EXPERT_BLOCK
    assert steps == PEER_HEADS * (c // TC_ROUTE)
    sel = pl.BlockSpec((c, PEER_SEL), lambda i, k: (i, 0))
    once = pl.Buffered(1)
    key_pos = _route_positions(TC_ROUTE)
    return pl.pallas_call(
        _route_act_kernel,
        grid=(t // c, steps),
        in_specs=[
            pl.BlockSpec((c, D_MODEL), lambda i, k: (i, 0)),
            pl.BlockSpec((D_MODEL, c), lambda i, k: (0, i)),
            pl.BlockSpec((D_MODEL, EXPERT_BLOCK), lambda i, k: (0, k // ROUTE_SPLIT)),
            pl.BlockSpec((2 * PEER_HEADS * D_HALF, D_MODEL), lambda i, k: (0, 0), pipeline_mode=once),
            pl.BlockSpec((2 * PEER_HEADS, N_KEYS, D_HALF), lambda i, k: (0, 0, 0), pipeline_mode=once),
            pl.BlockSpec(key_pos.shape, lambda i, k: (0, 0), pipeline_mode=once),
        ],
        out_specs=[sel, sel, sel],
        out_shape=[jax.ShapeDtypeStruct((t, PEER_SEL), I32), jax.ShapeDtypeStruct((t, PEER_SEL), F32),
                   jax.ShapeDtypeStruct((t, PEER_SEL), F32)],
        scratch_shapes=[pltpu.VMEM((c, N_KEYS, N_KEYS), F32),
                        pltpu.VMEM((PEER_SEL, c), I32), pltpu.VMEM((PEER_SEL, c), F32)] + _route_scratch(TC_ROUTE),
        compiler_params=_tc_params("parallel", "arbitrary"),
        name="route_act",
    )(xb, xt, u_t, wq_t, keys, key_pos)


def _gelu(a):
    return 0.5 * a * (1.0 + lax.erf(a * math.sqrt(0.5)))


def _expert_mix_kernel(normalize, idx_ref, a_ref, gate_ref, v_ref, h_ref, g_ref, o_ref, grid_ref, w_ref, acc_ref):
    k = pl.program_id(1)
    c = idx_ref.shape[0]

    @pl.when(k == 0)
    def _():
        acc_ref[...] = h_ref[...]
        w_ref[...] = gate_ref[...] * _gelu(a_ref[...])

        _, key_bf = _key_rows()
        one, zero = jnp.ones((), BF16), jnp.zeros((), BF16)

        def group(gi, carry):
            t0 = pl.multiple_of(gi * TOKENS_PER_TRIP, TOKENS_PER_TRIP)
            _, _, first_bf, second_bf = _split_keys(idx_ref[pl.ds(t0, TOKENS_PER_TRIP), :])
            w_bf = w_ref[pl.ds(t0, TOKENS_PER_TRIP), :].astype(BF16)
            for g0 in range(0, TOKENS_PER_TRIP, SUBLANES):
                mats = []
                for g in range(g0, g0 + SUBLANES):
                    weighted = jnp.where(key_bf == first_bf[g:g + 1], w_bf[g:g + 1], zero)
                    pick_second = jnp.where(key_bf == second_bf[g:g + 1], one, zero)
                    mats.append(lax.dot_general(weighted, pick_second, (((1,), (1,)), ((), ())),
                                                preferred_element_type=F32))
                grid_ref[:, pl.ds(t0 + g0, SUBLANES), :] = jnp.swapaxes(jnp.stack(mats, axis=0), 0, 1)
            return carry

        lax.fori_loop(0, c // TOKENS_PER_TRIP, group, 0)

    dense = jnp.concatenate([grid_ref[k * KEYS_PER_BLOCK + kk] for kk in range(KEYS_PER_BLOCK)], axis=1)
    acc_ref[...] += jnp.dot(dense.astype(BF16), v_ref[...], preferred_element_type=F32)

    @pl.when(k == pl.num_programs(1) - 1)
    def _():
        hh = acc_ref[...]
        o_ref[...] = hh * _rms_scale(hh) * g_ref[...] if normalize else hh


def _expert_mix(idx, a, gate, v_tab, h2, g, normalize):
    t = idx.shape[0]
    c = TC_EXPERT
    sel = pl.BlockSpec((c, PEER_SEL), lambda i, k: (i, 0))
    tok = pl.BlockSpec((c, D_MODEL), lambda i, k: (i, 0))
    return pl.pallas_call(
        functools.partial(_expert_mix_kernel, normalize),
        grid=(t // c, N_EXPERTS // EXPERT_BLOCK),
        in_specs=[sel, sel, sel, pl.BlockSpec((EXPERT_BLOCK, D_MODEL), lambda i, k: (k, 0)), tok,
                  pl.BlockSpec((1, D_MODEL), lambda i, k: (0, 0))],
        out_specs=tok,
        out_shape=jax.ShapeDtypeStruct(h2.shape, F32),
        scratch_shapes=[pltpu.VMEM((N_KEYS, c, N_KEYS), F32), pltpu.VMEM((c, PEER_SEL), F32),
                        pltpu.VMEM((c, D_MODEL), F32)],
        compiler_params=_tc_params("parallel", "arbitrary"),
        name="expert_mix",
    )(idx, a, gate, v_tab, h2, g)


SC_CORES = 2
SC_SUBCORES = 16
SC_WORKERS = SC_CORES * SC_SUBCORES
SC_LANES = 16
SC_GATHER_ROWS = 16
SC_GATHERS_PER_TOKEN = PEER_SEL // SC_GATHER_ROWS
SC_TOKENS = 8
SC_CHUNKS = D_MODEL // SC_LANES
SC_SHARE = 4
TM_ELEM = 1024

_SC_PARAMS = pltpu.CompilerParams(needs_layout_passes=False)


def _route_kernel(xt_ref, wq_ref, keys_ref, key_pos_ref, idx_ref, gate_ref, idx_scr, gate_scr, *route_scratch):
    everything = pl.ds(0, xt_ref.shape[1])

    def head_body(h, carry):
        _route_head(xt_ref[...], wq_ref, keys_ref, key_pos_ref, h, idx_scr, gate_scr, route_scratch, everything,
                    rolled=True)
        return carry

    lax.fori_loop(0, PEER_HEADS, head_body, 0)
    idx_ref[...] = idx_scr[...].T
    gate_ref[...] = gate_scr[...].T


def _route(xt, wq_t, keys, first):
    t = xt.shape[1] - first
    c = TC_ROUTE
    sel = pl.BlockSpec((c, PEER_SEL), lambda i: (i, 0))
    key_pos = _route_positions(c)
    return pl.pallas_call(
        _route_kernel,
        grid=(t // c,),
        in_specs=[
            pl.BlockSpec((D_MODEL, c), lambda i: (0, i + first // c)),
            pl.BlockSpec((2 * PEER_HEADS * D_HALF, D_MODEL), lambda i: (0, 0)),
            pl.BlockSpec((2 * PEER_HEADS, N_KEYS, D_HALF), lambda i: (0, 0, 0)),
            pl.BlockSpec(key_pos.shape, lambda i: (0, 0)),
        ],
        out_specs=[sel, sel],
        out_shape=[jax.ShapeDtypeStruct((t, PEER_SEL), I32), jax.ShapeDtypeStruct((t, PEER_SEL), F32)],
        scratch_shapes=[pltpu.VMEM((PEER_SEL, c), I32), pltpu.VMEM((PEER_SEL, c), F32)] + _route_scratch(c),
        compiler_params=_tc_params("parallel"),
        name="route",
    )(xt, wq_t, keys, key_pos)


def _expert_weight_kernel(a_ref, g_ref, after_ref, w_ref):
    del after_ref
    w_ref[...] = g_ref[...] * _gelu(a_ref[...])


def _expert_weight(a, g, after):
    t = a.shape[0]
    spec = pl.BlockSpec((TM_ELEM, PEER_SEL), lambda i: (i, 0))
    return pl.pallas_call(
        _expert_weight_kernel,
        grid=(t // TM_ELEM,),
        in_specs=[spec, spec, pl.BlockSpec(memory_space=pl.ANY)],
        out_specs=spec,
        out_shape=jax.ShapeDtypeStruct((t, PEER_SEL), F32),
        compiler_params=_tc_params("parallel"),
        name="expert_weight",
    )(a, g, after)


def _residual_kernel(normalize, out_in_ref, h_ref, y_ref, g_ref, o_ref):
    del out_in_ref
    h = h_ref[...] + y_ref[...]
    o_ref[...] = h * _rms_scale(h) * g_ref[...] if normalize else h


def _residual(out, h2, y2, g, normalize):
    first = h2.shape[0] - y2.shape[0]
    tail = pl.BlockSpec((TM_ELEM, D_MODEL), lambda i: (i + first // TM_ELEM, 0))
    return pl.pallas_call(
        functools.partial(_residual_kernel, normalize),
        grid=(y2.shape[0] // TM_ELEM,),
        in_specs=[pl.BlockSpec(memory_space=pl.ANY), tail, pl.BlockSpec((TM_ELEM, D_MODEL), lambda i: (i, 0)),
                  pl.BlockSpec((1, D_MODEL), lambda i: (0, 0))],
        out_specs=tail,
        out_shape=jax.ShapeDtypeStruct(out.shape, F32),
        input_output_aliases={0: 0},
        compiler_params=_tc_params("parallel"),
        name="residual",
    )(out, h2, y2, g)


def _sc_mesh():
    return plsc.VectorSubcoreMesh(core_axis_name="c", subcore_axis_name="s")


def _sc_worker_id():
    return lax.axis_index("s") * SC_CORES + lax.axis_index("c")


def _sc_gather_pipeline(tab_hbm, idx_v, rows_v, sems, compute):
    n_gathers = idx_v.shape[0]

    def gather(n, slot):
        return pltpu.make_async_copy(tab_hbm.at[idx_v[n, :]], rows_v.at[slot], sems.at[slot])

    gather(0, 0).start()

    def step(n2, carry):
        for slot in range(2):
            n = n2 * 2 + slot

            @pl.when(n + 1 < n_gathers)
            def _():
                gather(n + 1, 1 - slot).start()

            gather(n, slot).wait()
            compute(n, slot)
        return carry

    lax.fori_loop(0, n_gathers // 2, step, 0)


def _expert_dots_sc(u_tab, idx, xn2):
    t = idx.shape[0]
    first = xn2.shape[0] - t
    tok_per_worker = t // SC_WORKERS
    idx2 = idx.reshape(t * SC_GATHERS_PER_TOKEN, SC_GATHER_ROWS)

    def body(u_hbm, idx_hbm, x_hbm, a_hbm, idx_v, x_v, rows_v, a_v, sems):
        wid = _sc_worker_id()
        lanes = lax.iota(I32, SC_LANES)

        def compute(n, slot):
            tl = n // SC_GATHERS_PER_TOKEN
            g = n % SC_GATHERS_PER_TOKEN

            def chunk(c, accs):
                off = pl.multiple_of(c * SC_LANES, SC_LANES)
                xv = x_v[tl, pl.ds(off, SC_LANES)]
                return tuple(accs[r] + rows_v[slot, r, pl.ds(off, SC_LANES)] * xv for r in range(SC_GATHER_ROWS))

            accs = lax.fori_loop(0, SC_CHUNKS, chunk,
                                 tuple(jnp.zeros((SC_LANES,), F32) for _ in range(SC_GATHER_ROWS)))
            tot = jnp.zeros((SC_LANES,), F32)
            for r in range(SC_GATHER_ROWS):
                tot = jnp.where(lanes == r, jnp.sum(accs[r]), tot)
            a_v[tl, pl.ds(pl.multiple_of(g * SC_GATHER_ROWS, SC_LANES), SC_LANES)] = tot

        def block(bi, carry):
            tok0 = wid * tok_per_worker + bi * SC_TOKENS
            pltpu.sync_copy(idx_hbm.at[pl.ds(tok0 * SC_GATHERS_PER_TOKEN, SC_TOKENS * SC_GATHERS_PER_TOKEN)], idx_v)
            pltpu.sync_copy(x_hbm.at[pl.ds(first + tok0, SC_TOKENS)], x_v)
            _sc_gather_pipeline(u_hbm, idx_v, rows_v, sems, compute)
            pltpu.sync_copy(a_v, a_hbm.at[pl.ds(tok0, SC_TOKENS)])
            return carry

        lax.fori_loop(0, tok_per_worker // SC_TOKENS, block, 0)

    return pl.kernel(
        body,
        out_type=jax.ShapeDtypeStruct((t, PEER_SEL), F32),
        mesh=_sc_mesh(),
        scratch_types=[
            pltpu.VMEM((SC_TOKENS * SC_GATHERS_PER_TOKEN, SC_GATHER_ROWS), I32),
            pltpu.VMEM((SC_TOKENS, D_MODEL), F32),
            pltpu.VMEM((2, SC_GATHER_ROWS, D_MODEL), F32),
            pltpu.VMEM((SC_TOKENS, PEER_SEL), F32),
            pltpu.SemaphoreType.DMA((2,)),
        ],
        compiler_params=_SC_PARAMS,
        name="expert_dots_sc",
    )(u_tab, idx2, xn2)


def _expert_mix_sc(v_tab, idx, w):
    t = idx.shape[0]
    tok_per_worker = t // SC_WORKERS
    idx2 = idx.reshape(t * SC_GATHERS_PER_TOKEN, SC_GATHER_ROWS)

    def body(v_hbm, idx_hbm, w_hbm, y_hbm, idx_v, w_v, rows_v, y_v, sems):
        wid = _sc_worker_id()

        def compute(n, slot):
            tl = n // SC_GATHERS_PER_TOKEN
            g = n % SC_GATHERS_PER_TOKEN
            wvec = w_v[tl, pl.ds(pl.multiple_of(g * SC_GATHER_ROWS, SC_LANES), SC_LANES)]

            for half in range(2):
                base = half * (D_MODEL // 2)

                def row(r, accs):
                    wv = jnp.take_along_axis(wvec, jnp.full((SC_LANES,), r, I32), axis=0)
                    return tuple(accs[c] + wv * rows_v[slot, r, pl.ds(base + c * SC_LANES, SC_LANES)]
                                 for c in range(SC_CHUNKS // 2))

                accs = lax.fori_loop(0, SC_GATHER_ROWS, row,
                                     tuple(jnp.zeros((SC_LANES,), F32) for _ in range(SC_CHUNKS // 2)))
                for c in range(SC_CHUNKS // 2):
                    sl = pl.ds(base + c * SC_LANES, SC_LANES)
                    y_v[tl, sl] = y_v[tl, sl] + accs[c]

        def block(bi, carry):
            tok0 = wid * tok_per_worker + bi * SC_TOKENS
            pltpu.sync_copy(idx_hbm.at[pl.ds(tok0 * SC_GATHERS_PER_TOKEN, SC_TOKENS * SC_GATHERS_PER_TOKEN)], idx_v)
            pltpu.sync_copy(w_hbm.at[pl.ds(tok0, SC_TOKENS)], w_v)
            zero = jnp.zeros((SC_LANES,), F32)
            for tl in range(SC_TOKENS):
                for c in range(SC_CHUNKS):
                    y_v[tl, pl.ds(c * SC_LANES, SC_LANES)] = zero
            _sc_gather_pipeline(v_hbm, idx_v, rows_v, sems, compute)
            pltpu.sync_copy(y_v, y_hbm.at[pl.ds(tok0, SC_TOKENS)])
            return carry

        lax.fori_loop(0, tok_per_worker // SC_TOKENS, block, 0)

    return pl.kernel(
        body,
        out_type=jax.ShapeDtypeStruct((t, D_MODEL), F32),
        mesh=_sc_mesh(),
        scratch_types=[
            pltpu.VMEM((SC_TOKENS * SC_GATHERS_PER_TOKEN, SC_GATHER_ROWS), I32),
            pltpu.VMEM((SC_TOKENS, PEER_SEL), F32),
            pltpu.VMEM((2, SC_GATHER_ROWS, D_MODEL), F32),
            pltpu.VMEM((SC_TOKENS, D_MODEL), F32),
            pltpu.SemaphoreType.DMA((2,)),
        ],
        compiler_params=_SC_PARAMS,
        name="expert_mix_sc",
    )(v_tab, idx2, w)


def _rearranged_in_proj(w_in, b_in):
    def cols(a):
        q = a[..., :Q_W]
        k = a[..., Q_W:Q_W + N_KV_HEADS * HEAD_DIM]
        v = a[..., Q_W + N_KV_HEADS * HEAD_DIM:Q_W + 2 * N_KV_HEADS * HEAD_DIM]
        rest = a[..., Q_W + 2 * N_KV_HEADS * HEAD_DIM:]
        dup = lambda m: jnp.concatenate(
            [m[..., hd * HEAD_DIM:(hd + 1) * HEAD_DIM] for hd in range(N_KV_HEADS) for _ in range(2)], axis=-1)
        return jnp.concatenate([q, dup(k), dup(v), rest], axis=-1)
    return cols(w_in).astype(BF16), cols(b_in)[None, :]


def kernel(x, ln_mix_g, w_in, b_in, attn_sinks, w_attn_up, w_pool_grp, pool_scale, w_pool_up, w_o,
           ln_ffn_g, w_query, sub_keys, u_experts, v_experts, ln_final_g):
    b, s, d = x.shape
    t = b * s
    depth = w_in.shape[0]
    h = x
    for l in range(depth):
        w_r, b_r = _rearranged_in_proj(w_in[l], b_in[l])
        q, kv, pz, gate = _inproj(h.reshape(t, d), ln_mix_g[l][None, :], w_r, b_r)
        o = _attention(attn_sinks[l], q.reshape(b, s, Q_W), kv.reshape(b, s, KV_COLS))
        h, xn2, xb, xt = _merge(o, gate.reshape(b, s, 2 * d), h, pz.reshape(b, s, POOL_W),
                                w_attn_up[l].astype(BF16), w_pool_grp[l].astype(BF16), pool_scale[l][None, :],
                                w_pool_up[l].astype(BF16), w_o[l].astype(BF16), ln_ffn_g[l][None, :])
        xn2, xb = xn2.reshape(t, d), xb.reshape(t, d)
        wq_t = w_query[l].T.astype(BF16)
        keys = sub_keys[l].reshape(2 * PEER_HEADS, SUBLANES, N_KEYS // SUBLANES, D_HALF)
        keys = keys[:, jnp.array(SUBLANE_BITREV)].transpose(0, 2, 1, 3)
        keys = keys.reshape(2 * PEER_HEADS, N_KEYS, D_HALF).astype(BF16)
        u_t = u_experts[l].astype(BF16).reshape(N_KEYS, N_KEYS, d).transpose(2, 1, 0).reshape(d, N_EXPERTS)
        last = l + 1 == depth
        g_out = ln_final_g[None, :]
        h2 = h.reshape(t, d)
        t_tc = t - (t // TC_EXPERT) // SC_SHARE * TC_EXPERT
        if t_tc < t:
            idx_sc, g_sc = _route(xt, wq_t, keys, t_tc)
            a_sc = _expert_dots_sc(u_experts[l], idx_sc, xn2)
        idx, g, a = _route_act(xb, xt, u_t, wq_t, keys, t_tc)
        if t_tc < t:
            y_sc = _expert_mix_sc(v_experts[l], idx_sc, _expert_weight(a_sc, g_sc, a))
        h2 = _expert_mix(idx, a, g, v_experts[l].astype(BF16), h2, g_out, last)
        if t_tc < t:
            h2 = _residual(h2, h.reshape(t, d), y_sc, g_out, last)
        h = h2.reshape(b, s, d)
    return h
```

```python
import functools
import math

import jax
import jax.numpy as jnp
import numpy as np
from jax import lax
from jax.experimental import pallas as pl
from jax.experimental.pallas import tpu as pltpu
from jax.experimental.pallas import tpu_sc as plsc

F32 = jnp.float32
BF16 = jnp.bfloat16
I32 = jnp.int32

D_MODEL = 1024
N_HEADS = 16
N_KV_HEADS = 2
GROUP = N_HEADS // N_KV_HEADS
HEAD_DIM = 64
BLOCK = 128
ATTN_SCALE = 1.0 / math.sqrt(HEAD_DIM)
NEG_INF = -1e30
POOL_WINDOWS = (2, 4, 8, 16)
POOL_GROUP = 128
POOL_W = len(POOL_WINDOWS) * POOL_GROUP
POOL_HIST = max(POOL_WINDOWS)
Q_W = N_HEADS * HEAD_DIM
PEER_HEADS = 8
N_KEYS = 128
N_EXPERTS = N_KEYS * N_KEYS
D_HALF = 128
PEER_TOPK = 16
PEER_SEL = PEER_HEADS * PEER_TOPK
EPS = 1e-5

VMEM_LIMIT_BYTES = 56 * 1024 * 1024
SUBLANES = 8
LANES = 128
SUBLANE_BITREV = (0, 4, 2, 6, 1, 5, 3, 7)

C_Q = 0
C_K = C_Q + Q_W
C_V = C_K + 2 * BLOCK
C_P = C_V + 2 * BLOCK
C_G = C_P + POOL_W
C_END = C_G + 2 * D_MODEL
KV_COLS = C_P - C_K

TM_PROJ = 512
ATTN_Q_BLOCKS = 2
TS_MERGE = 512
ROUTE_SPLIT = 1
TC_ROUTE = 256
TC_EXPERT = 512
EXPERT_BLOCK = 1024
KEYS_PER_BLOCK = EXPERT_BLOCK // N_KEYS
TOKENS_PER_TRIP = 32


def _tc_params(*sem):
    return pltpu.CompilerParams(dimension_semantics=sem, vmem_limit_bytes=VMEM_LIMIT_BYTES)


def _rms_scale(x):
    return lax.rsqrt(jnp.mean(x * x, axis=-1, keepdims=True) + EPS)


def _inproj_kernel(x_ref, g_ref, w_ref, b_ref, q_ref, kv_ref, pz_ref, gate_ref):
    x = x_ref[...]
    xn = (x * _rms_scale(x) * g_ref[...]).astype(BF16)

    def proj(lo, hi):
        return jnp.dot(xn, w_ref[:, lo:hi], preferred_element_type=F32) + b_ref[:, lo:hi]

    q_ref[...] = proj(C_Q, C_K).astype(BF16)
    kv_ref[...] = proj(C_K, C_P).astype(BF16)
    pz_ref[...] = proj(C_P, C_G)
    gate_ref[...] = proj(C_G, C_END).astype(BF16)


def _inproj(x2, g, w, b):
    t = x2.shape[0]
    row = lambda i: (i, 0)
    fixed = lambda i: (0, 0)
    return pl.pallas_call(
        _inproj_kernel,
        grid=(t // TM_PROJ,),
        in_specs=[
            pl.BlockSpec((TM_PROJ, D_MODEL), row),
            pl.BlockSpec((1, D_MODEL), fixed),
            pl.BlockSpec((D_MODEL, C_END), fixed),
            pl.BlockSpec((1, C_END), fixed),
        ],
        out_specs=[
            pl.BlockSpec((TM_PROJ, Q_W), row),
            pl.BlockSpec((TM_PROJ, KV_COLS), row),
            pl.BlockSpec((TM_PROJ, POOL_W), row),
            pl.BlockSpec((TM_PROJ, 2 * D_MODEL), row),
        ],
        out_shape=[
            jax.ShapeDtypeStruct((t, Q_W), BF16),
            jax.ShapeDtypeStruct((t, KV_COLS), BF16),
            jax.ShapeDtypeStruct((t, POOL_W), F32),
            jax.ShapeDtypeStruct((t, 2 * D_MODEL), BF16),
        ],
        compiler_params=_tc_params("parallel"),
        name="inproj",
    )(x2, g, w, b)


def _attn_kernel(sink_ref, q_ref, kvc_ref, kvp_ref, o_ref, s_scr, p_scr):
    j = pl.program_id(1)
    row = lax.broadcasted_iota(I32, (BLOCK, BLOCK), 0)
    col = lax.broadcasted_iota(I32, (BLOCK, BLOCK), 1)
    low = lax.broadcasted_iota(I32, (1, BLOCK), 1) < HEAD_DIM
    zero = jnp.zeros((), BF16)
    for qb in range(ATTN_Q_BLOCKS):
        rows = slice(qb * BLOCK, (qb + 1) * BLOCK)
        kvc = kvc_ref[0, rows, :]
        kvp = kvp_ref[0] if qb == 0 else kvc_ref[0, (qb - 1) * BLOCK:qb * BLOCK, :]
        in_window = jnp.logical_and(col > row, j > 0) if qb == 0 else col > row
        mask = jnp.concatenate([in_window, col <= row], axis=1)
        for hkv in range(N_KV_HEADS):
            kc = slice(hkv * BLOCK, (hkv + 1) * BLOCK)
            k2 = jnp.concatenate([kvp[:, kc], kvc[:, kc]], axis=0)
            for pair in range(GROUP // 2):
                c0 = (hkv * (GROUP // 2) + pair) * BLOCK
                q2 = q_ref[0, rows, c0:c0 + BLOCK]
                for half in range(2):
                    qm = jnp.where(low, q2, zero) if half == 0 else jnp.where(low, zero, q2)
                    s = lax.dot_general(qm, k2, (((1,), (1,)), ((), ())), preferred_element_type=F32)
                    s_scr[hkv * GROUP + pair * 2 + half] = jnp.where(mask, s * ATTN_SCALE, NEG_INF)
        recip = []
        for head in range(N_HEADS):
            s = s_scr[head]
            sink = sink_ref[head]
            m = jnp.maximum(jnp.max(s, axis=-1, keepdims=True), sink)
            e = jnp.exp(s - m)
            p_scr[head] = e.astype(BF16)
            recip.append(1.0 / (jnp.sum(e, axis=-1, keepdims=True) + jnp.exp(sink - m)))
        for hkv in range(N_KV_HEADS):
            vc = slice(2 * BLOCK + hkv * BLOCK, 2 * BLOCK + (hkv + 1) * BLOCK)
            v2 = jnp.concatenate([kvp[:, vc], kvc[:, vc]], axis=0)
            v_half = (jnp.where(low, v2, zero), jnp.where(low, zero, v2))
            for pair in range(GROUP // 2):
                c0 = (hkv * (GROUP // 2) + pair) * BLOCK
                head = hkv * GROUP + pair * 2
                o2 = (jnp.dot(p_scr[head], v_half[0], preferred_element_type=F32)
                      + jnp.dot(p_scr[head + 1], v_half[1], preferred_element_type=F32))
                o2 = o2 * jnp.where(low, recip[head], recip[head + 1])
                o_ref[0, rows, c0:c0 + BLOCK] = o2.astype(BF16)


def _attention(sinks, q3, kv3):
    b, s, _ = q3.shape
    tq = ATTN_Q_BLOCKS * BLOCK
    return pl.pallas_call(
        _attn_kernel,
        grid=(b, s // tq),
        in_specs=[
            pl.BlockSpec(memory_space=pltpu.SMEM),
            pl.BlockSpec((1, tq, Q_W), lambda i, j: (i, j, 0)),
            pl.BlockSpec((1, tq, KV_COLS), lambda i, j: (i, j, 0)),
            pl.BlockSpec((1, BLOCK, KV_COLS), lambda i, j: (i, jnp.maximum(j * ATTN_Q_BLOCKS - 1, 0), 0)),
        ],
        out_specs=pl.BlockSpec((1, tq, Q_W), lambda i, j: (i, j, 0)),
        out_shape=jax.ShapeDtypeStruct((b, s, Q_W), BF16),
        scratch_shapes=[pltpu.VMEM((N_HEADS, BLOCK, 2 * BLOCK), F32), pltpu.VMEM((N_HEADS, BLOCK, 2 * BLOCK), BF16)],
        compiler_params=_tc_params("parallel", "arbitrary"),
        name="attention",
    )(sinks, q3, kv3, kv3)


def _sigmoid(x):
    return 1.0 / (1.0 + jnp.exp(-x))


def _merge_kernel(o_ref, gate_ref, x_ref, pz_ref, pzp_ref, wau_ref, wgrp_ref, scale_ref, wpu_ref,
                  wo_ref, g_ref, h_ref, xn_ref, xb_ref, xt_ref):
    j = pl.program_id(1)
    ts = pz_ref.shape[1]
    prev = jnp.where(j > 0, pzp_ref[0], 0.0)
    ext = jnp.concatenate([prev, pz_ref[0]], axis=0)
    t1 = (j * ts + 1 + lax.broadcasted_iota(I32, (ts, 1), 0)).astype(F32)
    ys = []
    for g, w in enumerate(POOL_WINDOWS):
        e = ext[:, g * POOL_GROUP:(g + 1) * POOL_GROUP]
        tsum = e
        span = 1
        while span < w:
            tsum = tsum + pltpu.roll(tsum, span, 0)
            span *= 2
        pooled = tsum[POOL_HIST:] / jnp.minimum(t1, float(w)) - e[POOL_HIST:]
        y = jnp.dot(pooled.astype(BF16), wgrp_ref[g], preferred_element_type=F32)
        ys.append((y * scale_ref[:, g * POOL_GROUP:(g + 1) * POOL_GROUP]).astype(BF16))
    y_p = jnp.dot(jnp.concatenate(ys, axis=1), wpu_ref[...], preferred_element_type=F32)
    y_a = jnp.dot(o_ref[0], wau_ref[...], preferred_element_type=F32)
    gate = gate_ref[0]
    merged = (_sigmoid(gate[:, :D_MODEL].astype(F32)) * y_a
              + _sigmoid(gate[:, D_MODEL:].astype(F32)) * y_p)
    h = x_ref[0] + jnp.dot(merged.astype(BF16), wo_ref[...], preferred_element_type=F32)
    h_ref[0] = h
    xn = h * _rms_scale(h) * g_ref[...]
    xn_ref[0] = xn
    xb_ref[0] = xn.astype(BF16)
    xt_ref[...] = xn.T.astype(BF16)


def _merge(o3, gate3, x, pz3, wau, wgrp, scale, wpu, wo, g):
    b, s, _ = x.shape
    ts = TS_MERGE
    hist_blocks = ts // POOL_HIST
    tile = lambda i, j: (i, j, 0)
    fixed2 = lambda i, j: (0, 0)
    return pl.pallas_call(
        _merge_kernel,
        grid=(b, s // ts),
        in_specs=[
            pl.BlockSpec((1, ts, Q_W), tile),
            pl.BlockSpec((1, ts, 2 * D_MODEL), tile),
            pl.BlockSpec((1, ts, D_MODEL), tile),
            pl.BlockSpec((1, ts, POOL_W), tile),
            pl.BlockSpec((1, POOL_HIST, POOL_W), lambda i, j: (i, jnp.maximum(j * hist_blocks - 1, 0), 0)),
            pl.BlockSpec((Q_W, D_MODEL), fixed2),
            pl.BlockSpec((len(POOL_WINDOWS), POOL_GROUP, POOL_GROUP), lambda i, j: (0, 0, 0)),
            pl.BlockSpec((1, POOL_W), fixed2),
            pl.BlockSpec((POOL_W, D_MODEL), fixed2),
            pl.BlockSpec((D_MODEL, D_MODEL), fixed2),
            pl.BlockSpec((1, D_MODEL), fixed2),
        ],
        out_specs=[pl.BlockSpec((1, ts, D_MODEL), tile), pl.BlockSpec((1, ts, D_MODEL), tile),
                   pl.BlockSpec((1, ts, D_MODEL), tile),
                   pl.BlockSpec((D_MODEL, ts), lambda i, j: (0, i * (s // ts) + j))],
        out_shape=[jax.ShapeDtypeStruct((b, s, D_MODEL), F32), jax.ShapeDtypeStruct((b, s, D_MODEL), F32),
                   jax.ShapeDtypeStruct((b, s, D_MODEL), BF16), jax.ShapeDtypeStruct((D_MODEL, b * s), BF16)],
        compiler_params=_tc_params("parallel", "arbitrary"),
        name="merge",
    )(o3, gate3, x, pz3, pz3, wau, wgrp, scale, wpu, wo, g)


def _rows(ref, g):
    return ref[g * SUBLANES:(g + 1) * SUBLANES, :]


def _better_left(left, right):
    (vl, pl_), (vr, pr) = left, right
    return jnp.maximum(vl, vr), jnp.where(vl >= vr, pl_, pr)


def _best_sublane(v, p, ordered):
    rows = SUBLANES
    while rows > 1:
        rows //= 2
        vl, vr, pl_, pr = v[:rows], v[rows:2 * rows], p[:rows], p[rows:2 * rows]
        take = (vl >= vr) if ordered else jnp.logical_or(vl > vr, jnp.logical_and(vl == vr, pl_ < pr))
        v, p = jnp.where(take, vl, vr), jnp.where(take, pl_, pr)
    return v, p


def _record(found, k, m, am):
    vals, poss = found
    rank = lax.broadcasted_iota(I32, vals.shape, 0)
    return jnp.where(rank == k, m, vals), jnp.where(rank == k, am, poss)


KEY_COLUMN = 4


def _rounds(one_round, init, rolled):
    if rolled:
        return lax.fori_loop(0, PEER_TOPK, one_round, init)
    carry = init
    for k in range(PEER_TOPK):
        carry = one_round(k, carry)
    return carry


def _top16_keys(problems, key_pos_ref, rolled):
    n_cols = N_KEYS // SUBLANES // KEY_COLUMN
    for val_ref, pos_ref in problems:
        for q in range(n_cols):
            lv = [_rows(val_ref, q * KEY_COLUMN + l) for l in range(KEY_COLUMN)]
            lp = [_rows(key_pos_ref, q * KEY_COLUMN + l) for l in range(KEY_COLUMN)]
            for span in range(KEY_COLUMN - 1, 0, -1):
                for i in range(span):
                    swap = lv[i + 1] > lv[i]
                    lv[i], lv[i + 1] = jnp.maximum(lv[i], lv[i + 1]), jnp.minimum(lv[i], lv[i + 1])
                    lp[i], lp[i + 1] = jnp.where(swap, lp[i + 1], lp[i]), jnp.where(swap, lp[i], lp[i + 1])
            for l in range(KEY_COLUMN):
                val_ref[(q * KEY_COLUMN + l) * SUBLANES:(q * KEY_COLUMN + l + 1) * SUBLANES, :] = lv[l]
                pos_ref[(q * KEY_COLUMN + l) * SUBLANES:(q * KEY_COLUMN + l + 1) * SUBLANES, :] = lp[l]

    def one_round(k, found):
        out = []
        for (val_ref, pos_ref), best in zip(problems, found):
            heads = [(_rows(val_ref, q * KEY_COLUMN), _rows(pos_ref, q * KEY_COLUMN)) for q in range(n_cols)]
            top = heads
            while len(top) > 1:
                top = [_better_left(top[i], top[i + 1]) for i in range(0, len(top), 2)]
            m, am = _best_sublane(*top[0], ordered=True)
            for q in range(n_cols):
                popped = heads[q][1] == am
                for l in range(KEY_COLUMN):
                    g = q * KEY_COLUMN + l
                    rows = slice(g * SUBLANES, (g + 1) * SUBLANES)
                    if l + 1 < KEY_COLUMN:
                        val_ref[rows, :] = jnp.where(popped, _rows(val_ref, g + 1), _rows(val_ref, g))
                        pos_ref[rows, :] = jnp.where(popped, _rows(pos_ref, g + 1), _rows(pos_ref, g))
                    else:
                        val_ref[rows, :] = jnp.where(popped, -jnp.inf, _rows(val_ref, g))
            out.append(_record(best, k, m, am))
        return tuple(out)

    blank = jnp.zeros((PEER_TOPK, key_pos_ref.shape[1]), F32)
    return _rounds(one_round, tuple((blank, blank) for _ in problems), rolled)


def _top16_pairs(sv0, sv1, cand_ref, extra_ref, rolled):
    c = sv0.shape[1]
    sub = lax.broadcasted_iota(I32, (SUBLANES, c), 0)
    for a in range(PEER_TOPK):
        cand_ref[a * SUBLANES:(a + 1) * SUBLANES, :] = jnp.where(
            sub < PEER_TOPK // (a + 1), sv0[a:a + 1] + sv1[:SUBLANES], -jnp.inf)
    extra_ref[...] = sv0[0:1] + sv1[SUBLANES:]
    sub_f = sub.astype(F32)
    extra_pos = sub_f + SUBLANES

    def one_round(k, carry):
        best, popped_count = carry
        head_pos = popped_count * PEER_TOPK + sub_f
        top = (_rows(cand_ref, 0), head_pos), (extra_ref[...], extra_pos)
        (vl, pl_), (vr, pr) = top
        take = jnp.logical_or(vl > vr, jnp.logical_and(vl == vr, pl_ < pr))
        m, am = _best_sublane(jnp.where(take, vl, vr), jnp.where(take, pl_, pr), ordered=False)
        popped = head_pos == am
        for a in range(PEER_TOPK):
            below = _rows(cand_ref, a + 1) if a + 1 < PEER_TOPK else -jnp.inf
            cand_ref[a * SUBLANES:(a + 1) * SUBLANES, :] = jnp.where(popped, below, _rows(cand_ref, a))
        extra_ref[...] = jnp.where(extra_pos == am, -jnp.inf, extra_ref[...])
        return _record(best, k, m, am), popped_count + jnp.where(popped, 1.0, 0.0)

    blank = jnp.zeros((PEER_TOPK, c), F32)
    (fv, fpos), _ = _rounds(one_round, ((blank, blank), jnp.zeros((SUBLANES, c), F32)), rolled)
    return fv, fpos


def _select_row(table, sel, pos16):
    out = []
    for k in range(PEER_TOPK):
        out.append(jnp.sum(jnp.where(pos16 == sel[k:k + 1], table, 0), axis=0, keepdims=True))
    return jnp.concatenate(out, axis=0)


def _route_positions(c):
    s = np.arange(SUBLANES)
    key_pos = np.concatenate([np.array(SUBLANE_BITREV)[s] * (N_KEYS // SUBLANES) + g
                              for g in range(N_KEYS // SUBLANES)])
    return jnp.asarray(np.broadcast_to(key_pos[:, None], (N_KEYS, c)), F32)


def _route_scratch(c):
    return [pltpu.VMEM((2, N_KEYS, c), F32), pltpu.VMEM((2, N_KEYS, c), F32),
            pltpu.VMEM((PEER_TOPK * SUBLANES, c), F32), pltpu.VMEM((SUBLANES, c), F32)]


def _route_head(xt, wq_ref, keys_ref, key_pos_ref, h, idx_scr, gate_scr, scratch, tokens, rolled):
    score_scr, pos_scr, cand_scr, extra_scr = scratch
    c = xt.shape[1]
    pos16 = lax.broadcasted_iota(I32, (PEER_TOPK, c), 0)
    wq = wq_ref[pl.ds(pl.multiple_of(h * 2 * D_HALF, 2 * D_HALF), 2 * D_HALF), :]
    q_t = jnp.dot(wq, xt, preferred_element_type=F32).astype(BF16)
    for half in range(2):
        score_scr[half] = jnp.dot(keys_ref[h * 2 + half], q_t[half * D_HALF:(half + 1) * D_HALF],
                                  preferred_element_type=F32)
    sv, si = [], []
    for v, p in _top16_keys([(score_scr.at[half], pos_scr.at[half]) for half in range(2)], key_pos_ref, rolled):
        sv.append(v)
        si.append(p.astype(I32))
    fv, fpos = _top16_pairs(sv[0], sv[1], cand_scr, extra_scr, rolled)
    fpos = fpos.astype(I32)
    i0 = _select_row(si[0], fpos >> 4, pos16)
    i1 = _select_row(si[1], fpos & (PEER_TOPK - 1), pos16)
    e = jnp.exp(fv - fv[0:1])
    rows = pl.ds(pl.multiple_of(h * PEER_TOPK, PEER_TOPK), PEER_TOPK)
    idx_scr[rows, tokens] = i0 * N_KEYS + i1
    gate_scr[rows, tokens] = e / jnp.sum(e, axis=0, keepdims=True)


def _key_rows():
    key = lax.broadcasted_iota(I32, (N_KEYS, PEER_SEL), 0)
    return key, key.astype(F32).astype(BF16)


def _split_keys(idx_rows):
    first, second = idx_rows >> 7, idx_rows & (N_KEYS - 1)
    as_bf16 = lambda v: v.astype(F32).astype(BF16)
    return first, second, as_bf16(first), as_bf16(second)


def _route_act_kernel(xb_ref, xt_ref, ut_ref, wq_ref, keys_ref, key_pos_ref, idx_ref, gate_ref, a_ref,
                      grid_ref, idx_scr, gate_scr, *route_scratch):
    k = pl.program_id(1)
    c = xb_ref.shape[0]

    chunk_tokens = pl.ds(pl.multiple_of((k // PEER_HEADS) * TC_ROUTE, TC_ROUTE), TC_ROUTE)
    _route_head(xt_ref[:, chunk_tokens], wq_ref, keys_ref, key_pos_ref, k % PEER_HEADS,
                idx_scr, gate_scr, route_scratch, chunk_tokens, rolled=False)

    part = c // ROUTE_SPLIT
    part_tokens = pl.ds(pl.multiple_of((k % ROUTE_SPLIT) * part, part), part)
    dense = jnp.dot(xb_ref[part_tokens, :], ut_ref[...], preferred_element_type=F32)
    by_key = jnp.stack([dense[:, kk * N_KEYS:(kk + 1) * N_KEYS] for kk in range(KEYS_PER_BLOCK)], axis=0)
    second_rows = pl.ds(pl.multiple_of((k // ROUTE_SPLIT) * KEYS_PER_BLOCK, KEYS_PER_BLOCK), KEYS_PER_BLOCK)
    grid_ref[part_tokens, second_rows, :] = jnp.swapaxes(by_key, 0, 1)

    @pl.when(k == pl.num_programs(1) - 1)
    def _():
        idx_ref[...] = idx_scr[...].T
        gate_ref[...] = gate_scr[...].T
        key, key_bf = _key_rows()
        one, zero = jnp.ones((), BF16), jnp.zeros((), BF16)

        def group(gi, carry):
            t0 = pl.multiple_of(gi * TOKENS_PER_TRIP, TOKENS_PER_TRIP)
            _, second, first_bf, _ = _split_keys(idx_ref[pl.ds(t0, TOKENS_PER_TRIP), :])
            rows = []
            for g in range(TOKENS_PER_TRIP):
                pick_first = jnp.where(key_bf == first_bf[g:g + 1], one, zero)
                picked = jnp.dot(grid_ref[t0 + g].astype(BF16), pick_first,
                                 preferred_element_type=F32)
                rows.append(jnp.sum(jnp.where(key == second[g:g + 1], picked, 0.0), axis=0, keepdims=True))
            a_ref[pl.ds(t0, TOKENS_PER_TRIP), :] = jnp.concatenate(rows, axis=0)
            return carry

        lax.fori_loop(0, c // TOKENS_PER_TRIP, group, 0)


def _route_act(xb, xt, u_t, wq_t, keys, t):
    c = TC_EXPERT
    steps = ROUTE_SPLIT * N_EXPERTS // EXPERT_BLOCK
    assert steps == PEER_HEADS * (c // TC_ROUTE)
    sel = pl.BlockSpec((c, PEER_SEL), lambda i, k: (i, 0))
    once = pl.Buffered(1)
    key_pos = _route_positions(TC_ROUTE)
    return pl.pallas_call(
        _route_act_kernel,
        grid=(t // c, steps),
        in_specs=[
            pl.BlockSpec((c, D_MODEL), lambda i, k: (i, 0)),
            pl.BlockSpec((D_MODEL, c), lambda i, k: (0, i)),
            pl.BlockSpec((D_MODEL, EXPERT_BLOCK), lambda i, k: (0, k // ROUTE_SPLIT)),
            pl.BlockSpec((2 * PEER_HEADS * D_HALF, D_MODEL), lambda i, k: (0, 0), pipeline_mode=once),
            pl.BlockSpec((2 * PEER_HEADS, N_KEYS, D_HALF), lambda i, k: (0, 0, 0), pipeline_mode=once),
            pl.BlockSpec(key_pos.shape, lambda i, k: (0, 0), pipeline_mode=once),
        ],
        out_specs=[sel, sel, sel],
        out_shape=[jax.ShapeDtypeStruct((t, PEER_SEL), I32), jax.ShapeDtypeStruct((t, PEER_SEL), F32),
                   jax.ShapeDtypeStruct((t, PEER_SEL), F32)],
        scratch_shapes=[pltpu.VMEM((c, N_KEYS, N_KEYS), F32),
                        pltpu.VMEM((PEER_SEL, c), I32), pltpu.VMEM((PEER_SEL, c), F32)] + _route_scratch(TC_ROUTE),
        compiler_params=_tc_params("parallel", "arbitrary"),
        name="route_act",
    )(xb, xt, u_t, wq_t, keys, key_pos)


def _gelu(a):
    return 0.5 * a * (1.0 + lax.erf(a * math.sqrt(0.5)))


def _expert_mix_kernel(normalize, idx_ref, a_ref, gate_ref, v_ref, h_ref, g_ref, o_ref, grid_ref, w_ref, acc_ref):
    k = pl.program_id(1)
    c = idx_ref.shape[0]

    @pl.when(k == 0)
    def _():
        acc_ref[...] = h_ref[...]
        w_ref[...] = gate_ref[...] * _gelu(a_ref[...])

        _, key_bf = _key_rows()
        one, zero = jnp.ones((), BF16), jnp.zeros((), BF16)

        def group(gi, carry):
            t0 = pl.multiple_of(gi * TOKENS_PER_TRIP, TOKENS_PER_TRIP)
            _, _, first_bf, second_bf = _split_keys(idx_ref[pl.ds(t0, TOKENS_PER_TRIP), :])
            w_bf = w_ref[pl.ds(t0, TOKENS_PER_TRIP), :].astype(BF16)
            for g0 in range(0, TOKENS_PER_TRIP, SUBLANES):
                mats = []
                for g in range(g0, g0 + SUBLANES):
                    weighted = jnp.where(key_bf == first_bf[g:g + 1], w_bf[g:g + 1], zero)
                    pick_second = jnp.where(key_bf == second_bf[g:g + 1], one, zero)
                    mats.append(lax.dot_general(weighted, pick_second, (((1,), (1,)), ((), ())),
                                                preferred_element_type=F32))
                grid_ref[:, pl.ds(t0 + g0, SUBLANES), :] = jnp.swapaxes(jnp.stack(mats, axis=0), 0, 1)
            return carry

        lax.fori_loop(0, c // TOKENS_PER_TRIP, group, 0)

    dense = jnp.concatenate([grid_ref[k * KEYS_PER_BLOCK + kk] for kk in range(KEYS_PER_BLOCK)], axis=1)
    acc_ref[...] += jnp.dot(dense.astype(BF16), v_ref[...], preferred_element_type=F32)

    @pl.when(k == pl.num_programs(1) - 1)
    def _():
        hh = acc_ref[...]
        o_ref[...] = hh * _rms_scale(hh) * g_ref[...] if normalize else hh


def _expert_mix(idx, a, gate, v_tab, h2, g, normalize):
    t = idx.shape[0]
    c = TC_EXPERT
    sel = pl.BlockSpec((c, PEER_SEL), lambda i, k: (i, 0))
    tok = pl.BlockSpec((c, D_MODEL), lambda i, k: (i, 0))
    return pl.pallas_call(
        functools.partial(_expert_mix_kernel, normalize),
        grid=(t // c, N_EXPERTS // EXPERT_BLOCK),
        in_specs=[sel, sel, sel, pl.BlockSpec((EXPERT_BLOCK, D_MODEL), lambda i, k: (k, 0)), tok,
                  pl.BlockSpec((1, D_MODEL), lambda i, k: (0, 0))],
        out_specs=tok,
        out_shape=jax.ShapeDtypeStruct(h2.shape, F32),
        scratch_shapes=[pltpu.VMEM((N_KEYS, c, N_KEYS), F32), pltpu.VMEM((c, PEER_SEL), F32),
                        pltpu.VMEM((c, D_MODEL), F32)],
        compiler_params=_tc_params("parallel", "arbitrary"),
        name="expert_mix",
    )(idx, a, gate, v_tab, h2, g)


SC_CORES = 2
SC_SUBCORES = 16
SC_WORKERS = SC_CORES * SC_SUBCORES
SC_LANES = 16
SC_GATHER_ROWS = 16
SC_GATHERS_PER_TOKEN = PEER_SEL // SC_GATHER_ROWS
SC_TOKENS = 16
SC_CHUNKS = D_MODEL // SC_LANES
SC_SHARE = 4
TM_ELEM = 1024

_SC_PARAMS = pltpu.CompilerParams(needs_layout_passes=False)


def _route_kernel(xt_ref, wq_ref, keys_ref, key_pos_ref, idx_ref, gate_ref, idx_scr, gate_scr, *route_scratch):
    everything = pl.ds(0, xt_ref.shape[1])

    def head_body(h, carry):
        _route_head(xt_ref[...], wq_ref, keys_ref, key_pos_ref, h, idx_scr, gate_scr, route_scratch, everything,
                    rolled=True)
        return carry

    lax.fori_loop(0, PEER_HEADS, head_body, 0)
    idx_ref[...] = idx_scr[...].T
    gate_ref[...] = gate_scr[...].T


def _route(xt, wq_t, keys, first):
    t = xt.shape[1] - first
    c = TC_ROUTE
    sel = pl.BlockSpec((c, PEER_SEL), lambda i: (i, 0))
    key_pos = _route_positions(c)
    return pl.pallas_call(
        _route_kernel,
        grid=(t // c,),
        in_specs=[
            pl.BlockSpec((D_MODEL, c), lambda i: (0, i + first // c)),
            pl.BlockSpec((2 * PEER_HEADS * D_HALF, D_MODEL), lambda i: (0, 0)),
            pl.BlockSpec((2 * PEER_HEADS, N_KEYS, D_HALF), lambda i: (0, 0, 0)),
            pl.BlockSpec(key_pos.shape, lambda i: (0, 0)),
        ],
        out_specs=[sel, sel],
        out_shape=[jax.ShapeDtypeStruct((t, PEER_SEL), I32), jax.ShapeDtypeStruct((t, PEER_SEL), F32)],
        scratch_shapes=[pltpu.VMEM((PEER_SEL, c), I32), pltpu.VMEM((PEER_SEL, c), F32)] + _route_scratch(c),
        compiler_params=_tc_params("parallel"),
        name="route",
    )(xt, wq_t, keys, key_pos)


def _expert_weight_kernel(a_ref, g_ref, after_ref, w_ref):
    del after_ref
    w_ref[...] = g_ref[...] * _gelu(a_ref[...])


def _expert_weight(a, g, after):
    t = a.shape[0]
    spec = pl.BlockSpec((TM_ELEM, PEER_SEL), lambda i: (i, 0))
    return pl.pallas_call(
        _expert_weight_kernel,
        grid=(t // TM_ELEM,),
        in_specs=[spec, spec, pl.BlockSpec(memory_space=pl.ANY)],
        out_specs=spec,
        out_shape=jax.ShapeDtypeStruct((t, PEER_SEL), F32),
        compiler_params=_tc_params("parallel"),
        name="expert_weight",
    )(a, g, after)


def _residual_kernel(normalize, out_in_ref, h_ref, y_ref, g_ref, o_ref):
    del out_in_ref
    h = h_ref[...] + y_ref[...]
    o_ref[...] = h * _rms_scale(h) * g_ref[...] if normalize else h


def _residual(out, h2, y2, g, normalize):
    first = h2.shape[0] - y2.shape[0]
    tail = pl.BlockSpec((TM_ELEM, D_MODEL), lambda i: (i + first // TM_ELEM, 0))
    return pl.pallas_call(
        functools.partial(_residual_kernel, normalize),
        grid=(y2.shape[0] // TM_ELEM,),
        in_specs=[pl.BlockSpec(memory_space=pl.ANY), tail, pl.BlockSpec((TM_ELEM, D_MODEL), lambda i: (i, 0)),
                  pl.BlockSpec((1, D_MODEL), lambda i: (0, 0))],
        out_specs=tail,
        out_shape=jax.ShapeDtypeStruct(out.shape, F32),
        input_output_aliases={0: 0},
        compiler_params=_tc_params("parallel"),
        name="residual",
    )(out, h2, y2, g)


def _sc_mesh():
    return plsc.VectorSubcoreMesh(core_axis_name="c", subcore_axis_name="s")


def _sc_worker_id():
    return lax.axis_index("s") * SC_CORES + lax.axis_index("c")


def _sc_gather_pipeline(tab_hbm, idx_v, rows_v, sems, compute):
    n_gathers = idx_v.shape[0]

    def gather(n, slot):
        return pltpu.make_async_copy(tab_hbm.at[idx_v[n, :]], rows_v.at[slot], sems.at[slot])

    gather(0, 0).start()

    def step(n2, carry):
        for slot in range(2):
            n = n2 * 2 + slot

            @pl.when(n + 1 < n_gathers)
            def _():
                gather(n + 1, 1 - slot).start()

            gather(n, slot).wait()
            compute(n, slot)
        return carry

    lax.fori_loop(0, n_gathers // 2, step, 0)


def _expert_dots_sc(u_tab, idx, xn2):
    t = idx.shape[0]
    first = xn2.shape[0] - t
    tok_per_worker = t // SC_WORKERS
    idx2 = idx.reshape(t * SC_GATHERS_PER_TOKEN, SC_GATHER_ROWS)

    def body(u_hbm, idx_hbm, x_hbm, a_hbm, idx_v, x_v, rows_v, a_v, sems):
        wid = _sc_worker_id()
        lanes = lax.iota(I32, SC_LANES)

        def compute(n, slot):
            tl = n // SC_GATHERS_PER_TOKEN
            g = n % SC_GATHERS_PER_TOKEN

            def chunk(c, accs):
                off = pl.multiple_of(c * SC_LANES, SC_LANES)
                xv = x_v[tl, pl.ds(off, SC_LANES)]
                return tuple(accs[r] + rows_v[slot, r, pl.ds(off, SC_LANES)] * xv for r in range(SC_GATHER_ROWS))

            accs = lax.fori_loop(0, SC_CHUNKS, chunk,
                                 tuple(jnp.zeros((SC_LANES,), F32) for _ in range(SC_GATHER_ROWS)))
            tot = jnp.zeros((SC_LANES,), F32)
            for r in range(SC_GATHER_ROWS):
                tot = jnp.where(lanes == r, jnp.sum(accs[r]), tot)
            a_v[tl, pl.ds(pl.multiple_of(g * SC_GATHER_ROWS, SC_LANES), SC_LANES)] = tot

        def block(bi, carry):
            tok0 = wid * tok_per_worker + bi * SC_TOKENS
            pltpu.sync_copy(idx_hbm.at[pl.ds(tok0 * SC_GATHERS_PER_TOKEN, SC_TOKENS * SC_GATHERS_PER_TOKEN)], idx_v)
            pltpu.sync_copy(x_hbm.at[pl.ds(first + tok0, SC_TOKENS)], x_v)
            _sc_gather_pipeline(u_hbm, idx_v, rows_v, sems, compute)
            pltpu.sync_copy(a_v, a_hbm.at[pl.ds(tok0, SC_TOKENS)])
            return carry

        lax.fori_loop(0, tok_per_worker // SC_TOKENS, block, 0)

    return pl.kernel(
        body,
        out_type=jax.ShapeDtypeStruct((t, PEER_SEL), F32),
        mesh=_sc_mesh(),
        scratch_types=[
            pltpu.VMEM((SC_TOKENS * SC_GATHERS_PER_TOKEN, SC_GATHER_ROWS), I32),
            pltpu.VMEM((SC_TOKENS, D_MODEL), F32),
            pltpu.VMEM((2, SC_GATHER_ROWS, D_MODEL), F32),
            pltpu.VMEM((SC_TOKENS, PEER_SEL), F32),
            pltpu.SemaphoreType.DMA((2,)),
        ],
        compiler_params=_SC_PARAMS,
        name="expert_dots_sc",
    )(u_tab, idx2, xn2)


def _expert_mix_sc(v_tab, idx, w):
    t = idx.shape[0]
    tok_per_worker = t // SC_WORKERS
    idx2 = idx.reshape(t * SC_GATHERS_PER_TOKEN, SC_GATHER_ROWS)

    def body(v_hbm, idx_hbm, w_hbm, y_hbm, idx_v, w_v, rows_v, y_v, sems):
        wid = _sc_worker_id()

        def compute(n, slot):
            tl = n // SC_GATHERS_PER_TOKEN
            g = n % SC_GATHERS_PER_TOKEN
            wvec = w_v[tl, pl.ds(pl.multiple_of(g * SC_GATHER_ROWS, SC_LANES), SC_LANES)]

            for half in range(2):
                base = half * (D_MODEL // 2)

                def row(r, accs):
                    wv = jnp.take_along_axis(wvec, jnp.full((SC_LANES,), r, I32), axis=0)
                    return tuple(accs[c] + wv * rows_v[slot, r, pl.ds(base + c * SC_LANES, SC_LANES)]
                                 for c in range(SC_CHUNKS // 2))

                accs = lax.fori_loop(0, SC_GATHER_ROWS, row,
                                     tuple(jnp.zeros((SC_LANES,), F32) for _ in range(SC_CHUNKS // 2)))
                for c in range(SC_CHUNKS // 2):
                    sl = pl.ds(base + c * SC_LANES, SC_LANES)
                    y_v[tl, sl] = y_v[tl, sl] + accs[c]

        def block(bi, carry):
            tok0 = wid * tok_per_worker + bi * SC_TOKENS
            pltpu.sync_copy(idx_hbm.at[pl.ds(tok0 * SC_GATHERS_PER_TOKEN, SC_TOKENS * SC_GATHERS_PER_TOKEN)], idx_v)
            pltpu.sync_copy(w_hbm.at[pl.ds(tok0, SC_TOKENS)], w_v)
            zero = jnp.zeros((SC_LANES,), F32)
            for tl in range(SC_TOKENS):
                for c in range(SC_CHUNKS):
                    y_v[tl, pl.ds(c * SC_LANES, SC_LANES)] = zero
            _sc_gather_pipeline(v_hbm, idx_v, rows_v, sems, compute)
            pltpu.sync_copy(y_v, y_hbm.at[pl.ds(tok0, SC_TOKENS)])
            return carry

        lax.fori_loop(0, tok_per_worker // SC_TOKENS, block, 0)

    return pl.kernel(
        body,
        out_type=jax.ShapeDtypeStruct((t, D_MODEL), F32),
        mesh=_sc_mesh(),
        scratch_types=[
            pltpu.VMEM((SC_TOKENS * SC_GATHERS_PER_TOKEN, SC_GATHER_ROWS), I32),
            pltpu.VMEM((SC_TOKENS, PEER_SEL), F32),
            pltpu.VMEM((2, SC_GATHER_ROWS, D_MODEL), F32),
            pltpu.VMEM((SC_TOKENS, D_MODEL), F32),
            pltpu.SemaphoreType.DMA((2,)),
        ],
        compiler_params=_SC_PARAMS,
        name="expert_mix_sc",
    )(v_tab, idx2, w)


def _rearranged_in_proj(w_in, b_in):
    def cols(a):
        q = a[..., :Q_W]
        k = a[..., Q_W:Q_W + N_KV_HEADS * HEAD_DIM]
        v = a[..., Q_W + N_KV_HEADS * HEAD_DIM:Q_W + 2 * N_KV_HEADS * HEAD_DIM]
        rest = a[..., Q_W + 2 * N_KV_HEADS * HEAD_DIM:]
        dup = lambda m: jnp.concatenate(
            [m[..., hd * HEAD_DIM:(hd + 1) * HEAD_DIM] for hd in range(N_KV_HEADS) for _ in range(2)], axis=-1)
        return jnp.concatenate([q, dup(k), dup(v), rest], axis=-1)
    return cols(w_in).astype(BF16), cols(b_in)[None, :]


def kernel(x, ln_mix_g, w_in, b_in, attn_sinks, w_attn_up, w_pool_grp, pool_scale, w_pool_up, w_o,
           ln_ffn_g, w_query, sub_keys, u_experts, v_experts, ln_final_g):
    b, s, d = x.shape
    t = b * s
    depth = w_in.shape[0]
    h = x
    for l in range(depth):
        w_r, b_r = _rearranged_in_proj(w_in[l], b_in[l])
        q, kv, pz, gate = _inproj(h.reshape(t, d), ln_mix_g[l][None, :], w_r, b_r)
        o = _attention(attn_sinks[l], q.reshape(b, s, Q_W), kv.reshape(b, s, KV_COLS))
        h, xn2, xb, xt = _merge(o, gate.reshape(b, s, 2 * d), h, pz.reshape(b, s, POOL_W),
                                w_attn_up[l].astype(BF16), w_pool_grp[l].astype(BF16), pool_scale[l][None, :],
                                w_pool_up[l].astype(BF16), w_o[l].astype(BF16), ln_ffn_g[l][None, :])
        xn2, xb = xn2.reshape(t, d), xb.reshape(t, d)
        wq_t = w_query[l].T.astype(BF16)
        keys = sub_keys[l].reshape(2 * PEER_HEADS, SUBLANES, N_KEYS // SUBLANES, D_HALF)
        keys = keys[:, jnp.array(SUBLANE_BITREV)].transpose(0, 2, 1, 3)
        keys = keys.reshape(2 * PEER_HEADS, N_KEYS, D_HALF).astype(BF16)
        u_t = u_experts[l].astype(BF16).reshape(N_KEYS, N_KEYS, d).transpose(2, 1, 0).reshape(d, N_EXPERTS)
        last = l + 1 == depth
        g_out = ln_final_g[None, :]
        h2 = h.reshape(t, d)
        t_tc = t - (t // TC_EXPERT) // SC_SHARE * TC_EXPERT
        if t_tc < t:
            idx_sc, g_sc = _route(xt, wq_t, keys, t_tc)
            a_sc = _expert_dots_sc(u_experts[l], idx_sc, xn2)
        idx, g, a = _route_act(xb, xt, u_t, wq_t, keys, t_tc)
        if t_tc < t:
            y_sc = _expert_mix_sc(v_experts[l], idx_sc, _expert_weight(a_sc, g_sc, a))
        h2 = _expert_mix(idx, a, g, v_experts[l].astype(BF16), h2, g_out, last)
        if t_tc < t:
            h2 = _residual(h2, h.reshape(t, d), y_sc, g_out, last)
        h = h2.reshape(b, s, d)
    return h
```

```python
import functools
import math

import jax
import jax.numpy as jnp
import numpy as np
from jax import lax
from jax.experimental import pallas as pl
from jax.experimental.pallas import tpu as pltpu
from jax.experimental.pallas import tpu_sc as plsc

F32 = jnp.float32
BF16 = jnp.bfloat16
I32 = jnp.int32

D_MODEL = 1024
N_HEADS = 16
N_KV_HEADS = 2
GROUP = N_HEADS // N_KV_HEADS
HEAD_DIM = 64
BLOCK = 128
ATTN_SCALE = 1.0 / math.sqrt(HEAD_DIM)
NEG_INF = -1e30
POOL_WINDOWS = (2, 4, 8, 16)
POOL_GROUP = 128
POOL_W = len(POOL_WINDOWS) * POOL_GROUP
POOL_HIST = max(POOL_WINDOWS)
Q_W = N_HEADS * HEAD_DIM
PEER_HEADS = 8
N_KEYS = 128
N_EXPERTS = N_KEYS * N_KEYS
D_HALF = 128
PEER_TOPK = 16
PEER_SEL = PEER_HEADS * PEER_TOPK
EPS = 1e-5

VMEM_LIMIT_BYTES = 56 * 1024 * 1024
SUBLANES = 8
LANES = 128
SUBLANE_BITREV = (0, 4, 2, 6, 1, 5, 3, 7)

C_Q = 0
C_K = C_Q + Q_W
C_V = C_K + 2 * BLOCK
C_P = C_V + 2 * BLOCK
C_G = C_P + POOL_W
C_END = C_G + 2 * D_MODEL
KV_COLS = C_P - C_K

TM_PROJ = 512
ATTN_Q_BLOCKS = 2
TS_MERGE = 512
ROUTE_SPLIT = 1
TC_ROUTE = 512
TC_EXPERT = 512
EXPERT_BLOCK = 2048
KEYS_PER_BLOCK = EXPERT_BLOCK // N_KEYS
TOKENS_PER_TRIP = 32


def _tc_params(*sem):
    return pltpu.CompilerParams(dimension_semantics=sem, vmem_limit_bytes=VMEM_LIMIT_BYTES)


def _rms_scale(x):
    return lax.rsqrt(jnp.mean(x * x, axis=-1, keepdims=True) + EPS)


def _inproj_kernel(x_ref, g_ref, w_ref, b_ref, q_ref, kv_ref, pz_ref, gate_ref):
    x = x_ref[...]
    xn = (x * _rms_scale(x) * g_ref[...]).astype(BF16)

    def proj(lo, hi):
        return jnp.dot(xn, w_ref[:, lo:hi], preferred_element_type=F32) + b_ref[:, lo:hi]

    q_ref[...] = proj(C_Q, C_K).astype(BF16)
    kv_ref[...] = proj(C_K, C_P).astype(BF16)
    pz_ref[...] = proj(C_P, C_G)
    gate_ref[...] = proj(C_G, C_END).astype(BF16)


def _inproj(x2, g, w, b):
    t = x2.shape[0]
    row = lambda i: (i, 0)
    fixed = lambda i: (0, 0)
    return pl.pallas_call(
        _inproj_kernel,
        grid=(t // TM_PROJ,),
        in_specs=[
            pl.BlockSpec((TM_PROJ, D_MODEL), row),
            pl.BlockSpec((1, D_MODEL), fixed),
            pl.BlockSpec((D_MODEL, C_END), fixed),
            pl.BlockSpec((1, C_END), fixed),
        ],
        out_specs=[
            pl.BlockSpec((TM_PROJ, Q_W), row),
            pl.BlockSpec((TM_PROJ, KV_COLS), row),
            pl.BlockSpec((TM_PROJ, POOL_W), row),
            pl.BlockSpec((TM_PROJ, 2 * D_MODEL), row),
        ],
        out_shape=[
            jax.ShapeDtypeStruct((t, Q_W), BF16),
            jax.ShapeDtypeStruct((t, KV_COLS), BF16),
            jax.ShapeDtypeStruct((t, POOL_W), F32),
            jax.ShapeDtypeStruct((t, 2 * D_MODEL), BF16),
        ],
        compiler_params=_tc_params("parallel"),
        name="inproj",
    )(x2, g, w, b)


def _attn_kernel(sink_ref, q_ref, kvc_ref, kvp_ref, o_ref, s_scr, p_scr):
    j = pl.program_id(1)
    row = lax.broadcasted_iota(I32, (BLOCK, BLOCK), 0)
    col = lax.broadcasted_iota(I32, (BLOCK, BLOCK), 1)
    low = lax.broadcasted_iota(I32, (1, BLOCK), 1) < HEAD_DIM
    zero = jnp.zeros((), BF16)
    for qb in range(ATTN_Q_BLOCKS):
        rows = slice(qb * BLOCK, (qb + 1) * BLOCK)
        kvc = kvc_ref[0, rows, :]
        kvp = kvp_ref[0] if qb == 0 else kvc_ref[0, (qb - 1) * BLOCK:qb * BLOCK, :]
        in_window = jnp.logical_and(col > row, j > 0) if qb == 0 else col > row
        mask = jnp.concatenate([in_window, col <= row], axis=1)
        for hkv in range(N_KV_HEADS):
            kc = slice(hkv * BLOCK, (hkv + 1) * BLOCK)
            k2 = jnp.concatenate([kvp[:, kc], kvc[:, kc]], axis=0)
            for pair in range(GROUP // 2):
                c0 = (hkv * (GROUP // 2) + pair) * BLOCK
                q2 = q_ref[0, rows, c0:c0 + BLOCK]
                for half in range(2):
                    qm = jnp.where(low, q2, zero) if half == 0 else jnp.where(low, zero, q2)
                    s = lax.dot_general(qm, k2, (((1,), (1,)), ((), ())), preferred_element_type=F32)
                    s_scr[hkv * GROUP + pair * 2 + half] = jnp.where(mask, s * ATTN_SCALE, NEG_INF)
        recip = []
        for head in range(N_HEADS):
            s = s_scr[head]
            sink = sink_ref[head]
            m = jnp.maximum(jnp.max(s, axis=-1, keepdims=True), sink)
            e = jnp.exp(s - m)
            p_scr[head] = e.astype(BF16)
            recip.append(1.0 / (jnp.sum(e, axis=-1, keepdims=True) + jnp.exp(sink - m)))
        for hkv in range(N_KV_HEADS):
            vc = slice(2 * BLOCK + hkv * BLOCK, 2 * BLOCK + (hkv + 1) * BLOCK)
            v2 = jnp.concatenate([kvp[:, vc], kvc[:, vc]], axis=0)
            v_half = (jnp.where(low, v2, zero), jnp.where(low, zero, v2))
            for pair in range(GROUP // 2):
                c0 = (hkv * (GROUP // 2) + pair) * BLOCK
                head = hkv * GROUP + pair * 2
                o2 = (jnp.dot(p_scr[head], v_half[0], preferred_element_type=F32)
                      + jnp.dot(p_scr[head + 1], v_half[1], preferred_element_type=F32))
                o2 = o2 * jnp.where(low, recip[head], recip[head + 1])
                o_ref[0, rows, c0:c0 + BLOCK] = o2.astype(BF16)


def _attention(sinks, q3, kv3):
    b, s, _ = q3.shape
    tq = ATTN_Q_BLOCKS * BLOCK
    return pl.pallas_call(
        _attn_kernel,
        grid=(b, s // tq),
        in_specs=[
            pl.BlockSpec(memory_space=pltpu.SMEM),
            pl.BlockSpec((1, tq, Q_W), lambda i, j: (i, j, 0)),
            pl.BlockSpec((1, tq, KV_COLS), lambda i, j: (i, j, 0)),
            pl.BlockSpec((1, BLOCK, KV_COLS), lambda i, j: (i, jnp.maximum(j * ATTN_Q_BLOCKS - 1, 0), 0)),
        ],
        out_specs=pl.BlockSpec((1, tq, Q_W), lambda i, j: (i, j, 0)),
        out_shape=jax.ShapeDtypeStruct((b, s, Q_W), BF16),
        scratch_shapes=[pltpu.VMEM((N_HEADS, BLOCK, 2 * BLOCK), F32), pltpu.VMEM((N_HEADS, BLOCK, 2 * BLOCK), BF16)],
        compiler_params=_tc_params("parallel", "arbitrary"),
        name="attention",
    )(sinks, q3, kv3, kv3)


def _sigmoid(x):
    return 1.0 / (1.0 + jnp.exp(-x))


def _merge_kernel(o_ref, gate_ref, x_ref, pz_ref, pzp_ref, wau_ref, wgrp_ref, scale_ref, wpu_ref,
                  wo_ref, g_ref, h_ref, xn_ref, xb_ref, xt_ref):
    j = pl.program_id(1)
    ts = pz_ref.shape[1]
    prev = jnp.where(j > 0, pzp_ref[0], 0.0)
    ext = jnp.concatenate([prev, pz_ref[0]], axis=0)
    t1 = (j * ts + 1 + lax.broadcasted_iota(I32, (ts, 1), 0)).astype(F32)
    ys = []
    for g, w in enumerate(POOL_WINDOWS):
        e = ext[:, g * POOL_GROUP:(g + 1) * POOL_GROUP]
        tsum = e
        span = 1
        while span < w:
            tsum = tsum + pltpu.roll(tsum, span, 0)
            span *= 2
        pooled = tsum[POOL_HIST:] / jnp.minimum(t1, float(w)) - e[POOL_HIST:]
        y = jnp.dot(pooled.astype(BF16), wgrp_ref[g], preferred_element_type=F32)
        ys.append((y * scale_ref[:, g * POOL_GROUP:(g + 1) * POOL_GROUP]).astype(BF16))
    y_p = jnp.dot(jnp.concatenate(ys, axis=1), wpu_ref[...], preferred_element_type=F32)
    y_a = jnp.dot(o_ref[0], wau_ref[...], preferred_element_type=F32)
    gate = gate_ref[0]
    merged = (_sigmoid(gate[:, :D_MODEL].astype(F32)) * y_a
              + _sigmoid(gate[:, D_MODEL:].astype(F32)) * y_p)
    h = x_ref[0] + jnp.dot(merged.astype(BF16), wo_ref[...], preferred_element_type=F32)
    h_ref[0] = h
    xn = h * _rms_scale(h) * g_ref[...]
    xn_ref[0] = xn
    xb_ref[0] = xn.astype(BF16)
    xt_ref[...] = xn.T.astype(BF16)


def _merge(o3, gate3, x, pz3, wau, wgrp, scale, wpu, wo, g):
    b, s, _ = x.shape
    ts = TS_MERGE
    hist_blocks = ts // POOL_HIST
    tile = lambda i, j: (i, j, 0)
    fixed2 = lambda i, j: (0, 0)
    return pl.pallas_call(
        _merge_kernel,
        grid=(b, s // ts),
        in_specs=[
            pl.BlockSpec((1, ts, Q_W), tile),
            pl.BlockSpec((1, ts, 2 * D_MODEL), tile),
            pl.BlockSpec((1, ts, D_MODEL), tile),
            pl.BlockSpec((1, ts, POOL_W), tile),
            pl.BlockSpec((1, POOL_HIST, POOL_W), lambda i, j: (i, jnp.maximum(j * hist_blocks - 1, 0), 0)),
            pl.BlockSpec((Q_W, D_MODEL), fixed2),
            pl.BlockSpec((len(POOL_WINDOWS), POOL_GROUP, POOL_GROUP), lambda i, j: (0, 0, 0)),
            pl.BlockSpec((1, POOL_W), fixed2),
            pl.BlockSpec((POOL_W, D_MODEL), fixed2),
            pl.BlockSpec((D_MODEL, D_MODEL), fixed2),
            pl.BlockSpec((1, D_MODEL), fixed2),
        ],
        out_specs=[pl.BlockSpec((1, ts, D_MODEL), tile), pl.BlockSpec((1, ts, D_MODEL), tile),
                   pl.BlockSpec((1, ts, D_MODEL), tile),
                   pl.BlockSpec((D_MODEL, ts), lambda i, j: (0, i * (s // ts) + j))],
        out_shape=[jax.ShapeDtypeStruct((b, s, D_MODEL), F32), jax.ShapeDtypeStruct((b, s, D_MODEL), F32),
                   jax.ShapeDtypeStruct((b, s, D_MODEL), BF16), jax.ShapeDtypeStruct((D_MODEL, b * s), BF16)],
        compiler_params=_tc_params("parallel", "arbitrary"),
        name="merge",
    )(o3, gate3, x, pz3, pz3, wau, wgrp, scale, wpu, wo, g)


def _rows(ref, g):
    return ref[g * SUBLANES:(g + 1) * SUBLANES, :]


def _better_left(left, right):
    (vl, pl_), (vr, pr) = left, right
    return jnp.maximum(vl, vr), jnp.where(vl >= vr, pl_, pr)


def _best_sublane(v, p, ordered):
    rows = SUBLANES
    while rows > 1:
        rows //= 2
        vl, vr, pl_, pr = v[:rows], v[rows:2 * rows], p[:rows], p[rows:2 * rows]
        take = (vl >= vr) if ordered else jnp.logical_or(vl > vr, jnp.logical_and(vl == vr, pl_ < pr))
        v, p = jnp.where(take, vl, vr), jnp.where(take, pl_, pr)
    return v, p


def _record(found, k, m, am):
    vals, poss = found
    rank = lax.broadcasted_iota(I32, vals.shape, 0)
    return jnp.where(rank == k, m, vals), jnp.where(rank == k, am, poss)


KEY_COLUMN = 4


def _rounds(one_round, init, rolled):
    if rolled:
        return lax.fori_loop(0, PEER_TOPK, one_round, init)
    carry = init
    for k in range(PEER_TOPK):
        carry = one_round(k, carry)
    return carry


def _top16_keys(problems, key_pos_ref, rolled):
    n_cols = N_KEYS // SUBLANES // KEY_COLUMN
    for val_ref, pos_ref in problems:
        for q in range(n_cols):
            lv = [_rows(val_ref, q * KEY_COLUMN + l) for l in range(KEY_COLUMN)]
            lp = [_rows(key_pos_ref, q * KEY_COLUMN + l) for l in range(KEY_COLUMN)]
            for span in range(KEY_COLUMN - 1, 0, -1):
                for i in range(span):
                    swap = lv[i + 1] > lv[i]
                    lv[i], lv[i + 1] = jnp.maximum(lv[i], lv[i + 1]), jnp.minimum(lv[i], lv[i + 1])
                    lp[i], lp[i + 1] = jnp.where(swap, lp[i + 1], lp[i]), jnp.where(swap, lp[i], lp[i + 1])
            for l in range(KEY_COLUMN):
                val_ref[(q * KEY_COLUMN + l) * SUBLANES:(q * KEY_COLUMN + l + 1) * SUBLANES, :] = lv[l]
                pos_ref[(q * KEY_COLUMN + l) * SUBLANES:(q * KEY_COLUMN + l + 1) * SUBLANES, :] = lp[l]

    def one_round(k, found):
        out = []
        for (val_ref, pos_ref), best in zip(problems, found):
            heads = [(_rows(val_ref, q * KEY_COLUMN), _rows(pos_ref, q * KEY_COLUMN)) for q in range(n_cols)]
            top = heads
            while len(top) > 1:
                top = [_better_left(top[i], top[i + 1]) for i in range(0, len(top), 2)]
            m, am = _best_sublane(*top[0], ordered=True)
            for q in range(n_cols):
                popped = heads[q][1] == am
                for l in range(KEY_COLUMN):
                    g = q * KEY_COLUMN + l
                    rows = slice(g * SUBLANES, (g + 1) * SUBLANES)
                    if l + 1 < KEY_COLUMN:
                        val_ref[rows, :] = jnp.where(popped, _rows(val_ref, g + 1), _rows(val_ref, g))
                        pos_ref[rows, :] = jnp.where(popped, _rows(pos_ref, g + 1), _rows(pos_ref, g))
                    else:
                        val_ref[rows, :] = jnp.where(popped, -jnp.inf, _rows(val_ref, g))
            out.append(_record(best, k, m, am))
        return tuple(out)

    blank = jnp.zeros((PEER_TOPK, key_pos_ref.shape[1]), F32)
    return _rounds(one_round, tuple((blank, blank) for _ in problems), rolled)


def _top16_pairs(sv0, sv1, cand_ref, extra_ref, rolled):
    c = sv0.shape[1]
    sub = lax.broadcasted_iota(I32, (SUBLANES, c), 0)
    for a in range(PEER_TOPK):
        cand_ref[a * SUBLANES:(a + 1) * SUBLANES, :] = jnp.where(
            sub < PEER_TOPK // (a + 1), sv0[a:a + 1] + sv1[:SUBLANES], -jnp.inf)
    extra_ref[...] = sv0[0:1] + sv1[SUBLANES:]
    sub_f = sub.astype(F32)
    extra_pos = sub_f + SUBLANES

    def one_round(k, carry):
        best, popped_count = carry
        head_pos = popped_count * PEER_TOPK + sub_f
        top = (_rows(cand_ref, 0), head_pos), (extra_ref[...], extra_pos)
        (vl, pl_), (vr, pr) = top
        take = jnp.logical_or(vl > vr, jnp.logical_and(vl == vr, pl_ < pr))
        m, am = _best_sublane(jnp.where(take, vl, vr), jnp.where(take, pl_, pr), ordered=False)
        popped = head_pos == am
        for a in range(PEER_TOPK):
            below = _rows(cand_ref, a + 1) if a + 1 < PEER_TOPK else -jnp.inf
            cand_ref[a * SUBLANES:(a + 1) * SUBLANES, :] = jnp.where(popped, below, _rows(cand_ref, a))
        extra_ref[...] = jnp.where(extra_pos == am, -jnp.inf, extra_ref[...])
        return _record(best, k, m, am), popped_count + jnp.where(popped, 1.0, 0.0)

    blank = jnp.zeros((PEER_TOPK, c), F32)
    (fv, fpos), _ = _rounds(one_round, ((blank, blank), jnp.zeros((SUBLANES, c), F32)), rolled)
    return fv, fpos


def _select_row(table, sel, pos16):
    out = []
    for k in range(PEER_TOPK):
        out.append(jnp.sum(jnp.where(pos16 == sel[k:k + 1], table, 0), axis=0, keepdims=True))
    return jnp.concatenate(out, axis=0)


def _route_positions(c):
    s = np.arange(SUBLANES)
    key_pos = np.concatenate([np.array(SUBLANE_BITREV)[s] * (N_KEYS // SUBLANES) + g
                              for g in range(N_KEYS // SUBLANES)])
    return jnp.asarray(np.broadcast_to(key_pos[:, None], (N_KEYS, c)), F32)


def _route_scratch(c):
    return [pltpu.VMEM((2, N_KEYS, c), F32), pltpu.VMEM((2, N_KEYS, c), F32),
            pltpu.VMEM((PEER_TOPK * SUBLANES, c), F32), pltpu.VMEM((SUBLANES, c), F32)]


def _route_head(xt, wq_ref, keys_ref, key_pos_ref, h, idx_scr, gate_scr, scratch, tokens, rolled):
    score_scr, pos_scr, cand_scr, extra_scr = scratch
    c = xt.shape[1]
    pos16 = lax.broadcasted_iota(I32, (PEER_TOPK, c), 0)
    wq = wq_ref[pl.ds(pl.multiple_of(h * 2 * D_HALF, 2 * D_HALF), 2 * D_HALF), :]
    q_t = jnp.dot(wq, xt, preferred_element_type=F32).astype(BF16)
    for half in range(2):
        score_scr[half] = jnp.dot(keys_ref[h * 2 + half], q_t[half * D_HALF:(half + 1) * D_HALF],
                                  preferred_element_type=F32)
    sv, si = [], []
    for v, p in _top16_keys([(score_scr.at[half], pos_scr.at[half]) for half in range(2)], key_pos_ref, rolled):
        sv.append(v)
        si.append(p.astype(I32))
    fv, fpos = _top16_pairs(sv[0], sv[1], cand_scr, extra_scr, rolled)
    fpos = fpos.astype(I32)
    i0 = _select_row(si[0], fpos >> 4, pos16)
    i1 = _select_row(si[1], fpos & (PEER_TOPK - 1), pos16)
    e = jnp.exp(fv - fv[0:1])
    rows = pl.ds(pl.multiple_of(h * PEER_TOPK, PEER_TOPK), PEER_TOPK)
    idx_scr[rows, tokens] = i0 * N_KEYS + i1
    gate_scr[rows, tokens] = e / jnp.sum(e, axis=0, keepdims=True)


def _key_rows():
    key = lax.broadcasted_iota(I32, (N_KEYS, PEER_SEL), 0)
    return key, key.astype(F32).astype(BF16)


def _split_keys(idx_rows):
    first, second = idx_rows >> 7, idx_rows & (N_KEYS - 1)
    as_bf16 = lambda v: v.astype(F32).astype(BF16)
    return first, second, as_bf16(first), as_bf16(second)


def _route_act_kernel(xb_ref, xt_ref, ut_ref, wq_ref, keys_ref, key_pos_ref, idx_ref, gate_ref, a_ref,
                      grid_ref, idx_scr, gate_scr, *route_scratch):
    k = pl.program_id(1)
    c = xb_ref.shape[0]

    chunk_tokens = pl.ds(pl.multiple_of((k // PEER_HEADS) * TC_ROUTE, TC_ROUTE), TC_ROUTE)
    _route_head(xt_ref[:, chunk_tokens], wq_ref, keys_ref, key_pos_ref, k % PEER_HEADS,
                idx_scr, gate_scr, route_scratch, chunk_tokens, rolled=False)

    part = c // ROUTE_SPLIT
    part_tokens = pl.ds(pl.multiple_of((k % ROUTE_SPLIT) * part, part), part)
    dense = jnp.dot(xb_ref[part_tokens, :], ut_ref[...], preferred_element_type=F32)
    by_key = jnp.stack([dense[:, kk * N_KEYS:(kk + 1) * N_KEYS] for kk in range(KEYS_PER_BLOCK)], axis=0)
    second_rows = pl.ds(pl.multiple_of((k // ROUTE_SPLIT) * KEYS_PER_BLOCK, KEYS_PER_BLOCK), KEYS_PER_BLOCK)
    grid_ref[part_tokens, second_rows, :] = jnp.swapaxes(by_key, 0, 1)

    @pl.when(k == pl.num_programs(1) - 1)
    def _():
        idx_ref[...] = idx_scr[...].T
        gate_ref[...] = gate_scr[...].T
        key, key_bf = _key_rows()
        one, zero = jnp.ones((), BF16), jnp.zeros((), BF16)

        def group(gi, carry):
            t0 = pl.multiple_of(gi * TOKENS_PER_TRIP, TOKENS_PER_TRIP)
            _, second, first_bf, _ = _split_keys(idx_ref[pl.ds(t0, TOKENS_PER_TRIP), :])
            rows = []
            for g in range(TOKENS_PER_TRIP):
                pick_first = jnp.where(key_bf == first_bf[g:g + 1], one, zero)
                picked = jnp.dot(grid_ref[t0 + g].astype(BF16), pick_first,
                                 preferred_element_type=F32)
                rows.append(jnp.sum(jnp.where(key == second[g:g + 1], picked, 0.0), axis=0, keepdims=True))
            a_ref[pl.ds(t0, TOKENS_PER_TRIP), :] = jnp.concatenate(rows, axis=0)
            return carry

        lax.fori_loop(0, c // TOKENS_PER_TRIP, group, 0)


def _route_act(xb, xt, u_t, wq_t, keys, t):
    c = TC_EXPERT
    steps = ROUTE_SPLIT * N_EXPERTS // EXPERT_BLOCK
    assert steps == PEER_HEADS * (c // TC_ROUTE)
    sel = pl.BlockSpec((c, PEER_SEL), lambda i, k: (i, 0))
    once = pl.Buffered(1)
    key_pos = _route_positions(TC_ROUTE)
    return pl.pallas_call(
        _route_act_kernel,
        grid=(t // c, steps),
        in_specs=[
            pl.BlockSpec((c, D_MODEL), lambda i, k: (i, 0)),
            pl.BlockSpec((D_MODEL, c), lambda i, k: (0, i)),
            pl.BlockSpec((D_MODEL, EXPERT_BLOCK), lambda i, k: (0, k // ROUTE_SPLIT)),
            pl.BlockSpec((2 * PEER_HEADS * D_HALF, D_MODEL), lambda i, k: (0, 0), pipeline_mode=once),
            pl.BlockSpec((2 * PEER_HEADS, N_KEYS, D_HALF), lambda i, k: (0, 0, 0), pipeline_mode=once),
            pl.BlockSpec(key_pos.shape, lambda i, k: (0, 0), pipeline_mode=once),
        ],
        out_specs=[sel, sel, sel],
        out_shape=[jax.ShapeDtypeStruct((t, PEER_SEL), I32), jax.ShapeDtypeStruct((t, PEER_SEL), F32),
                   jax.ShapeDtypeStruct((t, PEER_SEL), F32)],
        scratch_shapes=[pltpu.VMEM((c, N_KEYS, N_KEYS), F32),
                        pltpu.VMEM((PEER_SEL, c), I32), pltpu.VMEM((PEER_SEL, c), F32)] + _route_scratch(TC_ROUTE),
        compiler_params=_tc_params("parallel", "arbitrary"),
        name="route_act",
    )(xb, xt, u_t, wq_t, keys, key_pos)


def _gelu(a):
    return 0.5 * a * (1.0 + lax.erf(a * math.sqrt(0.5)))


def _expert_mix_kernel(normalize, idx_ref, a_ref, gate_ref, v_ref, h_ref, g_ref, o_ref, grid_ref, w_ref, acc_ref):
    k = pl.program_id(1)
    c = idx_ref.shape[0]

    @pl.when(k == 0)
    def _():
        acc_ref[...] = h_ref[...]
        w_ref[...] = gate_ref[...] * _gelu(a_ref[...])

        _, key_bf = _key_rows()
        one, zero = jnp.ones((), BF16), jnp.zeros((), BF16)

        def group(gi, carry):
            t0 = pl.multiple_of(gi * TOKENS_PER_TRIP, TOKENS_PER_TRIP)
            _, _, first_bf, second_bf = _split_keys(idx_ref[pl.ds(t0, TOKENS_PER_TRIP), :])
            w_bf = w_ref[pl.ds(t0, TOKENS_PER_TRIP), :].astype(BF16)
            for g0 in range(0, TOKENS_PER_TRIP, SUBLANES):
                mats = []
                for g in range(g0, g0 + SUBLANES):
                    weighted = jnp.where(key_bf == first_bf[g:g + 1], w_bf[g:g + 1], zero)
                    pick_second = jnp.where(key_bf == second_bf[g:g + 1], one, zero)
                    mats.append(lax.dot_general(weighted, pick_second, (((1,), (1,)), ((), ())),
                                                preferred_element_type=F32))
                grid_ref[:, pl.ds(t0 + g0, SUBLANES), :] = jnp.swapaxes(jnp.stack(mats, axis=0), 0, 1)
            return carry

        lax.fori_loop(0, c // TOKENS_PER_TRIP, group, 0)

    dense = jnp.concatenate([grid_ref[k * KEYS_PER_BLOCK + kk] for kk in range(KEYS_PER_BLOCK)], axis=1)
    acc_ref[...] += jnp.dot(dense.astype(BF16), v_ref[...], preferred_element_type=F32)

    @pl.when(k == pl.num_programs(1) - 1)
    def _():
        hh = acc_ref[...]
        o_ref[...] = hh * _rms_scale(hh) * g_ref[...] if normalize else hh


def _expert_mix(idx, a, gate, v_tab, h2, g, normalize):
    t = idx.shape[0]
    c = TC_EXPERT
    sel = pl.BlockSpec((c, PEER_SEL), lambda i, k: (i, 0))
    tok = pl.BlockSpec((c, D_MODEL), lambda i, k: (i, 0))
    return pl.pallas_call(
        functools.partial(_expert_mix_kernel, normalize),
        grid=(t // c, N_EXPERTS // EXPERT_BLOCK),
        in_specs=[sel, sel, sel, pl.BlockSpec((EXPERT_BLOCK, D_MODEL), lambda i, k: (k, 0)), tok,
                  pl.BlockSpec((1, D_MODEL), lambda i, k: (0, 0))],
        out_specs=tok,
        out_shape=jax.ShapeDtypeStruct(h2.shape, F32),
        scratch_shapes=[pltpu.VMEM((N_KEYS, c, N_KEYS), F32), pltpu.VMEM((c, PEER_SEL), F32),
                        pltpu.VMEM((c, D_MODEL), F32)],
        compiler_params=_tc_params("parallel", "arbitrary"),
        name="expert_mix",
    )(idx, a, gate, v_tab, h2, g)


SC_CORES = 2
SC_SUBCORES = 16
SC_WORKERS = SC_CORES * SC_SUBCORES
SC_LANES = 16
SC_GATHER_ROWS = 16
SC_GATHERS_PER_TOKEN = PEER_SEL // SC_GATHER_ROWS
SC_TOKENS = 16
SC_CHUNKS = D_MODEL // SC_LANES
SC_SHARE = 4
SC_MIX_BACK = 8
TM_ELEM = 1024

_SC_PARAMS = pltpu.CompilerParams(needs_layout_passes=False)


def _route_kernel(xt_ref, wq_ref, keys_ref, key_pos_ref, idx_ref, gate_ref, idx_scr, gate_scr, *route_scratch):
    everything = pl.ds(0, xt_ref.shape[1])

    def head_body(h, carry):
        _route_head(xt_ref[...], wq_ref, keys_ref, key_pos_ref, h, idx_scr, gate_scr, route_scratch, everything,
                    rolled=True)
        return carry

    lax.fori_loop(0, PEER_HEADS, head_body, 0)
    idx_ref[...] = idx_scr[...].T
    gate_ref[...] = gate_scr[...].T


def _route(xt, wq_t, keys, first):
    t = xt.shape[1] - first
    c = TC_ROUTE
    sel = pl.BlockSpec((c, PEER_SEL), lambda i: (i, 0))
    key_pos = _route_positions(c)
    return pl.pallas_call(
        _route_kernel,
        grid=(t // c,),
        in_specs=[
            pl.BlockSpec((D_MODEL, c), lambda i: (0, i + first // c)),
            pl.BlockSpec((2 * PEER_HEADS * D_HALF, D_MODEL), lambda i: (0, 0)),
            pl.BlockSpec((2 * PEER_HEADS, N_KEYS, D_HALF), lambda i: (0, 0, 0)),
            pl.BlockSpec(key_pos.shape, lambda i: (0, 0)),
        ],
        out_specs=[sel, sel],
        out_shape=[jax.ShapeDtypeStruct((t, PEER_SEL), I32), jax.ShapeDtypeStruct((t, PEER_SEL), F32)],
        scratch_shapes=[pltpu.VMEM((PEER_SEL, c), I32), pltpu.VMEM((PEER_SEL, c), F32)] + _route_scratch(c),
        compiler_params=_tc_params("parallel"),
        name="route",
    )(xt, wq_t, keys, key_pos)


def _expert_weight_kernel(a_ref, g_ref, after_ref, w_ref):
    del after_ref
    w_ref[...] = g_ref[...] * _gelu(a_ref[...])


def _expert_weight(a, g, after):
    t = a.shape[0]
    spec = pl.BlockSpec((TM_ELEM, PEER_SEL), lambda i: (i, 0))
    return pl.pallas_call(
        _expert_weight_kernel,
        grid=(t // TM_ELEM,),
        in_specs=[spec, spec, pl.BlockSpec(memory_space=pl.ANY)],
        out_specs=spec,
        out_shape=jax.ShapeDtypeStruct((t, PEER_SEL), F32),
        compiler_params=_tc_params("parallel"),
        name="expert_weight",
    )(a, g, after)


def _residual_kernel(normalize, out_in_ref, h_ref, y_ref, g_ref, o_ref):
    del out_in_ref
    h = h_ref[...] + y_ref[...]
    o_ref[...] = h * _rms_scale(h) * g_ref[...] if normalize else h


def _residual(out, h2, y2, g, normalize):
    first = h2.shape[0] - y2.shape[0]
    tail = pl.BlockSpec((TM_ELEM, D_MODEL), lambda i: (i + first // TM_ELEM, 0))
    return pl.pallas_call(
        functools.partial(_residual_kernel, normalize),
        grid=(y2.shape[0] // TM_ELEM,),
        in_specs=[pl.BlockSpec(memory_space=pl.ANY), tail, pl.BlockSpec((TM_ELEM, D_MODEL), lambda i: (i, 0)),
                  pl.BlockSpec((1, D_MODEL), lambda i: (0, 0))],
        out_specs=tail,
        out_shape=jax.ShapeDtypeStruct(out.shape, F32),
        input_output_aliases={0: 0},
        compiler_params=_tc_params("parallel"),
        name="residual",
    )(out, h2, y2, g)


def _sc_mesh():
    return plsc.VectorSubcoreMesh(core_axis_name="c", subcore_axis_name="s")


def _sc_worker_id():
    return lax.axis_index("s") * SC_CORES + lax.axis_index("c")


def _sc_gather_pipeline(tab_hbm, idx_v, rows_v, sems, compute):
    n_gathers = idx_v.shape[0]

    def gather(n, slot):
        return pltpu.make_async_copy(tab_hbm.at[idx_v[n, :]], rows_v.at[slot], sems.at[slot])

    gather(0, 0).start()

    def step(n2, carry):
        for slot in range(2):
            n = n2 * 2 + slot

            @pl.when(n + 1 < n_gathers)
            def _():
                gather(n + 1, 1 - slot).start()

            gather(n, slot).wait()
            compute(n, slot)
        return carry

    lax.fori_loop(0, n_gathers // 2, step, 0)


def _expert_dots_sc(u_tab, idx, xn2):
    t = idx.shape[0]
    first = xn2.shape[0] - t
    tok_per_worker = t // SC_WORKERS
    idx2 = idx.reshape(t * SC_GATHERS_PER_TOKEN, SC_GATHER_ROWS)

    def body(u_hbm, idx_hbm, x_hbm, a_hbm, idx_v, x_v, rows_v, a_v, sems):
        wid = _sc_worker_id()
        lanes = lax.iota(I32, SC_LANES)

        def compute(n, slot):
            tl = n // SC_GATHERS_PER_TOKEN
            g = n % SC_GATHERS_PER_TOKEN

            def chunk(c, accs):
                off = pl.multiple_of(c * SC_LANES, SC_LANES)
                xv = x_v[tl, pl.ds(off, SC_LANES)]
                return tuple(accs[r] + rows_v[slot, r, pl.ds(off, SC_LANES)] * xv for r in range(SC_GATHER_ROWS))

            accs = lax.fori_loop(0, SC_CHUNKS, chunk,
                                 tuple(jnp.zeros((SC_LANES,), F32) for _ in range(SC_GATHER_ROWS)))
            tot = jnp.zeros((SC_LANES,), F32)
            for r in range(SC_GATHER_ROWS):
                tot = jnp.where(lanes == r, jnp.sum(accs[r]), tot)
            a_v[tl, pl.ds(pl.multiple_of(g * SC_GATHER_ROWS, SC_LANES), SC_LANES)] = tot

        def block(bi, carry):
            tok0 = wid * tok_per_worker + bi * SC_TOKENS
            pltpu.sync_copy(idx_hbm.at[pl.ds(tok0 * SC_GATHERS_PER_TOKEN, SC_TOKENS * SC_GATHERS_PER_TOKEN)], idx_v)
            pltpu.sync_copy(x_hbm.at[pl.ds(first + tok0, SC_TOKENS)], x_v)
            _sc_gather_pipeline(u_hbm, idx_v, rows_v, sems, compute)
            pltpu.sync_copy(a_v, a_hbm.at[pl.ds(tok0, SC_TOKENS)])
            return carry

        lax.fori_loop(0, tok_per_worker // SC_TOKENS, block, 0)

    return pl.kernel(
        body,
        out_type=jax.ShapeDtypeStruct((t, PEER_SEL), F32),
        mesh=_sc_mesh(),
        scratch_types=[
            pltpu.VMEM((SC_TOKENS * SC_GATHERS_PER_TOKEN, SC_GATHER_ROWS), I32),
            pltpu.VMEM((SC_TOKENS, D_MODEL), F32),
            pltpu.VMEM((2, SC_GATHER_ROWS, D_MODEL), F32),
            pltpu.VMEM((SC_TOKENS, PEER_SEL), F32),
            pltpu.SemaphoreType.DMA((2,)),
        ],
        compiler_params=_SC_PARAMS,
        name="expert_dots_sc",
    )(u_tab, idx2, xn2)


def _expert_mix_sc(v_tab, idx, w):
    t = idx.shape[0]
    tok_per_worker = t // SC_WORKERS
    idx2 = idx.reshape(t * SC_GATHERS_PER_TOKEN, SC_GATHER_ROWS)

    def body(v_hbm, idx_hbm, w_hbm, y_hbm, idx_v, w_v, rows_v, y_v, sems):
        wid = _sc_worker_id()

        def compute(n, slot):
            tl = n // SC_GATHERS_PER_TOKEN
            g = n % SC_GATHERS_PER_TOKEN
            wvec = w_v[tl, pl.ds(pl.multiple_of(g * SC_GATHER_ROWS, SC_LANES), SC_LANES)]

            for half in range(2):
                base = half * (D_MODEL // 2)

                def row(r, accs):
                    wv = jnp.take_along_axis(wvec, jnp.full((SC_LANES,), r, I32), axis=0)
                    return tuple(accs[c] + wv * rows_v[slot, r, pl.ds(base + c * SC_LANES, SC_LANES)]
                                 for c in range(SC_CHUNKS // 2))

                accs = lax.fori_loop(0, SC_GATHER_ROWS, row,
                                     tuple(jnp.zeros((SC_LANES,), F32) for _ in range(SC_CHUNKS // 2)))
                for c in range(SC_CHUNKS // 2):
                    sl = pl.ds(base + c * SC_LANES, SC_LANES)
                    y_v[tl, sl] = y_v[tl, sl] + accs[c]

        def block(bi, carry):
            tok0 = wid * tok_per_worker + bi * SC_TOKENS
            pltpu.sync_copy(idx_hbm.at[pl.ds(tok0 * SC_GATHERS_PER_TOKEN, SC_TOKENS * SC_GATHERS_PER_TOKEN)], idx_v)
            pltpu.sync_copy(w_hbm.at[pl.ds(tok0, SC_TOKENS)], w_v)
            zero = jnp.zeros((SC_LANES,), F32)
            for tl in range(SC_TOKENS):
                for c in range(SC_CHUNKS):
                    y_v[tl, pl.ds(c * SC_LANES, SC_LANES)] = zero
            _sc_gather_pipeline(v_hbm, idx_v, rows_v, sems, compute)
            pltpu.sync_copy(y_v, y_hbm.at[pl.ds(tok0, SC_TOKENS)])
            return carry

        lax.fori_loop(0, tok_per_worker // SC_TOKENS, block, 0)

    return pl.kernel(
        body,
        out_type=jax.ShapeDtypeStruct((t, D_MODEL), F32),
        mesh=_sc_mesh(),
        scratch_types=[
            pltpu.VMEM((SC_TOKENS * SC_GATHERS_PER_TOKEN, SC_GATHER_ROWS), I32),
            pltpu.VMEM((SC_TOKENS, PEER_SEL), F32),
            pltpu.VMEM((2, SC_GATHER_ROWS, D_MODEL), F32),
            pltpu.VMEM((SC_TOKENS, D_MODEL), F32),
            pltpu.SemaphoreType.DMA((2,)),
        ],
        compiler_params=_SC_PARAMS,
        name="expert_mix_sc",
    )(v_tab, idx2, w)


def _rearranged_in_proj(w_in, b_in):
    def cols(a):
        q = a[..., :Q_W]
        k = a[..., Q_W:Q_W + N_KV_HEADS * HEAD_DIM]
        v = a[..., Q_W + N_KV_HEADS * HEAD_DIM:Q_W + 2 * N_KV_HEADS * HEAD_DIM]
        rest = a[..., Q_W + 2 * N_KV_HEADS * HEAD_DIM:]
        dup = lambda m: jnp.concatenate(
            [m[..., hd * HEAD_DIM:(hd + 1) * HEAD_DIM] for hd in range(N_KV_HEADS) for _ in range(2)], axis=-1)
        return jnp.concatenate([q, dup(k), dup(v), rest], axis=-1)
    return cols(w_in).astype(BF16), cols(b_in)[None, :]


def kernel(x, ln_mix_g, w_in, b_in, attn_sinks, w_attn_up, w_pool_grp, pool_scale, w_pool_up, w_o,
           ln_ffn_g, w_query, sub_keys, u_experts, v_experts, ln_final_g):
    b, s, d = x.shape
    t = b * s
    depth = w_in.shape[0]
    h = x
    for l in range(depth):
        w_r, b_r = _rearranged_in_proj(w_in[l], b_in[l])
        q, kv, pz, gate = _inproj(h.reshape(t, d), ln_mix_g[l][None, :], w_r, b_r)
        o = _attention(attn_sinks[l], q.reshape(b, s, Q_W), kv.reshape(b, s, KV_COLS))
        h, xn2, xb, xt = _merge(o, gate.reshape(b, s, 2 * d), h, pz.reshape(b, s, POOL_W),
                                w_attn_up[l].astype(BF16), w_pool_grp[l].astype(BF16), pool_scale[l][None, :],
                                w_pool_up[l].astype(BF16), w_o[l].astype(BF16), ln_ffn_g[l][None, :])
        xn2, xb = xn2.reshape(t, d), xb.reshape(t, d)
        wq_t = w_query[l].T.astype(BF16)
        keys = sub_keys[l].reshape(2 * PEER_HEADS, SUBLANES, N_KEYS // SUBLANES, D_HALF)
        keys = keys[:, jnp.array(SUBLANE_BITREV)].transpose(0, 2, 1, 3)
        keys = keys.reshape(2 * PEER_HEADS, N_KEYS, D_HALF).astype(BF16)
        u_t = u_experts[l].astype(BF16).reshape(N_KEYS, N_KEYS, d).transpose(2, 1, 0).reshape(d, N_EXPERTS)
        last = l + 1 == depth
        g_out = ln_final_g[None, :]
        h2 = h.reshape(t, d)
        sc_tiles = (t // TC_EXPERT) // SC_SHARE
        t_act = t - sc_tiles * TC_EXPERT
        back = sc_tiles // SC_MIX_BACK * TC_EXPERT
        if sc_tiles:
            idx_sc, g_sc = _route(xt, wq_t, keys, t_act)
            a_sc = _expert_dots_sc(u_experts[l], idx_sc, xn2)
        idx, g, a = _route_act(xb, xt, u_t, wq_t, keys, t_act)
        if sc_tiles:
            y_sc = _expert_mix_sc(v_experts[l], idx_sc[back:], _expert_weight(a_sc[back:], g_sc[back:], a))
            idx, g, a = (jnp.concatenate([tc, sc[:back]]) for tc, sc in ((idx, idx_sc), (g, g_sc), (a, a_sc)))
        h2 = _expert_mix(idx, a, g, v_experts[l].astype(BF16), h2, g_out, last)
        if sc_tiles:
            h2 = _residual(h2, h.reshape(t, d), y_sc, g_out, last)
        h = h2.reshape(b, s, d)
    return h
```

```python
import functools
import math

import jax
import jax.numpy as jnp
import numpy as np
from jax import lax
from jax.experimental import pallas as pl
from jax.experimental.pallas import tpu as pltpu
from jax.experimental.pallas import tpu_sc as plsc

F32 = jnp.float32
BF16 = jnp.bfloat16
I32 = jnp.int32

D_MODEL = 1024
N_HEADS = 16
N_KV_HEADS = 2
GROUP = N_HEADS // N_KV_HEADS
HEAD_DIM = 64
BLOCK = 128
ATTN_SCALE = 1.0 / math.sqrt(HEAD_DIM)
NEG_INF = -1e30
POOL_WINDOWS = (2, 4, 8, 16)
POOL_GROUP = 128
POOL_W = len(POOL_WINDOWS) * POOL_GROUP
POOL_HIST = max(POOL_WINDOWS)
Q_W = N_HEADS * HEAD_DIM
PEER_HEADS = 8
N_KEYS = 128
N_EXPERTS = N_KEYS * N_KEYS
D_HALF = 128
PEER_TOPK = 16
PEER_SEL = PEER_HEADS * PEER_TOPK
EPS = 1e-5

VMEM_LIMIT_BYTES = 56 * 1024 * 1024
SUBLANES = 8
LANES = 128
SUBLANE_BITREV = (0, 4, 2, 6, 1, 5, 3, 7)

C_Q = 0
C_K = C_Q + Q_W
C_V = C_K + 2 * BLOCK
C_P = C_V + 2 * BLOCK
C_G = C_P + POOL_W
C_END = C_G + 2 * D_MODEL
KV_COLS = C_P - C_K

TM_PROJ = 512
ATTN_Q_BLOCKS = 2
TS_MERGE = 512
ROUTE_SPLIT = 1
TC_ROUTE = 512
TC_EXPERT = 512
EXPERT_BLOCK = 2048
KEYS_PER_BLOCK = EXPERT_BLOCK // N_KEYS
TOKENS_PER_TRIP = 32


def _tc_params(*sem):
    return pltpu.CompilerParams(dimension_semantics=sem, vmem_limit_bytes=VMEM_LIMIT_BYTES)


def _rms_scale(x):
    return lax.rsqrt(jnp.mean(x * x, axis=-1, keepdims=True) + EPS)


def _inproj_kernel(x_ref, g_ref, w_ref, b_ref, q_ref, kv_ref, pz_ref, gate_ref):
    x = x_ref[...]
    xn = (x * _rms_scale(x) * g_ref[...]).astype(BF16)

    def proj(lo, hi):
        return jnp.dot(xn, w_ref[:, lo:hi], preferred_element_type=F32) + b_ref[:, lo:hi]

    q_ref[...] = proj(C_Q, C_K).astype(BF16)
    kv_ref[...] = proj(C_K, C_P).astype(BF16)
    pz_ref[...] = proj(C_P, C_G)
    gate_ref[...] = proj(C_G, C_END).astype(BF16)


def _inproj(x2, g, w, b):
    t = x2.shape[0]
    row = lambda i: (i, 0)
    fixed = lambda i: (0, 0)
    return pl.pallas_call(
        _inproj_kernel,
        grid=(t // TM_PROJ,),
        in_specs=[
            pl.BlockSpec((TM_PROJ, D_MODEL), row),
            pl.BlockSpec((1, D_MODEL), fixed),
            pl.BlockSpec((D_MODEL, C_END), fixed),
            pl.BlockSpec((1, C_END), fixed),
        ],
        out_specs=[
            pl.BlockSpec((TM_PROJ, Q_W), row),
            pl.BlockSpec((TM_PROJ, KV_COLS), row),
            pl.BlockSpec((TM_PROJ, POOL_W), row),
            pl.BlockSpec((TM_PROJ, 2 * D_MODEL), row),
        ],
        out_shape=[
            jax.ShapeDtypeStruct((t, Q_W), BF16),
            jax.ShapeDtypeStruct((t, KV_COLS), BF16),
            jax.ShapeDtypeStruct((t, POOL_W), F32),
            jax.ShapeDtypeStruct((t, 2 * D_MODEL), BF16),
        ],
        compiler_params=_tc_params("parallel"),
        name="inproj",
    )(x2, g, w, b)


def _attn_kernel(sink_ref, q_ref, kvc_ref, kvp_ref, o_ref, s_scr, p_scr):
    j = pl.program_id(1)
    row = lax.broadcasted_iota(I32, (BLOCK, BLOCK), 0)
    col = lax.broadcasted_iota(I32, (BLOCK, BLOCK), 1)
    low = lax.broadcasted_iota(I32, (1, BLOCK), 1) < HEAD_DIM
    zero = jnp.zeros((), BF16)
    for qb in range(ATTN_Q_BLOCKS):
        rows = slice(qb * BLOCK, (qb + 1) * BLOCK)
        kvc = kvc_ref[0, rows, :]
        kvp = kvp_ref[0] if qb == 0 else kvc_ref[0, (qb - 1) * BLOCK:qb * BLOCK, :]
        in_window = jnp.logical_and(col > row, j > 0) if qb == 0 else col > row
        mask = jnp.concatenate([in_window, col <= row], axis=1)
        for hkv in range(N_KV_HEADS):
            kc = slice(hkv * BLOCK, (hkv + 1) * BLOCK)
            k2 = jnp.concatenate([kvp[:, kc], kvc[:, kc]], axis=0)
            for pair in range(GROUP // 2):
                c0 = (hkv * (GROUP // 2) + pair) * BLOCK
                q2 = q_ref[0, rows, c0:c0 + BLOCK]
                for half in range(2):
                    qm = jnp.where(low, q2, zero) if half == 0 else jnp.where(low, zero, q2)
                    s = lax.dot_general(qm, k2, (((1,), (1,)), ((), ())), preferred_element_type=F32)
                    s_scr[hkv * GROUP + pair * 2 + half] = jnp.where(mask, s, NEG_INF)
        recip = []
        for head in range(N_HEADS):
            s = s_scr[head]
            sink = sink_ref[head]
            m = jnp.maximum(jnp.max(s, axis=-1, keepdims=True), sink)
            e = jnp.exp(s - m)
            p_scr[head] = e.astype(BF16)
            recip.append(1.0 / (jnp.sum(e, axis=-1, keepdims=True) + jnp.exp(sink - m)))
        for hkv in range(N_KV_HEADS):
            vc = slice(2 * BLOCK + hkv * BLOCK, 2 * BLOCK + (hkv + 1) * BLOCK)
            v2 = jnp.concatenate([kvp[:, vc], kvc[:, vc]], axis=0)
            v_half = (jnp.where(low, v2, zero), jnp.where(low, zero, v2))
            for pair in range(GROUP // 2):
                c0 = (hkv * (GROUP // 2) + pair) * BLOCK
                head = hkv * GROUP + pair * 2
                o2 = (jnp.dot(p_scr[head], v_half[0], preferred_element_type=F32)
                      + jnp.dot(p_scr[head + 1], v_half[1], preferred_element_type=F32))
                o2 = o2 * jnp.where(low, recip[head], recip[head + 1])
                o_ref[0, rows, c0:c0 + BLOCK] = o2.astype(BF16)


def _attention(sinks, q3, kv3):
    b, s, _ = q3.shape
    tq = ATTN_Q_BLOCKS * BLOCK
    return pl.pallas_call(
        _attn_kernel,
        grid=(b, s // tq),
        in_specs=[
            pl.BlockSpec(memory_space=pltpu.SMEM),
            pl.BlockSpec((1, tq, Q_W), lambda i, j: (i, j, 0)),
            pl.BlockSpec((1, tq, KV_COLS), lambda i, j: (i, j, 0)),
            pl.BlockSpec((1, BLOCK, KV_COLS), lambda i, j: (i, jnp.maximum(j * ATTN_Q_BLOCKS - 1, 0), 0)),
        ],
        out_specs=pl.BlockSpec((1, tq, Q_W), lambda i, j: (i, j, 0)),
        out_shape=jax.ShapeDtypeStruct((b, s, Q_W), BF16),
        scratch_shapes=[pltpu.VMEM((N_HEADS, BLOCK, 2 * BLOCK), F32), pltpu.VMEM((N_HEADS, BLOCK, 2 * BLOCK), BF16)],
        compiler_params=_tc_params("parallel", "arbitrary"),
        name="attention",
    )(sinks, q3, kv3, kv3)


def _sigmoid(x):
    return 1.0 / (1.0 + jnp.exp(-x))


def _merge_kernel(o_ref, gate_ref, x_ref, pz_ref, pzp_ref, wau_ref, wgrp_ref, scale_ref, wpu_ref,
                  wo_ref, g_ref, h_ref, xn_ref, xb_ref, xt_ref):
    j = pl.program_id(1)
    ts = pz_ref.shape[1]
    prev = jnp.where(j > 0, pzp_ref[0], 0.0)
    ext = jnp.concatenate([prev, pz_ref[0]], axis=0)
    t1 = (j * ts + 1 + lax.broadcasted_iota(I32, (ts, 1), 0)).astype(F32)
    ys = []
    for g, w in enumerate(POOL_WINDOWS):
        e = ext[:, g * POOL_GROUP:(g + 1) * POOL_GROUP]
        tsum = e
        span = 1
        while span < w:
            tsum = tsum + pltpu.roll(tsum, span, 0)
            span *= 2
        pooled = tsum[POOL_HIST:] / jnp.minimum(t1, float(w)) - e[POOL_HIST:]
        y = jnp.dot(pooled.astype(BF16), wgrp_ref[g], preferred_element_type=F32)
        ys.append((y * scale_ref[:, g * POOL_GROUP:(g + 1) * POOL_GROUP]).astype(BF16))
    y_p = jnp.dot(jnp.concatenate(ys, axis=1), wpu_ref[...], preferred_element_type=F32)
    y_a = jnp.dot(o_ref[0], wau_ref[...], preferred_element_type=F32)
    gate = gate_ref[0]
    merged = (_sigmoid(gate[:, :D_MODEL].astype(F32)) * y_a
              + _sigmoid(gate[:, D_MODEL:].astype(F32)) * y_p)
    h = x_ref[0] + jnp.dot(merged.astype(BF16), wo_ref[...], preferred_element_type=F32)
    h_ref[0] = h
    xn = h * _rms_scale(h) * g_ref[...]
    xn_ref[0] = xn
    xb_ref[0] = xn.astype(BF16)
    xt_ref[...] = xn.T.astype(BF16)


def _merge(o3, gate3, x, pz3, wau, wgrp, scale, wpu, wo, g):
    b, s, _ = x.shape
    ts = TS_MERGE
    hist_blocks = ts // POOL_HIST
    tile = lambda i, j: (i, j, 0)
    fixed2 = lambda i, j: (0, 0)
    return pl.pallas_call(
        _merge_kernel,
        grid=(b, s // ts),
        in_specs=[
            pl.BlockSpec((1, ts, Q_W), tile),
            pl.BlockSpec((1, ts, 2 * D_MODEL), tile),
            pl.BlockSpec((1, ts, D_MODEL), tile),
            pl.BlockSpec((1, ts, POOL_W), tile),
            pl.BlockSpec((1, POOL_HIST, POOL_W), lambda i, j: (i, jnp.maximum(j * hist_blocks - 1, 0), 0)),
            pl.BlockSpec((Q_W, D_MODEL), fixed2),
            pl.BlockSpec((len(POOL_WINDOWS), POOL_GROUP, POOL_GROUP), lambda i, j: (0, 0, 0)),
            pl.BlockSpec((1, POOL_W), fixed2),
            pl.BlockSpec((POOL_W, D_MODEL), fixed2),
            pl.BlockSpec((D_MODEL, D_MODEL), fixed2),
            pl.BlockSpec((1, D_MODEL), fixed2),
        ],
        out_specs=[pl.BlockSpec((1, ts, D_MODEL), tile), pl.BlockSpec((1, ts, D_MODEL), tile),
                   pl.BlockSpec((1, ts, D_MODEL), tile),
                   pl.BlockSpec((D_MODEL, ts), lambda i, j: (0, i * (s // ts) + j))],
        out_shape=[jax.ShapeDtypeStruct((b, s, D_MODEL), F32), jax.ShapeDtypeStruct((b, s, D_MODEL), F32),
                   jax.ShapeDtypeStruct((b, s, D_MODEL), BF16), jax.ShapeDtypeStruct((D_MODEL, b * s), BF16)],
        compiler_params=_tc_params("parallel", "arbitrary"),
        name="merge",
    )(o3, gate3, x, pz3, pz3, wau, wgrp, scale, wpu, wo, g)


def _rows(ref, g):
    return ref[g * SUBLANES:(g + 1) * SUBLANES, :]


def _better_left(left, right):
    (vl, pl_), (vr, pr) = left, right
    return jnp.maximum(vl, vr), jnp.where(vl >= vr, pl_, pr)


def _best_sublane(v, p, ordered):
    rows = SUBLANES
    while rows > 1:
        rows //= 2
        vl, vr, pl_, pr = v[:rows], v[rows:2 * rows], p[:rows], p[rows:2 * rows]
        take = (vl >= vr) if ordered else jnp.logical_or(vl > vr, jnp.logical_and(vl == vr, pl_ < pr))
        v, p = jnp.where(take, vl, vr), jnp.where(take, pl_, pr)
    return v, p


def _record(found, k, m, am):
    vals, poss = found
    rank = lax.broadcasted_iota(I32, vals.shape, 0)
    return jnp.where(rank == k, m, vals), jnp.where(rank == k, am, poss)


KEY_COLUMN = 4


def _rounds(one_round, init, rolled):
    if rolled:
        return lax.fori_loop(0, PEER_TOPK, one_round, init)
    carry = init
    for k in range(PEER_TOPK):
        carry = one_round(k, carry)
    return carry


def _top16_keys(problems, key_pos_ref, rolled):
    n_cols = N_KEYS // SUBLANES // KEY_COLUMN
    for val_ref, pos_ref in problems:
        for q in range(n_cols):
            lv = [_rows(val_ref, q * KEY_COLUMN + l) for l in range(KEY_COLUMN)]
            lp = [_rows(key_pos_ref, q * KEY_COLUMN + l) for l in range(KEY_COLUMN)]
            for span in range(KEY_COLUMN - 1, 0, -1):
                for i in range(span):
                    swap = lv[i + 1] > lv[i]
                    lv[i], lv[i + 1] = jnp.maximum(lv[i], lv[i + 1]), jnp.minimum(lv[i], lv[i + 1])
                    lp[i], lp[i + 1] = jnp.where(swap, lp[i + 1], lp[i]), jnp.where(swap, lp[i], lp[i + 1])
            for l in range(KEY_COLUMN):
                val_ref[(q * KEY_COLUMN + l) * SUBLANES:(q * KEY_COLUMN + l + 1) * SUBLANES, :] = lv[l]
                pos_ref[(q * KEY_COLUMN + l) * SUBLANES:(q * KEY_COLUMN + l + 1) * SUBLANES, :] = lp[l]

    def one_round(k, found):
        out = []
        for (val_ref, pos_ref), best in zip(problems, found):
            heads = [(_rows(val_ref, q * KEY_COLUMN), _rows(pos_ref, q * KEY_COLUMN)) for q in range(n_cols)]
            top = heads
            while len(top) > 1:
                top = [_better_left(top[i], top[i + 1]) for i in range(0, len(top), 2)]
            m, am = _best_sublane(*top[0], ordered=True)
            for q in range(n_cols):
                popped = heads[q][1] == am
                for l in range(KEY_COLUMN):
                    g = q * KEY_COLUMN + l
                    rows = slice(g * SUBLANES, (g + 1) * SUBLANES)
                    if l + 1 < KEY_COLUMN:
                        val_ref[rows, :] = jnp.where(popped, _rows(val_ref, g + 1), _rows(val_ref, g))
                        pos_ref[rows, :] = jnp.where(popped, _rows(pos_ref, g + 1), _rows(pos_ref, g))
                    else:
                        val_ref[rows, :] = jnp.where(popped, -jnp.inf, _rows(val_ref, g))
            out.append(_record(best, k, m, am))
        return tuple(out)

    blank = jnp.zeros((PEER_TOPK, key_pos_ref.shape[1]), F32)
    return _rounds(one_round, tuple((blank, blank) for _ in problems), rolled)


def _top16_pairs(sv0, sv1, cand_ref, extra_ref, rolled):
    c = sv0.shape[1]
    sub = lax.broadcasted_iota(I32, (SUBLANES, c), 0)
    for a in range(PEER_TOPK):
        cand_ref[a * SUBLANES:(a + 1) * SUBLANES, :] = jnp.where(
            sub < PEER_TOPK // (a + 1), sv0[a:a + 1] + sv1[:SUBLANES], -jnp.inf)
    extra_ref[...] = sv0[0:1] + sv1[SUBLANES:]
    sub_f = sub.astype(F32)
    extra_pos = sub_f + SUBLANES

    def one_round(k, carry):
        best, popped_count = carry
        head_pos = popped_count * PEER_TOPK + sub_f
        top = (_rows(cand_ref, 0), head_pos), (extra_ref[...], extra_pos)
        (vl, pl_), (vr, pr) = top
        take = jnp.logical_or(vl > vr, jnp.logical_and(vl == vr, pl_ < pr))
        m, am = _best_sublane(jnp.where(take, vl, vr), jnp.where(take, pl_, pr), ordered=False)
        popped = head_pos == am
        for a in range(PEER_TOPK):
            below = _rows(cand_ref, a + 1) if a + 1 < PEER_TOPK else -jnp.inf
            cand_ref[a * SUBLANES:(a + 1) * SUBLANES, :] = jnp.where(popped, below, _rows(cand_ref, a))
        extra_ref[...] = jnp.where(extra_pos == am, -jnp.inf, extra_ref[...])
        return _record(best, k, m, am), popped_count + jnp.where(popped, 1.0, 0.0)

    blank = jnp.zeros((PEER_TOPK, c), F32)
    (fv, fpos), _ = _rounds(one_round, ((blank, blank), jnp.zeros((SUBLANES, c), F32)), rolled)
    return fv, fpos


def _select_row(table, sel, pos16):
    out = []
    for k in range(PEER_TOPK):
        out.append(jnp.sum(jnp.where(pos16 == sel[k:k + 1], table, 0), axis=0, keepdims=True))
    return jnp.concatenate(out, axis=0)


def _route_positions(c):
    s = np.arange(SUBLANES)
    key_pos = np.concatenate([np.array(SUBLANE_BITREV)[s] * (N_KEYS // SUBLANES) + g
                              for g in range(N_KEYS // SUBLANES)])
    return jnp.asarray(np.broadcast_to(key_pos[:, None], (N_KEYS, c)), F32)


def _route_scratch(c):
    return [pltpu.VMEM((2, N_KEYS, c), F32), pltpu.VMEM((2, N_KEYS, c), F32),
            pltpu.VMEM((PEER_TOPK * SUBLANES, c), F32), pltpu.VMEM((SUBLANES, c), F32)]


def _route_head(xt, wq_ref, keys_ref, key_pos_ref, h, idx_scr, gate_scr, scratch, tokens, rolled):
    score_scr, pos_scr, cand_scr, extra_scr = scratch
    c = xt.shape[1]
    pos16 = lax.broadcasted_iota(I32, (PEER_TOPK, c), 0)
    wq = wq_ref[pl.ds(pl.multiple_of(h * 2 * D_HALF, 2 * D_HALF), 2 * D_HALF), :]
    q_t = jnp.dot(wq, xt, preferred_element_type=F32).astype(BF16)
    for half in range(2):
        score_scr[half] = jnp.dot(keys_ref[h * 2 + half], q_t[half * D_HALF:(half + 1) * D_HALF],
                                  preferred_element_type=F32)
    sv, si = [], []
    for v, p in _top16_keys([(score_scr.at[half], pos_scr.at[half]) for half in range(2)], key_pos_ref, rolled):
        sv.append(v)
        si.append(p.astype(I32))
    fv, fpos = _top16_pairs(sv[0], sv[1], cand_scr, extra_scr, rolled)
    fpos = fpos.astype(I32)
    i0 = _select_row(si[0], fpos >> 4, pos16)
    i1 = _select_row(si[1], fpos & (PEER_TOPK - 1), pos16)
    e = jnp.exp(fv - fv[0:1])
    rows = pl.ds(pl.multiple_of(h * PEER_TOPK, PEER_TOPK), PEER_TOPK)
    idx_scr[rows, tokens] = i0 * N_KEYS + i1
    gate_scr[rows, tokens] = e / jnp.sum(e, axis=0, keepdims=True)


def _key_rows():
    key = lax.broadcasted_iota(I32, (N_KEYS, PEER_SEL), 0)
    return key, key.astype(F32).astype(BF16)


def _split_keys(idx_rows):
    first, second = idx_rows >> 7, idx_rows & (N_KEYS - 1)
    as_bf16 = lambda v: v.astype(F32).astype(BF16)
    return first, second, as_bf16(first), as_bf16(second)


def _route_act_kernel(xb_ref, xt_ref, ut_ref, wq_ref, keys_ref, key_pos_ref, idx_ref, gate_ref, a_ref,
                      grid_ref, idx_scr, gate_scr, *route_scratch):
    k = pl.program_id(1)
    c = xb_ref.shape[0]

    chunk_tokens = pl.ds(pl.multiple_of((k // PEER_HEADS) * TC_ROUTE, TC_ROUTE), TC_ROUTE)
    _route_head(xt_ref[:, chunk_tokens], wq_ref, keys_ref, key_pos_ref, k % PEER_HEADS,
                idx_scr, gate_scr, route_scratch, chunk_tokens, rolled=False)

    part = c // ROUTE_SPLIT
    part_tokens = pl.ds(pl.multiple_of((k % ROUTE_SPLIT) * part, part), part)
    dense = jnp.dot(xb_ref[part_tokens, :], ut_ref[...], preferred_element_type=F32)
    by_key = jnp.stack([dense[:, kk * N_KEYS:(kk + 1) * N_KEYS] for kk in range(KEYS_PER_BLOCK)], axis=0)
    second_rows = pl.ds(pl.multiple_of((k // ROUTE_SPLIT) * KEYS_PER_BLOCK, KEYS_PER_BLOCK), KEYS_PER_BLOCK)
    grid_ref[part_tokens, second_rows, :] = jnp.swapaxes(by_key, 0, 1)

    @pl.when(k == pl.num_programs(1) - 1)
    def _():
        idx_ref[...] = idx_scr[...].T
        gate_ref[...] = gate_scr[...].T
        key, key_bf = _key_rows()
        one, zero = jnp.ones((), BF16), jnp.zeros((), BF16)

        def group(gi, carry):
            t0 = pl.multiple_of(gi * TOKENS_PER_TRIP, TOKENS_PER_TRIP)
            _, second, first_bf, _ = _split_keys(idx_ref[pl.ds(t0, TOKENS_PER_TRIP), :])
            rows = []
            for g in range(TOKENS_PER_TRIP):
                pick_first = jnp.where(key_bf == first_bf[g:g + 1], one, zero)
                picked = jnp.dot(grid_ref[t0 + g].astype(BF16), pick_first,
                                 preferred_element_type=F32)
                rows.append(jnp.sum(jnp.where(key == second[g:g + 1], picked, 0.0), axis=0, keepdims=True))
            a_ref[pl.ds(t0, TOKENS_PER_TRIP), :] = jnp.concatenate(rows, axis=0)
            return carry

        lax.fori_loop(0, c // TOKENS_PER_TRIP, group, 0)


def _route_act(xb, xt, u_t, wq_t, keys, t):
    c = TC_EXPERT
    steps = ROUTE_SPLIT * N_EXPERTS // EXPERT_BLOCK
    assert steps == PEER_HEADS * (c // TC_ROUTE)
    sel = pl.BlockSpec((c, PEER_SEL), lambda i, k: (i, 0))
    once = pl.Buffered(1)
    key_pos = _route_positions(TC_ROUTE)
    return pl.pallas_call(
        _route_act_kernel,
        grid=(t // c, steps),
        in_specs=[
            pl.BlockSpec((c, D_MODEL), lambda i, k: (i, 0)),
            pl.BlockSpec((D_MODEL, c), lambda i, k: (0, i)),
            pl.BlockSpec((D_MODEL, EXPERT_BLOCK), lambda i, k: (0, k // ROUTE_SPLIT)),
            pl.BlockSpec((2 * PEER_HEADS * D_HALF, D_MODEL), lambda i, k: (0, 0), pipeline_mode=once),
            pl.BlockSpec((2 * PEER_HEADS, N_KEYS, D_HALF), lambda i, k: (0, 0, 0), pipeline_mode=once),
            pl.BlockSpec(key_pos.shape, lambda i, k: (0, 0), pipeline_mode=once),
        ],
        out_specs=[sel, sel, sel],
        out_shape=[jax.ShapeDtypeStruct((t, PEER_SEL), I32), jax.ShapeDtypeStruct((t, PEER_SEL), F32),
                   jax.ShapeDtypeStruct((t, PEER_SEL), F32)],
        scratch_shapes=[pltpu.VMEM((c, N_KEYS, N_KEYS), F32),
                        pltpu.VMEM((PEER_SEL, c), I32), pltpu.VMEM((PEER_SEL, c), F32)] + _route_scratch(TC_ROUTE),
        compiler_params=_tc_params("parallel", "arbitrary"),
        name="route_act",
    )(xb, xt, u_t, wq_t, keys, key_pos)


def _gelu(a):
    return 0.5 * a * (1.0 + lax.erf(a * math.sqrt(0.5)))


def _expert_mix_kernel(normalize, idx_ref, a_ref, gate_ref, v_ref, h_ref, g_ref, o_ref, grid_ref, w_ref, acc_ref):
    k = pl.program_id(1)
    c = idx_ref.shape[0]

    @pl.when(k == 0)
    def _():
        acc_ref[...] = h_ref[...]
        w_ref[...] = gate_ref[...] * _gelu(a_ref[...])

        _, key_bf = _key_rows()
        one, zero = jnp.ones((), BF16), jnp.zeros((), BF16)

        def group(gi, carry):
            t0 = pl.multiple_of(gi * TOKENS_PER_TRIP, TOKENS_PER_TRIP)
            _, _, first_bf, second_bf = _split_keys(idx_ref[pl.ds(t0, TOKENS_PER_TRIP), :])
            w_bf = w_ref[pl.ds(t0, TOKENS_PER_TRIP), :].astype(BF16)
            for g0 in range(0, TOKENS_PER_TRIP, SUBLANES):
                mats = []
                for g in range(g0, g0 + SUBLANES):
                    weighted = jnp.where(key_bf == first_bf[g:g + 1], w_bf[g:g + 1], zero)
                    pick_second = jnp.where(key_bf == second_bf[g:g + 1], one, zero)
                    mats.append(lax.dot_general(weighted, pick_second, (((1,), (1,)), ((), ())),
                                                preferred_element_type=F32))
                grid_ref[:, pl.ds(t0 + g0, SUBLANES), :] = jnp.swapaxes(jnp.stack(mats, axis=0), 0, 1)
            return carry

        lax.fori_loop(0, c // TOKENS_PER_TRIP, group, 0)

    dense = jnp.concatenate([grid_ref[k * KEYS_PER_BLOCK + kk] for kk in range(KEYS_PER_BLOCK)], axis=1)
    acc_ref[...] += jnp.dot(dense.astype(BF16), v_ref[...], preferred_element_type=F32)

    @pl.when(k == pl.num_programs(1) - 1)
    def _():
        hh = acc_ref[...]
        o_ref[...] = hh * _rms_scale(hh) * g_ref[...] if normalize else hh


def _expert_mix(idx, a, gate, v_tab, h2, g, normalize):
    t = idx.shape[0]
    c = TC_EXPERT
    sel = pl.BlockSpec((c, PEER_SEL), lambda i, k: (i, 0))
    tok = pl.BlockSpec((c, D_MODEL), lambda i, k: (i, 0))
    return pl.pallas_call(
        functools.partial(_expert_mix_kernel, normalize),
        grid=(t // c, N_EXPERTS // EXPERT_BLOCK),
        in_specs=[sel, sel, sel, pl.BlockSpec((EXPERT_BLOCK, D_MODEL), lambda i, k: (k, 0)), tok,
                  pl.BlockSpec((1, D_MODEL), lambda i, k: (0, 0))],
        out_specs=tok,
        out_shape=jax.ShapeDtypeStruct(h2.shape, F32),
        scratch_shapes=[pltpu.VMEM((N_KEYS, c, N_KEYS), F32), pltpu.VMEM((c, PEER_SEL), F32),
                        pltpu.VMEM((c, D_MODEL), F32)],
        compiler_params=_tc_params("parallel", "arbitrary"),
        name="expert_mix",
    )(idx, a, gate, v_tab, h2, g)


SC_CORES = 2
SC_SUBCORES = 16
SC_WORKERS = SC_CORES * SC_SUBCORES
SC_LANES = 16
SC_GATHER_ROWS = 16
SC_GATHERS_PER_TOKEN = PEER_SEL // SC_GATHER_ROWS
SC_TOKENS = 16
SC_CHUNKS = D_MODEL // SC_LANES
SC_SHARE = 4
SC_MIX_BACK = 4
TM_ELEM = 1024

_SC_PARAMS = pltpu.CompilerParams(needs_layout_passes=False)


def _route_kernel(xt_ref, wq_ref, keys_ref, key_pos_ref, idx_ref, gate_ref, idx_scr, gate_scr, *route_scratch):
    everything = pl.ds(0, xt_ref.shape[1])

    def head_body(h, carry):
        _route_head(xt_ref[...], wq_ref, keys_ref, key_pos_ref, h, idx_scr, gate_scr, route_scratch, everything,
                    rolled=True)
        return carry

    lax.fori_loop(0, PEER_HEADS, head_body, 0)
    idx_ref[...] = idx_scr[...].T
    gate_ref[...] = gate_scr[...].T


def _route(xt, wq_t, keys, first):
    t = xt.shape[1] - first
    c = TC_ROUTE
    sel = pl.BlockSpec((c, PEER_SEL), lambda i: (i, 0))
    key_pos = _route_positions(c)
    return pl.pallas_call(
        _route_kernel,
        grid=(t // c,),
        in_specs=[
            pl.BlockSpec((D_MODEL, c), lambda i: (0, i + first // c)),
            pl.BlockSpec((2 * PEER_HEADS * D_HALF, D_MODEL), lambda i: (0, 0)),
            pl.BlockSpec((2 * PEER_HEADS, N_KEYS, D_HALF), lambda i: (0, 0, 0)),
            pl.BlockSpec(key_pos.shape, lambda i: (0, 0)),
        ],
        out_specs=[sel, sel],
        out_shape=[jax.ShapeDtypeStruct((t, PEER_SEL), I32), jax.ShapeDtypeStruct((t, PEER_SEL), F32)],
        scratch_shapes=[pltpu.VMEM((PEER_SEL, c), I32), pltpu.VMEM((PEER_SEL, c), F32)] + _route_scratch(c),
        compiler_params=_tc_params("parallel"),
        name="route",
    )(xt, wq_t, keys, key_pos)


def _expert_weight_kernel(a_ref, g_ref, after_ref, w_ref):
    del after_ref
    w_ref[...] = g_ref[...] * _gelu(a_ref[...])


def _expert_weight(a, g, after):
    t = a.shape[0]
    spec = pl.BlockSpec((TM_ELEM, PEER_SEL), lambda i: (i, 0))
    return pl.pallas_call(
        _expert_weight_kernel,
        grid=(t // TM_ELEM,),
        in_specs=[spec, spec, pl.BlockSpec(memory_space=pl.ANY)],
        out_specs=spec,
        out_shape=jax.ShapeDtypeStruct((t, PEER_SEL), F32),
        compiler_params=_tc_params("parallel"),
        name="expert_weight",
    )(a, g, after)


def _residual_kernel(normalize, out_in_ref, h_ref, y_ref, g_ref, o_ref):
    del out_in_ref
    h = h_ref[...] + y_ref[...]
    o_ref[...] = h * _rms_scale(h) * g_ref[...] if normalize else h


def _residual(out, h2, y2, g, normalize):
    first = h2.shape[0] - y2.shape[0]
    tail = pl.BlockSpec((TM_ELEM, D_MODEL), lambda i: (i + first // TM_ELEM, 0))
    return pl.pallas_call(
        functools.partial(_residual_kernel, normalize),
        grid=(y2.shape[0] // TM_ELEM,),
        in_specs=[pl.BlockSpec(memory_space=pl.ANY), tail, pl.BlockSpec((TM_ELEM, D_MODEL), lambda i: (i, 0)),
                  pl.BlockSpec((1, D_MODEL), lambda i: (0, 0))],
        out_specs=tail,
        out_shape=jax.ShapeDtypeStruct(out.shape, F32),
        input_output_aliases={0: 0},
        compiler_params=_tc_params("parallel"),
        name="residual",
    )(out, h2, y2, g)


def _sc_mesh():
    return plsc.VectorSubcoreMesh(core_axis_name="c", subcore_axis_name="s")


def _sc_worker_id():
    return lax.axis_index("s") * SC_CORES + lax.axis_index("c")


def _sc_gather_pipeline(tab_hbm, idx_v, rows_v, sems, compute):
    n_gathers = idx_v.shape[0] * SC_GATHERS_PER_TOKEN

    def gather(n, slot):
        col = pl.multiple_of((n % SC_GATHERS_PER_TOKEN) * SC_GATHER_ROWS, SC_GATHER_ROWS)
        rows = idx_v[n // SC_GATHERS_PER_TOKEN, pl.ds(col, SC_GATHER_ROWS)]
        return pltpu.make_async_copy(tab_hbm.at[rows], rows_v.at[slot], sems.at[slot])

    gather(0, 0).start()

    def step(n2, carry):
        for slot in range(2):
            n = n2 * 2 + slot

            @pl.when(n + 1 < n_gathers)
            def _():
                gather(n + 1, 1 - slot).start()

            gather(n, slot).wait()
            compute(n, slot)
        return carry

    lax.fori_loop(0, n_gathers // 2, step, 0)


def _expert_dots_sc(u_tab, idx, xn2):
    t = idx.shape[0]
    first = xn2.shape[0] - t
    tok_per_worker = t // SC_WORKERS

    def body(u_hbm, idx_hbm, x_hbm, a_hbm, idx_v, x_v, rows_v, a_v, sems):
        wid = _sc_worker_id()
        lanes = lax.iota(I32, SC_LANES)

        def compute(n, slot):
            tl = n // SC_GATHERS_PER_TOKEN
            g = n % SC_GATHERS_PER_TOKEN

            def chunk(c, accs):
                off = pl.multiple_of(c * SC_LANES, SC_LANES)
                xv = x_v[tl, pl.ds(off, SC_LANES)]
                return tuple(accs[r] + rows_v[slot, r, pl.ds(off, SC_LANES)] * xv for r in range(SC_GATHER_ROWS))

            accs = lax.fori_loop(0, SC_CHUNKS, chunk,
                                 tuple(jnp.zeros((SC_LANES,), F32) for _ in range(SC_GATHER_ROWS)))
            tot = jnp.zeros((SC_LANES,), F32)
            for r in range(SC_GATHER_ROWS):
                tot = jnp.where(lanes == r, jnp.sum(accs[r]), tot)
            a_v[tl, pl.ds(pl.multiple_of(g * SC_GATHER_ROWS, SC_LANES), SC_LANES)] = tot

        def block(bi, carry):
            tok0 = wid * tok_per_worker + bi * SC_TOKENS
            pltpu.sync_copy(idx_hbm.at[pl.ds(tok0, SC_TOKENS)], idx_v)
            pltpu.sync_copy(x_hbm.at[pl.ds(first + tok0, SC_TOKENS)], x_v)
            _sc_gather_pipeline(u_hbm, idx_v, rows_v, sems, compute)
            pltpu.sync_copy(a_v, a_hbm.at[pl.ds(tok0, SC_TOKENS)])
            return carry

        lax.fori_loop(0, tok_per_worker // SC_TOKENS, block, 0)

    return pl.kernel(
        body,
        out_type=jax.ShapeDtypeStruct((t, PEER_SEL), F32),
        mesh=_sc_mesh(),
        scratch_types=[
            pltpu.VMEM((SC_TOKENS, PEER_SEL), I32),
            pltpu.VMEM((SC_TOKENS, D_MODEL), F32),
            pltpu.VMEM((2, SC_GATHER_ROWS, D_MODEL), F32),
            pltpu.VMEM((SC_TOKENS, PEER_SEL), F32),
            pltpu.SemaphoreType.DMA((2,)),
        ],
        compiler_params=_SC_PARAMS,
        name="expert_dots_sc",
    )(u_tab, idx, xn2)


def _expert_mix_sc(v_tab, idx, w):
    t = idx.shape[0]
    tok_per_worker = t // SC_WORKERS

    def body(v_hbm, idx_hbm, w_hbm, y_hbm, idx_v, w_v, rows_v, y_v, sems):
        wid = _sc_worker_id()

        def compute(n, slot):
            tl = n // SC_GATHERS_PER_TOKEN
            g = n % SC_GATHERS_PER_TOKEN
            wvec = w_v[tl, pl.ds(pl.multiple_of(g * SC_GATHER_ROWS, SC_LANES), SC_LANES)]

            for half in range(2):
                base = half * (D_MODEL // 2)

                def row(r, accs):
                    wv = jnp.take_along_axis(wvec, jnp.full((SC_LANES,), r, I32), axis=0)
                    return tuple(accs[c] + wv * rows_v[slot, r, pl.ds(base + c * SC_LANES, SC_LANES)]
                                 for c in range(SC_CHUNKS // 2))

                accs = lax.fori_loop(0, SC_GATHER_ROWS, row,
                                     tuple(jnp.zeros((SC_LANES,), F32) for _ in range(SC_CHUNKS // 2)))
                for c in range(SC_CHUNKS // 2):
                    sl = pl.ds(base + c * SC_LANES, SC_LANES)
                    y_v[tl, sl] = y_v[tl, sl] + accs[c]

        def block(bi, carry):
            tok0 = wid * tok_per_worker + bi * SC_TOKENS
            pltpu.sync_copy(idx_hbm.at[pl.ds(tok0, SC_TOKENS)], idx_v)
            pltpu.sync_copy(w_hbm.at[pl.ds(tok0, SC_TOKENS)], w_v)
            zero = jnp.zeros((SC_LANES,), F32)
            for tl in range(SC_TOKENS):
                for c in range(SC_CHUNKS):
                    y_v[tl, pl.ds(c * SC_LANES, SC_LANES)] = zero
            _sc_gather_pipeline(v_hbm, idx_v, rows_v, sems, compute)
            pltpu.sync_copy(y_v, y_hbm.at[pl.ds(tok0, SC_TOKENS)])
            return carry

        lax.fori_loop(0, tok_per_worker // SC_TOKENS, block, 0)

    return pl.kernel(
        body,
        out_type=jax.ShapeDtypeStruct((t, D_MODEL), F32),
        mesh=_sc_mesh(),
        scratch_types=[
            pltpu.VMEM((SC_TOKENS, PEER_SEL), I32),
            pltpu.VMEM((SC_TOKENS, PEER_SEL), F32),
            pltpu.VMEM((2, SC_GATHER_ROWS, D_MODEL), F32),
            pltpu.VMEM((SC_TOKENS, D_MODEL), F32),
            pltpu.SemaphoreType.DMA((2,)),
        ],
        compiler_params=_SC_PARAMS,
        name="expert_mix_sc",
    )(v_tab, idx, w)


def _rearranged_in_proj(w_in, b_in):
    assert math.frexp(ATTN_SCALE)[0] == 0.5

    def cols(a):
        q = a[..., :Q_W] * ATTN_SCALE
        k = a[..., Q_W:Q_W + N_KV_HEADS * HEAD_DIM]
        v = a[..., Q_W + N_KV_HEADS * HEAD_DIM:Q_W + 2 * N_KV_HEADS * HEAD_DIM]
        rest = a[..., Q_W + 2 * N_KV_HEADS * HEAD_DIM:]
        dup = lambda m: jnp.concatenate(
            [m[..., hd * HEAD_DIM:(hd + 1) * HEAD_DIM] for hd in range(N_KV_HEADS) for _ in range(2)], axis=-1)
        return jnp.concatenate([q, dup(k), dup(v), rest], axis=-1)
    return cols(w_in).astype(BF16), cols(b_in)[None, :]


def kernel(x, ln_mix_g, w_in, b_in, attn_sinks, w_attn_up, w_pool_grp, pool_scale, w_pool_up, w_o,
           ln_ffn_g, w_query, sub_keys, u_experts, v_experts, ln_final_g):
    b, s, d = x.shape
    t = b * s
    depth = w_in.shape[0]
    h = x
    for l in range(depth):
        w_r, b_r = _rearranged_in_proj(w_in[l], b_in[l])
        q, kv, pz, gate = _inproj(h.reshape(t, d), ln_mix_g[l][None, :], w_r, b_r)
        o = _attention(attn_sinks[l], q.reshape(b, s, Q_W), kv.reshape(b, s, KV_COLS))
        h, xn2, xb, xt = _merge(o, gate.reshape(b, s, 2 * d), h, pz.reshape(b, s, POOL_W),
                                w_attn_up[l].astype(BF16), w_pool_grp[l].astype(BF16), pool_scale[l][None, :],
                                w_pool_up[l].astype(BF16), w_o[l].astype(BF16), ln_ffn_g[l][None, :])
        xn2, xb = xn2.reshape(t, d), xb.reshape(t, d)
        wq_t = w_query[l].T.astype(BF16)
        keys = sub_keys[l].reshape(2 * PEER_HEADS, SUBLANES, N_KEYS // SUBLANES, D_HALF)
        keys = keys[:, jnp.array(SUBLANE_BITREV)].transpose(0, 2, 1, 3)
        keys = keys.reshape(2 * PEER_HEADS, N_KEYS, D_HALF).astype(BF16)
        u_t = u_experts[l].astype(BF16).reshape(N_KEYS, N_KEYS, d).transpose(2, 1, 0).reshape(d, N_EXPERTS)
        last = l + 1 == depth
        g_out = ln_final_g[None, :]
        h2 = h.reshape(t, d)
        sc_tiles = (t // TC_EXPERT) // SC_SHARE
        t_act = t - sc_tiles * TC_EXPERT
        back = sc_tiles // SC_MIX_BACK * TC_EXPERT
        if sc_tiles:
            idx_sc, g_sc = _route(xt, wq_t, keys, t_act)
            a_sc = _expert_dots_sc(u_experts[l], idx_sc, xn2)
        idx, g, a = _route_act(xb, xt, u_t, wq_t, keys, t_act)
        if sc_tiles:
            y_sc = _expert_mix_sc(v_experts[l], idx_sc[back:], _expert_weight(a_sc[back:], g_sc[back:], a))
            idx, g, a = (jnp.concatenate([tc, sc[:back]]) for tc, sc in ((idx, idx_sc), (g, g_sc), (a, a_sc)))
        h2 = _expert_mix(idx, a, g, v_experts[l].astype(BF16), h2, g_out, last)
        if sc_tiles:
            h2 = _residual(h2, h.reshape(t, d), y_sc, g_out, last)
        h = h2.reshape(b, s, d)
    return h
```

```python
import functools
import math

import jax
import jax.numpy as jnp
import numpy as np
from jax import lax
from jax.experimental import pallas as pl
from jax.experimental.pallas import tpu as pltpu
from jax.experimental.pallas import tpu_sc as plsc

F32 = jnp.float32
BF16 = jnp.bfloat16
I32 = jnp.int32

D_MODEL = 1024
N_HEADS = 16
N_KV_HEADS = 2
GROUP = N_HEADS // N_KV_HEADS
HEAD_DIM = 64
BLOCK = 128
ATTN_SCALE = 1.0 / math.sqrt(HEAD_DIM)
NEG_INF = -1e30
POOL_WINDOWS = (2, 4, 8, 16)
POOL_GROUP = 128
POOL_W = len(POOL_WINDOWS) * POOL_GROUP
POOL_HIST = max(POOL_WINDOWS)
Q_W = N_HEADS * HEAD_DIM
PEER_HEADS = 8
N_KEYS = 128
N_EXPERTS = N_KEYS * N_KEYS
D_HALF = 128
PEER_TOPK = 16
PEER_SEL = PEER_HEADS * PEER_TOPK
KEY_BITS = N_KEYS.bit_length() - 1
TOPK_BITS = PEER_TOPK.bit_length() - 1
EPS = 1e-5

VMEM_LIMIT_BYTES = 56 * 1024 * 1024
SUBLANES = 8
LANES = 128
SUBLANE_BITREV = (0, 4, 2, 6, 1, 5, 3, 7)

C_Q = 0
C_K = C_Q + Q_W
C_V = C_K + 2 * BLOCK
C_P = C_V + 2 * BLOCK
C_G = C_P + POOL_W
C_END = C_G + 2 * D_MODEL
KV_COLS = C_P - C_K

TM_PROJ = 512
ATTN_Q_BLOCKS = 2
TS_MERGE = 512
ROUTE_SPLIT = 1
TC_ROUTE = 512
TC_EXPERT = 512
EXPERT_BLOCK = 2048
KEYS_PER_BLOCK = EXPERT_BLOCK // N_KEYS
TOKENS_PER_TRIP = 64


def _tc_params(*sem):
    return pltpu.CompilerParams(dimension_semantics=sem, vmem_limit_bytes=VMEM_LIMIT_BYTES)


def _rms_scale(x):
    return lax.rsqrt(jnp.mean(x * x, axis=-1, keepdims=True) + EPS)


def _inproj_kernel(x_ref, g_ref, w_ref, b_ref, q_ref, kv_ref, pz_ref, gate_ref):
    x = x_ref[...]
    xn = (x * _rms_scale(x) * g_ref[...]).astype(BF16)

    def proj(lo, hi):
        return jnp.dot(xn, w_ref[:, lo:hi], preferred_element_type=F32) + b_ref[:, lo:hi]

    q_ref[...] = proj(C_Q, C_K).astype(BF16)
    kv_ref[...] = proj(C_K, C_P).astype(BF16)
    pz_ref[...] = proj(C_P, C_G)
    gate_ref[...] = proj(C_G, C_END).astype(BF16)


def _inproj(x2, g, w, b):
    t = x2.shape[0]
    row = lambda i: (i, 0)
    fixed = lambda i: (0, 0)
    return pl.pallas_call(
        _inproj_kernel,
        grid=(t // TM_PROJ,),
        in_specs=[
            pl.BlockSpec((TM_PROJ, D_MODEL), row),
            pl.BlockSpec((1, D_MODEL), fixed),
            pl.BlockSpec((D_MODEL, C_END), fixed),
            pl.BlockSpec((1, C_END), fixed),
        ],
        out_specs=[
            pl.BlockSpec((TM_PROJ, Q_W), row),
            pl.BlockSpec((TM_PROJ, KV_COLS), row),
            pl.BlockSpec((TM_PROJ, POOL_W), row),
            pl.BlockSpec((TM_PROJ, 2 * D_MODEL), row),
        ],
        out_shape=[
            jax.ShapeDtypeStruct((t, Q_W), BF16),
            jax.ShapeDtypeStruct((t, KV_COLS), BF16),
            jax.ShapeDtypeStruct((t, POOL_W), F32),
            jax.ShapeDtypeStruct((t, 2 * D_MODEL), BF16),
        ],
        compiler_params=_tc_params("parallel"),
        name="inproj",
    )(x2, g, w, b)


def _attn_kernel(sink_ref, q_ref, kvc_ref, kvp_ref, o_ref, s_scr, p_scr):
    j = pl.program_id(1)
    row = lax.broadcasted_iota(I32, (BLOCK, BLOCK), 0)
    col = lax.broadcasted_iota(I32, (BLOCK, BLOCK), 1)
    low = lax.broadcasted_iota(I32, (1, BLOCK), 1) < HEAD_DIM
    zero = jnp.zeros((), BF16)
    for qb in range(ATTN_Q_BLOCKS):
        rows = slice(qb * BLOCK, (qb + 1) * BLOCK)
        kvc = kvc_ref[0, rows, :]
        kvp = kvp_ref[0] if qb == 0 else kvc_ref[0, (qb - 1) * BLOCK:qb * BLOCK, :]
        in_window = jnp.logical_and(col > row, j > 0) if qb == 0 else col > row
        mask = jnp.concatenate([in_window, col <= row], axis=1)
        for hkv in range(N_KV_HEADS):
            kc = slice(hkv * BLOCK, (hkv + 1) * BLOCK)
            k2 = jnp.concatenate([kvp[:, kc], kvc[:, kc]], axis=0)
            for pair in range(GROUP // 2):
                c0 = (hkv * (GROUP // 2) + pair) * BLOCK
                q2 = q_ref[0, rows, c0:c0 + BLOCK]
                for half in range(2):
                    qm = jnp.where(low, q2, zero) if half == 0 else jnp.where(low, zero, q2)
                    s = lax.dot_general(qm, k2, (((1,), (1,)), ((), ())), preferred_element_type=F32)
                    s_scr[hkv * GROUP + pair * 2 + half] = jnp.where(mask, s, NEG_INF)
        recip = []
        for head in range(N_HEADS):
            s = s_scr[head]
            sink = sink_ref[head]
            m = jnp.maximum(jnp.max(s, axis=-1, keepdims=True), sink)
            e = jnp.exp(s - m)
            p_scr[head] = e.astype(BF16)
            recip.append(1.0 / (jnp.sum(e, axis=-1, keepdims=True) + jnp.exp(sink - m)))
        for hkv in range(N_KV_HEADS):
            vc = slice(2 * BLOCK + hkv * BLOCK, 2 * BLOCK + (hkv + 1) * BLOCK)
            v2 = jnp.concatenate([kvp[:, vc], kvc[:, vc]], axis=0)
            v_half = (jnp.where(low, v2, zero), jnp.where(low, zero, v2))
            for pair in range(GROUP // 2):
                c0 = (hkv * (GROUP // 2) + pair) * BLOCK
                head = hkv * GROUP + pair * 2
                o2 = (jnp.dot(p_scr[head], v_half[0], preferred_element_type=F32)
                      + jnp.dot(p_scr[head + 1], v_half[1], preferred_element_type=F32))
                o2 = o2 * jnp.where(low, recip[head], recip[head + 1])
                o_ref[0, rows, c0:c0 + BLOCK] = o2.astype(BF16)


def _attention(sinks, q3, kv3):
    b, s, _ = q3.shape
    tq = ATTN_Q_BLOCKS * BLOCK
    return pl.pallas_call(
        _attn_kernel,
        grid=(b, s // tq),
        in_specs=[
            pl.BlockSpec(memory_space=pltpu.SMEM),
            pl.BlockSpec((1, tq, Q_W), lambda i, j: (i, j, 0)),
            pl.BlockSpec((1, tq, KV_COLS), lambda i, j: (i, j, 0)),
            pl.BlockSpec((1, BLOCK, KV_COLS), lambda i, j: (i, jnp.maximum(j * ATTN_Q_BLOCKS - 1, 0), 0)),
        ],
        out_specs=pl.BlockSpec((1, tq, Q_W), lambda i, j: (i, j, 0)),
        out_shape=jax.ShapeDtypeStruct((b, s, Q_W), BF16),
        scratch_shapes=[pltpu.VMEM((N_HEADS, BLOCK, 2 * BLOCK), F32), pltpu.VMEM((N_HEADS, BLOCK, 2 * BLOCK), BF16)],
        compiler_params=_tc_params("parallel", "arbitrary"),
        name="attention",
    )(sinks, q3, kv3, kv3)


def _sigmoid(x):
    return 1.0 / (1.0 + jnp.exp(-x))


def _merge_kernel(o_ref, gate_ref, x_ref, pz_ref, pzp_ref, wau_ref, wgrp_ref, scale_ref, wpu_ref,
                  wo_ref, g_ref, h_ref, xn_ref, xb_ref, xt_ref):
    j = pl.program_id(1)
    ts = pz_ref.shape[1]
    prev = jnp.where(j > 0, pzp_ref[0], 0.0)
    ext = jnp.concatenate([prev, pz_ref[0]], axis=0)
    t1 = (j * ts + 1 + lax.broadcasted_iota(I32, (ts, 1), 0)).astype(F32)
    ys = []
    for g, w in enumerate(POOL_WINDOWS):
        e = ext[:, g * POOL_GROUP:(g + 1) * POOL_GROUP]
        tsum = e
        span = 1
        while span < w:
            tsum = tsum + pltpu.roll(tsum, span, 0)
            span *= 2
        pooled = tsum[POOL_HIST:] / jnp.minimum(t1, float(w)) - e[POOL_HIST:]
        y = jnp.dot(pooled.astype(BF16), wgrp_ref[g], preferred_element_type=F32)
        ys.append((y * scale_ref[:, g * POOL_GROUP:(g + 1) * POOL_GROUP]).astype(BF16))
    y_p = jnp.dot(jnp.concatenate(ys, axis=1), wpu_ref[...], preferred_element_type=F32)
    y_a = jnp.dot(o_ref[0], wau_ref[...], preferred_element_type=F32)
    gate = gate_ref[0]
    merged = (_sigmoid(gate[:, :D_MODEL].astype(F32)) * y_a
              + _sigmoid(gate[:, D_MODEL:].astype(F32)) * y_p)
    h = x_ref[0] + jnp.dot(merged.astype(BF16), wo_ref[...], preferred_element_type=F32)
    h_ref[0] = h
    xn = h * _rms_scale(h) * g_ref[...]
    xn_ref[0] = xn
    xb_ref[0] = xn.astype(BF16)
    xt_ref[...] = xn.T.astype(BF16)


def _merge(o3, gate3, x, pz3, wau, wgrp, scale, wpu, wo, g):
    b, s, _ = x.shape
    ts = TS_MERGE
    hist_blocks = ts // POOL_HIST
    tile = lambda i, j: (i, j, 0)
    fixed2 = lambda i, j: (0, 0)
    return pl.pallas_call(
        _merge_kernel,
        grid=(b, s // ts),
        in_specs=[
            pl.BlockSpec((1, ts, Q_W), tile),
            pl.BlockSpec((1, ts, 2 * D_MODEL), tile),
            pl.BlockSpec((1, ts, D_MODEL), tile),
            pl.BlockSpec((1, ts, POOL_W), tile),
            pl.BlockSpec((1, POOL_HIST, POOL_W), lambda i, j: (i, jnp.maximum(j * hist_blocks - 1, 0), 0)),
            pl.BlockSpec((Q_W, D_MODEL), fixed2),
            pl.BlockSpec((len(POOL_WINDOWS), POOL_GROUP, POOL_GROUP), lambda i, j: (0, 0, 0)),
            pl.BlockSpec((1, POOL_W), fixed2),
            pl.BlockSpec((POOL_W, D_MODEL), fixed2),
            pl.BlockSpec((D_MODEL, D_MODEL), fixed2),
            pl.BlockSpec((1, D_MODEL), fixed2),
        ],
        out_specs=[pl.BlockSpec((1, ts, D_MODEL), tile), pl.BlockSpec((1, ts, D_MODEL), tile),
                   pl.BlockSpec((1, ts, D_MODEL), tile),
                   pl.BlockSpec((D_MODEL, ts), lambda i, j: (0, i * (s // ts) + j))],
        out_shape=[jax.ShapeDtypeStruct((b, s, D_MODEL), F32), jax.ShapeDtypeStruct((b, s, D_MODEL), F32),
                   jax.ShapeDtypeStruct((b, s, D_MODEL), BF16), jax.ShapeDtypeStruct((D_MODEL, b * s), BF16)],
        compiler_params=_tc_params("parallel", "arbitrary"),
        name="merge",
    )(o3, gate3, x, pz3, pz3, wau, wgrp, scale, wpu, wo, g)


def _rows(ref, g):
    return ref[g * SUBLANES:(g + 1) * SUBLANES, :]


def _better_left(left, right):
    (vl, pl_), (vr, pr) = left, right
    return jnp.maximum(vl, vr), jnp.where(vl >= vr, pl_, pr)


def _best_sublane(v, p, ordered):
    rows = SUBLANES
    while rows > 1:
        rows //= 2
        vl, vr, pl_, pr = v[:rows], v[rows:2 * rows], p[:rows], p[rows:2 * rows]
        take = (vl >= vr) if ordered else jnp.logical_or(vl > vr, jnp.logical_and(vl == vr, pl_ < pr))
        v, p = jnp.where(take, vl, vr), jnp.where(take, pl_, pr)
    return v, p


def _record(found, k, m, am):
    vals, poss = found
    rank = lax.broadcasted_iota(I32, vals.shape, 0)
    return jnp.where(rank == k, m, vals), jnp.where(rank == k, am, poss)


KEY_COLUMN = 4


def _rounds(one_round, init, rolled):
    if rolled:
        return lax.fori_loop(0, PEER_TOPK, one_round, init)
    carry = init
    for k in range(PEER_TOPK):
        carry = one_round(k, carry)
    return carry


def _top16_keys(problems, key_pos_ref, rolled):
    n_cols = N_KEYS // SUBLANES // KEY_COLUMN
    for val_ref, pos_ref in problems:
        for q in range(n_cols):
            lv = [_rows(val_ref, q * KEY_COLUMN + l) for l in range(KEY_COLUMN)]
            lp = [_rows(key_pos_ref, q * KEY_COLUMN + l) for l in range(KEY_COLUMN)]
            for span in range(KEY_COLUMN - 1, 0, -1):
                for i in range(span):
                    swap = lv[i + 1] > lv[i]
                    lv[i], lv[i + 1] = jnp.maximum(lv[i], lv[i + 1]), jnp.minimum(lv[i], lv[i + 1])
                    lp[i], lp[i + 1] = jnp.where(swap, lp[i + 1], lp[i]), jnp.where(swap, lp[i], lp[i + 1])
            for l in range(KEY_COLUMN):
                val_ref[(q * KEY_COLUMN + l) * SUBLANES:(q * KEY_COLUMN + l + 1) * SUBLANES, :] = lv[l]
                pos_ref[(q * KEY_COLUMN + l) * SUBLANES:(q * KEY_COLUMN + l + 1) * SUBLANES, :] = lp[l]

    def one_round(k, found):
        out = []
        for (val_ref, pos_ref), best in zip(problems, found):
            heads = [(_rows(val_ref, q * KEY_COLUMN), _rows(pos_ref, q * KEY_COLUMN)) for q in range(n_cols)]
            top = heads
            while len(top) > 1:
                top = [_better_left(top[i], top[i + 1]) for i in range(0, len(top), 2)]
            m, am = _best_sublane(*top[0], ordered=True)
            for q in range(n_cols):
                popped = heads[q][1] == am
                for l in range(KEY_COLUMN):
                    g = q * KEY_COLUMN + l
                    rows = slice(g * SUBLANES, (g + 1) * SUBLANES)
                    if l + 1 < KEY_COLUMN:
                        val_ref[rows, :] = jnp.where(popped, _rows(val_ref, g + 1), _rows(val_ref, g))
                        pos_ref[rows, :] = jnp.where(popped, _rows(pos_ref, g + 1), _rows(pos_ref, g))
                    else:
                        val_ref[rows, :] = jnp.where(popped, -jnp.inf, _rows(val_ref, g))
            out.append(_record(best, k, m, am))
        return tuple(out)

    blank = jnp.zeros((PEER_TOPK, key_pos_ref.shape[1]), F32)
    return _rounds(one_round, tuple((blank, blank) for _ in problems), rolled)


def _top16_pairs(sv0, sv1, cand_ref, extra_ref, rolled):
    c = sv0.shape[1]
    sub = lax.broadcasted_iota(I32, (SUBLANES, c), 0)
    for a in range(PEER_TOPK):
        cand_ref[a * SUBLANES:(a + 1) * SUBLANES, :] = jnp.where(
            sub < PEER_TOPK // (a + 1), sv0[a:a + 1] + sv1[:SUBLANES], -jnp.inf)
    extra_ref[...] = sv0[0:1] + sv1[SUBLANES:]
    sub_f = sub.astype(F32)
    extra_pos = sub_f + SUBLANES

    def one_round(k, carry):
        best, popped_count = carry
        head_pos = popped_count * PEER_TOPK + sub_f
        top = (_rows(cand_ref, 0), head_pos), (extra_ref[...], extra_pos)
        (vl, pl_), (vr, pr) = top
        take = jnp.logical_or(vl > vr, jnp.logical_and(vl == vr, pl_ < pr))
        m, am = _best_sublane(jnp.where(take, vl, vr), jnp.where(take, pl_, pr), ordered=False)
        popped = head_pos == am
        for a in range(PEER_TOPK):
            below = _rows(cand_ref, a + 1) if a + 1 < PEER_TOPK else -jnp.inf
            cand_ref[a * SUBLANES:(a + 1) * SUBLANES, :] = jnp.where(popped, below, _rows(cand_ref, a))
        extra_ref[...] = jnp.where(extra_pos == am, -jnp.inf, extra_ref[...])
        return _record(best, k, m, am), popped_count + jnp.where(popped, 1.0, 0.0)

    blank = jnp.zeros((PEER_TOPK, c), F32)
    (fv, fpos), _ = _rounds(one_round, ((blank, blank), jnp.zeros((SUBLANES, c), F32)), rolled)
    return fv, fpos


def _select_row(table, sel, pos16):
    out = []
    for k in range(PEER_TOPK):
        out.append(jnp.sum(jnp.where(pos16 == sel[k:k + 1], table, 0), axis=0, keepdims=True))
    return jnp.concatenate(out, axis=0)


def _route_positions(c):
    s = np.arange(SUBLANES)
    key_pos = np.concatenate([np.array(SUBLANE_BITREV)[s] * (N_KEYS // SUBLANES) + g
                              for g in range(N_KEYS // SUBLANES)])
    return jnp.asarray(np.broadcast_to(key_pos[:, None], (N_KEYS, c)), F32)


def _route_scratch(c):
    return [pltpu.VMEM((2, N_KEYS, c), F32), pltpu.VMEM((2, N_KEYS, c), F32),
            pltpu.VMEM((PEER_TOPK * SUBLANES, c), F32), pltpu.VMEM((SUBLANES, c), F32)]


def _route_head(xt, wq_ref, keys_ref, key_pos_ref, h, idx_scr, gate_scr, scratch, tokens, rolled):
    score_scr, pos_scr, cand_scr, extra_scr = scratch
    c = xt.shape[1]
    pos16 = lax.broadcasted_iota(I32, (PEER_TOPK, c), 0)
    wq = wq_ref[pl.ds(pl.multiple_of(h * 2 * D_HALF, 2 * D_HALF), 2 * D_HALF), :]
    q_t = jnp.dot(wq, xt, preferred_element_type=F32).astype(BF16)
    for half in range(2):
        score_scr[half] = jnp.dot(keys_ref[h * 2 + half], q_t[half * D_HALF:(half + 1) * D_HALF],
                                  preferred_element_type=F32)
    sv, si = [], []
    for v, p in _top16_keys([(score_scr.at[half], pos_scr.at[half]) for half in range(2)], key_pos_ref, rolled):
        sv.append(v)
        si.append(p.astype(I32))
    fv, fpos = _top16_pairs(sv[0], sv[1], cand_scr, extra_scr, rolled)
    fpos = fpos.astype(I32)
    i0 = _select_row(si[0], fpos >> TOPK_BITS, pos16)
    i1 = _select_row(si[1], fpos & (PEER_TOPK - 1), pos16)
    e = jnp.exp(fv - fv[0:1])
    rows = pl.ds(pl.multiple_of(h * PEER_TOPK, PEER_TOPK), PEER_TOPK)
    idx_scr[rows, tokens] = i0 * N_KEYS + i1
    gate_scr[rows, tokens] = e / jnp.sum(e, axis=0, keepdims=True)


def _key_rows():
    key = lax.broadcasted_iota(I32, (N_KEYS, PEER_SEL), 0)
    return key, key.astype(F32).astype(BF16)


def _split_keys(idx_rows):
    first, second = idx_rows >> KEY_BITS, idx_rows & (N_KEYS - 1)
    as_bf16 = lambda v: v.astype(F32).astype(BF16)
    return first, second, as_bf16(first), as_bf16(second)


def _route_act_kernel(xb_ref, xt_ref, ut_ref, wq_ref, keys_ref, key_pos_ref, idx_ref, gate_ref, a_ref,
                      grid_ref, idx_scr, gate_scr, *route_scratch):
    k = pl.program_id(1)
    c = xb_ref.shape[0]

    chunk_tokens = pl.ds(pl.multiple_of((k // PEER_HEADS) * TC_ROUTE, TC_ROUTE), TC_ROUTE)
    _route_head(xt_ref[:, chunk_tokens], wq_ref, keys_ref, key_pos_ref, k % PEER_HEADS,
                idx_scr, gate_scr, route_scratch, chunk_tokens, rolled=False)

    part = c // ROUTE_SPLIT
    part_tokens = pl.ds(pl.multiple_of((k % ROUTE_SPLIT) * part, part), part)
    dense = jnp.dot(xb_ref[part_tokens, :], ut_ref[...], preferred_element_type=F32)
    by_key = jnp.stack([dense[:, kk * N_KEYS:(kk + 1) * N_KEYS] for kk in range(KEYS_PER_BLOCK)], axis=0)
    second_rows = pl.ds(pl.multiple_of((k // ROUTE_SPLIT) * KEYS_PER_BLOCK, KEYS_PER_BLOCK), KEYS_PER_BLOCK)
    grid_ref[part_tokens, second_rows, :] = jnp.swapaxes(by_key, 0, 1)

    @pl.when(k == pl.num_programs(1) - 1)
    def _():
        idx_ref[...] = idx_scr[...].T
        gate_ref[...] = gate_scr[...].T
        key, key_bf = _key_rows()
        one, zero = jnp.ones((), BF16), jnp.zeros((), BF16)

        def group(gi, carry):
            t0 = pl.multiple_of(gi * TOKENS_PER_TRIP, TOKENS_PER_TRIP)
            _, second, first_bf, _ = _split_keys(idx_ref[pl.ds(t0, TOKENS_PER_TRIP), :])
            rows = []
            for g in range(TOKENS_PER_TRIP):
                pick_first = jnp.where(key_bf == first_bf[g:g + 1], one, zero)
                picked = jnp.dot(grid_ref[t0 + g].astype(BF16), pick_first,
                                 preferred_element_type=F32)
                rows.append(jnp.sum(jnp.where(key == second[g:g + 1], picked, 0.0), axis=0, keepdims=True))
            a_ref[pl.ds(t0, TOKENS_PER_TRIP), :] = jnp.concatenate(rows, axis=0)
            return carry

        lax.fori_loop(0, c // TOKENS_PER_TRIP, group, 0)


def _route_act(xb, xt, u_t, wq_t, keys, t):
    c = TC_EXPERT
    steps = ROUTE_SPLIT * N_EXPERTS // EXPERT_BLOCK
    assert steps == PEER_HEADS * (c // TC_ROUTE)
    sel = pl.BlockSpec((c, PEER_SEL), lambda i, k: (i, 0))
    once = pl.Buffered(1)
    key_pos = _route_positions(TC_ROUTE)
    return pl.pallas_call(
        _route_act_kernel,
        grid=(t // c, steps),
        in_specs=[
            pl.BlockSpec((c, D_MODEL), lambda i, k: (i, 0)),
            pl.BlockSpec((D_MODEL, c), lambda i, k: (0, i)),
            pl.BlockSpec((D_MODEL, EXPERT_BLOCK), lambda i, k: (0, k // ROUTE_SPLIT)),
            pl.BlockSpec((2 * PEER_HEADS * D_HALF, D_MODEL), lambda i, k: (0, 0), pipeline_mode=once),
            pl.BlockSpec((2 * PEER_HEADS, N_KEYS, D_HALF), lambda i, k: (0, 0, 0), pipeline_mode=once),
            pl.BlockSpec(key_pos.shape, lambda i, k: (0, 0), pipeline_mode=once),
        ],
        out_specs=[sel, sel, sel],
        out_shape=[jax.ShapeDtypeStruct((t, PEER_SEL), I32), jax.ShapeDtypeStruct((t, PEER_SEL), F32),
                   jax.ShapeDtypeStruct((t, PEER_SEL), F32)],
        scratch_shapes=[pltpu.VMEM((c, N_KEYS, N_KEYS), F32),
                        pltpu.VMEM((PEER_SEL, c), I32), pltpu.VMEM((PEER_SEL, c), F32)] + _route_scratch(TC_ROUTE),
        compiler_params=_tc_params("parallel", "arbitrary"),
        name="route_act",
    )(xb, xt, u_t, wq_t, keys, key_pos)


def _gelu(a):
    return 0.5 * a * (1.0 + lax.erf(a * math.sqrt(0.5)))


def _expert_mix_kernel(normalize, idx_ref, a_ref, gate_ref, v_ref, h_ref, g_ref, o_ref, grid_ref, w_ref, acc_ref):
    k = pl.program_id(1)
    c = idx_ref.shape[0]

    @pl.when(k == 0)
    def _():
        acc_ref[...] = h_ref[...]
        w_ref[...] = gate_ref[...] * _gelu(a_ref[...])

        _, key_bf = _key_rows()
        one, zero = jnp.ones((), BF16), jnp.zeros((), BF16)

        def group(gi, carry):
            t0 = pl.multiple_of(gi * TOKENS_PER_TRIP, TOKENS_PER_TRIP)
            _, _, first_bf, second_bf = _split_keys(idx_ref[pl.ds(t0, TOKENS_PER_TRIP), :])
            w_bf = w_ref[pl.ds(t0, TOKENS_PER_TRIP), :].astype(BF16)
            for g0 in range(0, TOKENS_PER_TRIP, SUBLANES):
                mats = []
                for g in range(g0, g0 + SUBLANES):
                    weighted = jnp.where(key_bf == first_bf[g:g + 1], w_bf[g:g + 1], zero)
                    pick_second = jnp.where(key_bf == second_bf[g:g + 1], one, zero)
                    mats.append(lax.dot_general(weighted, pick_second, (((1,), (1,)), ((), ())),
                                                preferred_element_type=F32))
                grid_ref[:, pl.ds(t0 + g0, SUBLANES), :] = jnp.swapaxes(jnp.stack(mats, axis=0), 0, 1)
            return carry

        lax.fori_loop(0, c // TOKENS_PER_TRIP, group, 0)

    dense = jnp.concatenate([grid_ref[k * KEYS_PER_BLOCK + kk] for kk in range(KEYS_PER_BLOCK)], axis=1)
    acc_ref[...] += jnp.dot(dense.astype(BF16), v_ref[...], preferred_element_type=F32)

    @pl.when(k == pl.num_programs(1) - 1)
    def _():
        hh = acc_ref[...]
        o_ref[...] = hh * _rms_scale(hh) * g_ref[...] if normalize else hh


def _expert_mix(idx, a, gate, v_tab, h2, g, normalize):
    t = idx.shape[0]
    c = TC_EXPERT
    sel = pl.BlockSpec((c, PEER_SEL), lambda i, k: (i, 0))
    tok = pl.BlockSpec((c, D_MODEL), lambda i, k: (i, 0))
    return pl.pallas_call(
        functools.partial(_expert_mix_kernel, normalize),
        grid=(t // c, N_EXPERTS // EXPERT_BLOCK),
        in_specs=[sel, sel, sel, pl.BlockSpec((EXPERT_BLOCK, D_MODEL), lambda i, k: (k, 0)), tok,
                  pl.BlockSpec((1, D_MODEL), lambda i, k: (0, 0))],
        out_specs=tok,
        out_shape=jax.ShapeDtypeStruct(h2.shape, F32),
        scratch_shapes=[pltpu.VMEM((N_KEYS, c, N_KEYS), F32), pltpu.VMEM((c, PEER_SEL), F32),
                        pltpu.VMEM((c, D_MODEL), F32)],
        compiler_params=_tc_params("parallel", "arbitrary"),
        name="expert_mix",
    )(idx, a, gate, v_tab, h2, g)


SC_CORES = 2
SC_SUBCORES = 16
SC_WORKERS = SC_CORES * SC_SUBCORES
SC_LANES = 16
SC_GATHER_ROWS = 16
SC_GATHERS_PER_TOKEN = PEER_SEL // SC_GATHER_ROWS
SC_TOKENS = 16
SC_CHUNKS = D_MODEL // SC_LANES
SC_SHARE = 4
SC_MIX_BACK = 4
TM_ELEM = 1024

_SC_PARAMS = pltpu.CompilerParams(needs_layout_passes=False)


def _route_kernel(xt_ref, wq_ref, keys_ref, key_pos_ref, idx_ref, gate_ref, idx_scr, gate_scr, *route_scratch):
    everything = pl.ds(0, xt_ref.shape[1])

    def head_body(h, carry):
        _route_head(xt_ref[...], wq_ref, keys_ref, key_pos_ref, h, idx_scr, gate_scr, route_scratch, everything,
                    rolled=True)
        return carry

    lax.fori_loop(0, PEER_HEADS, head_body, 0)
    idx_ref[...] = idx_scr[...].T
    gate_ref[...] = gate_scr[...].T


def _route(xt, wq_t, keys, first):
    t = xt.shape[1] - first
    c = TC_ROUTE
    sel = pl.BlockSpec((c, PEER_SEL), lambda i: (i, 0))
    key_pos = _route_positions(c)
    return pl.pallas_call(
        _route_kernel,
        grid=(t // c,),
        in_specs=[
            pl.BlockSpec((D_MODEL, c), lambda i: (0, i + first // c)),
            pl.BlockSpec((2 * PEER_HEADS * D_HALF, D_MODEL), lambda i: (0, 0)),
            pl.BlockSpec((2 * PEER_HEADS, N_KEYS, D_HALF), lambda i: (0, 0, 0)),
            pl.BlockSpec(key_pos.shape, lambda i: (0, 0)),
        ],
        out_specs=[sel, sel],
        out_shape=[jax.ShapeDtypeStruct((t, PEER_SEL), I32), jax.ShapeDtypeStruct((t, PEER_SEL), F32)],
        scratch_shapes=[pltpu.VMEM((PEER_SEL, c), I32), pltpu.VMEM((PEER_SEL, c), F32)] + _route_scratch(c),
        compiler_params=_tc_params("parallel"),
        name="route",
    )(xt, wq_t, keys, key_pos)


def _expert_weight_kernel(a_ref, g_ref, after_ref, w_ref):
    del after_ref
    w_ref[...] = g_ref[...] * _gelu(a_ref[...])


def _expert_weight(a, g, after):
    t = a.shape[0]
    spec = pl.BlockSpec((TM_ELEM, PEER_SEL), lambda i: (i, 0))
    return pl.pallas_call(
        _expert_weight_kernel,
        grid=(t // TM_ELEM,),
        in_specs=[spec, spec, pl.BlockSpec(memory_space=pl.ANY)],
        out_specs=spec,
        out_shape=jax.ShapeDtypeStruct((t, PEER_SEL), F32),
        compiler_params=_tc_params("parallel"),
        name="expert_weight",
    )(a, g, after)


def _residual_kernel(normalize, out_in_ref, h_ref, y_ref, g_ref, o_ref):
    del out_in_ref
    h = h_ref[...] + y_ref[...]
    o_ref[...] = h * _rms_scale(h) * g_ref[...] if normalize else h


def _residual(out, h2, y2, g, normalize):
    first = h2.shape[0] - y2.shape[0]
    tail = pl.BlockSpec((TM_ELEM, D_MODEL), lambda i: (i + first // TM_ELEM, 0))
    return pl.pallas_call(
        functools.partial(_residual_kernel, normalize),
        grid=(y2.shape[0] // TM_ELEM,),
        in_specs=[pl.BlockSpec(memory_space=pl.ANY), tail, pl.BlockSpec((TM_ELEM, D_MODEL), lambda i: (i, 0)),
                  pl.BlockSpec((1, D_MODEL), lambda i: (0, 0))],
        out_specs=tail,
        out_shape=jax.ShapeDtypeStruct(out.shape, F32),
        input_output_aliases={0: 0},
        compiler_params=_tc_params("parallel"),
        name="residual",
    )(out, h2, y2, g)


def _sc_mesh():
    return plsc.VectorSubcoreMesh(core_axis_name="c", subcore_axis_name="s")


def _sc_worker_id():
    return lax.axis_index("s") * SC_CORES + lax.axis_index("c")


def _sc_gather_pipeline(tab_hbm, idx_v, rows_v, sems, compute):
    n_gathers = idx_v.shape[0] * SC_GATHERS_PER_TOKEN

    def gather(n, slot):
        col = pl.multiple_of((n % SC_GATHERS_PER_TOKEN) * SC_GATHER_ROWS, SC_GATHER_ROWS)
        rows = idx_v[n // SC_GATHERS_PER_TOKEN, pl.ds(col, SC_GATHER_ROWS)]
        return pltpu.make_async_copy(tab_hbm.at[rows], rows_v.at[slot], sems.at[slot])

    gather(0, 0).start()

    def step(n2, carry):
        for slot in range(2):
            n = n2 * 2 + slot

            @pl.when(n + 1 < n_gathers)
            def _():
                gather(n + 1, 1 - slot).start()

            gather(n, slot).wait()
            compute(n, slot)
        return carry

    lax.fori_loop(0, n_gathers // 2, step, 0)


def _expert_dots_sc(u_tab, idx, xn2):
    t = idx.shape[0]
    first = xn2.shape[0] - t
    tok_per_worker = t // SC_WORKERS

    def body(u_hbm, idx_hbm, x_hbm, a_hbm, idx_v, x_v, rows_v, a_v, sems):
        wid = _sc_worker_id()
        lanes = lax.iota(I32, SC_LANES)

        def compute(n, slot):
            tl = n // SC_GATHERS_PER_TOKEN
            g = n % SC_GATHERS_PER_TOKEN

            def chunk(c, accs):
                off = pl.multiple_of(c * SC_LANES, SC_LANES)
                xv = x_v[tl, pl.ds(off, SC_LANES)]
                return tuple(accs[r] + rows_v[slot, r, pl.ds(off, SC_LANES)] * xv for r in range(SC_GATHER_ROWS))

            accs = lax.fori_loop(0, SC_CHUNKS, chunk,
                                 tuple(jnp.zeros((SC_LANES,), F32) for _ in range(SC_GATHER_ROWS)))
            tot = jnp.zeros((SC_LANES,), F32)
            for r in range(SC_GATHER_ROWS):
                tot = jnp.where(lanes == r, jnp.sum(accs[r]), tot)
            a_v[tl, pl.ds(pl.multiple_of(g * SC_GATHER_ROWS, SC_LANES), SC_LANES)] = tot

        def block(bi, carry):
            tok0 = wid * tok_per_worker + bi * SC_TOKENS
            pltpu.sync_copy(idx_hbm.at[pl.ds(tok0, SC_TOKENS)], idx_v)
            pltpu.sync_copy(x_hbm.at[pl.ds(first + tok0, SC_TOKENS)], x_v)
            _sc_gather_pipeline(u_hbm, idx_v, rows_v, sems, compute)
            pltpu.sync_copy(a_v, a_hbm.at[pl.ds(tok0, SC_TOKENS)])
            return carry

        lax.fori_loop(0, tok_per_worker // SC_TOKENS, block, 0)

    return pl.kernel(
        body,
        out_type=jax.ShapeDtypeStruct((t, PEER_SEL), F32),
        mesh=_sc_mesh(),
        scratch_types=[
            pltpu.VMEM((SC_TOKENS, PEER_SEL), I32),
            pltpu.VMEM((SC_TOKENS, D_MODEL), F32),
            pltpu.VMEM((2, SC_GATHER_ROWS, D_MODEL), F32),
            pltpu.VMEM((SC_TOKENS, PEER_SEL), F32),
            pltpu.SemaphoreType.DMA((2,)),
        ],
        compiler_params=_SC_PARAMS,
        name="expert_dots_sc",
    )(u_tab, idx, xn2)


def _expert_mix_sc(v_tab, idx, w):
    t = idx.shape[0]
    tok_per_worker = t // SC_WORKERS

    def body(v_hbm, idx_hbm, w_hbm, y_hbm, idx_v, w_v, rows_v, y_v, sems):
        wid = _sc_worker_id()

        def compute(n, slot):
            tl = n // SC_GATHERS_PER_TOKEN
            g = n % SC_GATHERS_PER_TOKEN
            wvec = w_v[tl, pl.ds(pl.multiple_of(g * SC_GATHER_ROWS, SC_LANES), SC_LANES)]

            for half in range(2):
                base = half * (D_MODEL // 2)

                def row(r, accs):
                    wv = jnp.take_along_axis(wvec, jnp.full((SC_LANES,), r, I32), axis=0)
                    return tuple(accs[c] + wv * rows_v[slot, r, pl.ds(base + c * SC_LANES, SC_LANES)]
                                 for c in range(SC_CHUNKS // 2))

                accs = lax.fori_loop(0, SC_GATHER_ROWS, row,
                                     tuple(jnp.zeros((SC_LANES,), F32) for _ in range(SC_CHUNKS // 2)))
                for c in range(SC_CHUNKS // 2):
                    sl = pl.ds(base + c * SC_LANES, SC_LANES)
                    y_v[tl, sl] = y_v[tl, sl] + accs[c]

        def block(bi, carry):
            tok0 = wid * tok_per_worker + bi * SC_TOKENS
            pltpu.sync_copy(idx_hbm.at[pl.ds(tok0, SC_TOKENS)], idx_v)
            pltpu.sync_copy(w_hbm.at[pl.ds(tok0, SC_TOKENS)], w_v)
            zero = jnp.zeros((SC_LANES,), F32)
            for tl in range(SC_TOKENS):
                for c in range(SC_CHUNKS):
                    y_v[tl, pl.ds(c * SC_LANES, SC_LANES)] = zero
            _sc_gather_pipeline(v_hbm, idx_v, rows_v, sems, compute)
            pltpu.sync_copy(y_v, y_hbm.at[pl.ds(tok0, SC_TOKENS)])
            return carry

        lax.fori_loop(0, tok_per_worker // SC_TOKENS, block, 0)

    return pl.kernel(
        body,
        out_type=jax.ShapeDtypeStruct((t, D_MODEL), F32),
        mesh=_sc_mesh(),
        scratch_types=[
            pltpu.VMEM((SC_TOKENS, PEER_SEL), I32),
            pltpu.VMEM((SC_TOKENS, PEER_SEL), F32),
            pltpu.VMEM((2, SC_GATHER_ROWS, D_MODEL), F32),
            pltpu.VMEM((SC_TOKENS, D_MODEL), F32),
            pltpu.SemaphoreType.DMA((2,)),
        ],
        compiler_params=_SC_PARAMS,
        name="expert_mix_sc",
    )(v_tab, idx, w)


def _rearranged_in_proj(w_in, b_in):
    assert math.frexp(ATTN_SCALE)[0] == 0.5

    def cols(a):
        q = a[..., :Q_W] * ATTN_SCALE
        k = a[..., Q_W:Q_W + N_KV_HEADS * HEAD_DIM]
        v = a[..., Q_W + N_KV_HEADS * HEAD_DIM:Q_W + 2 * N_KV_HEADS * HEAD_DIM]
        rest = a[..., Q_W + 2 * N_KV_HEADS * HEAD_DIM:]
        dup = lambda m: jnp.concatenate(
            [m[..., hd * HEAD_DIM:(hd + 1) * HEAD_DIM] for hd in range(N_KV_HEADS) for _ in range(2)], axis=-1)
        return jnp.concatenate([q, dup(k), dup(v), rest], axis=-1)
    return cols(w_in).astype(BF16), cols(b_in)[None, :]


def kernel(x, ln_mix_g, w_in, b_in, attn_sinks, w_attn_up, w_pool_grp, pool_scale, w_pool_up, w_o,
           ln_ffn_g, w_query, sub_keys, u_experts, v_experts, ln_final_g):
    b, s, d = x.shape
    t = b * s
    depth = w_in.shape[0]
    h = x
    for l in range(depth):
        w_r, b_r = _rearranged_in_proj(w_in[l], b_in[l])
        q, kv, pz, gate = _inproj(h.reshape(t, d), ln_mix_g[l][None, :], w_r, b_r)
        o = _attention(attn_sinks[l], q.reshape(b, s, Q_W), kv.reshape(b, s, KV_COLS))
        h, xn2, xb, xt = _merge(o, gate.reshape(b, s, 2 * d), h, pz.reshape(b, s, POOL_W),
                                w_attn_up[l].astype(BF16), w_pool_grp[l].astype(BF16), pool_scale[l][None, :],
                                w_pool_up[l].astype(BF16), w_o[l].astype(BF16), ln_ffn_g[l][None, :])
        xn2, xb = xn2.reshape(t, d), xb.reshape(t, d)
        wq_t = w_query[l].T.astype(BF16)
        keys = sub_keys[l].reshape(2 * PEER_HEADS, SUBLANES, N_KEYS // SUBLANES, D_HALF)
        keys = keys[:, jnp.array(SUBLANE_BITREV)].transpose(0, 2, 1, 3)
        keys = keys.reshape(2 * PEER_HEADS, N_KEYS, D_HALF).astype(BF16)
        u_t = u_experts[l].astype(BF16).reshape(N_KEYS, N_KEYS, d).transpose(2, 1, 0).reshape(d, N_EXPERTS)
        last = l + 1 == depth
        g_out = ln_final_g[None, :]
        h2 = h.reshape(t, d)
        sc_tiles = (t // TC_EXPERT) // SC_SHARE
        t_act = t - sc_tiles * TC_EXPERT
        back = sc_tiles // SC_MIX_BACK * TC_EXPERT
        if sc_tiles:
            idx_sc, g_sc = _route(xt, wq_t, keys, t_act)
            a_sc = _expert_dots_sc(u_experts[l], idx_sc, xn2)
        idx, g, a = _route_act(xb, xt, u_t, wq_t, keys, t_act)
        if sc_tiles:
            y_sc = _expert_mix_sc(v_experts[l], idx_sc[back:], _expert_weight(a_sc[back:], g_sc[back:], a))
            idx, g, a = (jnp.concatenate([tc, sc[:back]]) for tc, sc in ((idx, idx_sc), (g, g_sc), (a, a_sc)))
        h2 = _expert_mix(idx, a, g, v_experts[l].astype(BF16), h2, g_out, last)
        if sc_tiles:
            h2 = _residual(h2, h.reshape(t, d), y_sc, g_out, last)
        h = h2.reshape(b, s, d)
    return h
```

```python
import functools
import math

import jax
import jax.numpy as jnp
import numpy as np
from jax import lax
from jax.experimental import pallas as pl
from jax.experimental.pallas import tpu as pltpu
from jax.experimental.pallas import tpu_sc as plsc

F32 = jnp.float32
BF16 = jnp.bfloat16
I32 = jnp.int32

D_MODEL = 1024
N_HEADS = 16
N_KV_HEADS = 2
GROUP = N_HEADS // N_KV_HEADS
HEAD_DIM = 64
BLOCK = 128
ATTN_SCALE = 1.0 / math.sqrt(HEAD_DIM)
NEG_INF = -1e30
POOL_WINDOWS = (2, 4, 8, 16)
POOL_GROUP = 128
POOL_W = len(POOL_WINDOWS) * POOL_GROUP
POOL_HIST = max(POOL_WINDOWS)
Q_W = N_HEADS * HEAD_DIM
PEER_HEADS = 8
N_KEYS = 128
N_EXPERTS = N_KEYS * N_KEYS
D_HALF = 128
PEER_TOPK = 16
PEER_SEL = PEER_HEADS * PEER_TOPK
KEY_BITS = N_KEYS.bit_length() - 1
TOPK_BITS = PEER_TOPK.bit_length() - 1
EPS = 1e-5

VMEM_LIMIT_BYTES = 56 * 1024 * 1024
SUBLANES = 8
LANES = 128
SUBLANE_BITREV = (0, 4, 2, 6, 1, 5, 3, 7)

C_Q = 0
C_K = C_Q + Q_W
C_V = C_K + 2 * BLOCK
C_P = C_V + 2 * BLOCK
C_G = C_P + POOL_W
C_END = C_G + 2 * D_MODEL
KV_COLS = C_P - C_K

TM_PROJ = 512
ATTN_Q_BLOCKS = 2
TS_MERGE = 512
ROUTE_SPLIT = 1
TC_ROUTE = 512
TC_EXPERT = 512
EXPERT_BLOCK = 2048
KEYS_PER_BLOCK = EXPERT_BLOCK // N_KEYS
TOKENS_PER_TRIP = 64


def _tc_params(*sem):
    return pltpu.CompilerParams(dimension_semantics=sem, vmem_limit_bytes=VMEM_LIMIT_BYTES)


def _rms_scale(x):
    return lax.rsqrt(jnp.mean(x * x, axis=-1, keepdims=True) + EPS)


def _inproj_kernel(x_ref, g_ref, w_ref, b_ref, q_ref, kv_ref, pz_ref, gate_ref):
    x = x_ref[...]
    xn = (x * _rms_scale(x) * g_ref[...]).astype(BF16)

    def proj(lo, hi):
        return jnp.dot(xn, w_ref[:, lo:hi], preferred_element_type=F32) + b_ref[:, lo:hi]

    q_ref[...] = proj(C_Q, C_K).astype(BF16)
    kv_ref[...] = proj(C_K, C_P).astype(BF16)
    pz_ref[...] = proj(C_P, C_G)
    gate_ref[...] = proj(C_G, C_END).astype(BF16)


def _inproj(x2, g, w, b):
    t = x2.shape[0]
    row = lambda i: (i, 0)
    fixed = lambda i: (0, 0)
    return pl.pallas_call(
        _inproj_kernel,
        grid=(t // TM_PROJ,),
        in_specs=[
            pl.BlockSpec((TM_PROJ, D_MODEL), row),
            pl.BlockSpec((1, D_MODEL), fixed),
            pl.BlockSpec((D_MODEL, C_END), fixed),
            pl.BlockSpec((1, C_END), fixed),
        ],
        out_specs=[
            pl.BlockSpec((TM_PROJ, Q_W), row),
            pl.BlockSpec((TM_PROJ, KV_COLS), row),
            pl.BlockSpec((TM_PROJ, POOL_W), row),
            pl.BlockSpec((TM_PROJ, 2 * D_MODEL), row),
        ],
        out_shape=[
            jax.ShapeDtypeStruct((t, Q_W), BF16),
            jax.ShapeDtypeStruct((t, KV_COLS), BF16),
            jax.ShapeDtypeStruct((t, POOL_W), F32),
            jax.ShapeDtypeStruct((t, 2 * D_MODEL), BF16),
        ],
        compiler_params=_tc_params("parallel"),
        name="inproj",
    )(x2, g, w, b)


def _attn_kernel(sink_ref, q_ref, kvc_ref, kvp_ref, o_ref, s_scr, p_scr):
    j = pl.program_id(1)
    row = lax.broadcasted_iota(I32, (BLOCK, BLOCK), 0)
    col = lax.broadcasted_iota(I32, (BLOCK, BLOCK), 1)
    low = lax.broadcasted_iota(I32, (1, BLOCK), 1) < HEAD_DIM
    zero = jnp.zeros((), BF16)
    for qb in range(ATTN_Q_BLOCKS):
        rows = slice(qb * BLOCK, (qb + 1) * BLOCK)
        kvc = kvc_ref[0, rows, :]
        kvp = kvp_ref[0] if qb == 0 else kvc_ref[0, (qb - 1) * BLOCK:qb * BLOCK, :]
        in_window = jnp.logical_and(col > row, j > 0) if qb == 0 else col > row
        mask = jnp.concatenate([in_window, col <= row], axis=1)
        for hkv in range(N_KV_HEADS):
            kc = slice(hkv * BLOCK, (hkv + 1) * BLOCK)
            k2 = jnp.concatenate([kvp[:, kc], kvc[:, kc]], axis=0)
            for pair in range(GROUP // 2):
                c0 = (hkv * (GROUP // 2) + pair) * BLOCK
                q2 = q_ref[0, rows, c0:c0 + BLOCK]
                for half in range(2):
                    qm = jnp.where(low, q2, zero) if half == 0 else jnp.where(low, zero, q2)
                    s = lax.dot_general(qm, k2, (((1,), (1,)), ((), ())), preferred_element_type=F32)
                    s_scr[hkv * GROUP + pair * 2 + half] = jnp.where(mask, s, NEG_INF)
        recip = []
        for head in range(N_HEADS):
            s = s_scr[head]
            sink = sink_ref[head]
            m = jnp.maximum(jnp.max(s, axis=-1, keepdims=True), sink)
            e = jnp.exp(s - m)
            p_scr[head] = e.astype(BF16)
            recip.append(1.0 / (jnp.sum(e, axis=-1, keepdims=True) + jnp.exp(sink - m)))
        for hkv in range(N_KV_HEADS):
            vc = slice(2 * BLOCK + hkv * BLOCK, 2 * BLOCK + (hkv + 1) * BLOCK)
            v2 = jnp.concatenate([kvp[:, vc], kvc[:, vc]], axis=0)
            v_half = (jnp.where(low, v2, zero), jnp.where(low, zero, v2))
            for pair in range(GROUP // 2):
                c0 = (hkv * (GROUP // 2) + pair) * BLOCK
                head = hkv * GROUP + pair * 2
                o2 = (jnp.dot(p_scr[head], v_half[0], preferred_element_type=F32)
                      + jnp.dot(p_scr[head + 1], v_half[1], preferred_element_type=F32))
                o2 = o2 * jnp.where(low, recip[head], recip[head + 1])
                o_ref[0, rows, c0:c0 + BLOCK] = o2.astype(BF16)


def _attention(sinks, q3, kv3):
    b, s, _ = q3.shape
    tq = ATTN_Q_BLOCKS * BLOCK
    return pl.pallas_call(
        _attn_kernel,
        grid=(b, s // tq),
        in_specs=[
            pl.BlockSpec(memory_space=pltpu.SMEM),
            pl.BlockSpec((1, tq, Q_W), lambda i, j: (i, j, 0)),
            pl.BlockSpec((1, tq, KV_COLS), lambda i, j: (i, j, 0)),
            pl.BlockSpec((1, BLOCK, KV_COLS), lambda i, j: (i, jnp.maximum(j * ATTN_Q_BLOCKS - 1, 0), 0)),
        ],
        out_specs=pl.BlockSpec((1, tq, Q_W), lambda i, j: (i, j, 0)),
        out_shape=jax.ShapeDtypeStruct((b, s, Q_W), BF16),
        scratch_shapes=[pltpu.VMEM((N_HEADS, BLOCK, 2 * BLOCK), F32), pltpu.VMEM((N_HEADS, BLOCK, 2 * BLOCK), BF16)],
        compiler_params=_tc_params("parallel", "arbitrary"),
        name="attention",
    )(sinks, q3, kv3, kv3)


def _sigmoid(x):
    return 1.0 / (1.0 + jnp.exp(-x))


def _merge_kernel(o_ref, gate_ref, x_ref, pz_ref, pzp_ref, wau_ref, wgrp_ref, scale_ref, wpu_ref,
                  wo_ref, g_ref, h_ref, xn_ref, xb_ref, xt_ref):
    j = pl.program_id(1)
    ts = pz_ref.shape[1]
    prev = jnp.where(j > 0, pzp_ref[0], 0.0)
    ext = jnp.concatenate([prev, pz_ref[0]], axis=0)
    t1 = (j * ts + 1 + lax.broadcasted_iota(I32, (ts, 1), 0)).astype(F32)
    ys = []
    for g, w in enumerate(POOL_WINDOWS):
        e = ext[:, g * POOL_GROUP:(g + 1) * POOL_GROUP]
        tsum = e
        span = 1
        while span < w:
            tsum = tsum + pltpu.roll(tsum, span, 0)
            span *= 2
        pooled = tsum[POOL_HIST:] / jnp.minimum(t1, float(w)) - e[POOL_HIST:]
        y = jnp.dot(pooled.astype(BF16), wgrp_ref[g], preferred_element_type=F32)
        ys.append((y * scale_ref[:, g * POOL_GROUP:(g + 1) * POOL_GROUP]).astype(BF16))
    y_p = jnp.dot(jnp.concatenate(ys, axis=1), wpu_ref[...], preferred_element_type=F32)
    y_a = jnp.dot(o_ref[0], wau_ref[...], preferred_element_type=F32)
    gate = gate_ref[0]
    merged = (_sigmoid(gate[:, :D_MODEL].astype(F32)) * y_a
              + _sigmoid(gate[:, D_MODEL:].astype(F32)) * y_p)
    h = x_ref[0] + jnp.dot(merged.astype(BF16), wo_ref[...], preferred_element_type=F32)
    h_ref[0] = h
    xn = h * _rms_scale(h) * g_ref[...]
    xn_ref[0] = xn
    xb_ref[0] = xn.astype(BF16)
    xt_ref[...] = xn.T.astype(BF16)


def _merge(o3, gate3, x, pz3, wau, wgrp, scale, wpu, wo, g):
    b, s, _ = x.shape
    ts = TS_MERGE
    hist_blocks = ts // POOL_HIST
    tile = lambda i, j: (i, j, 0)
    fixed2 = lambda i, j: (0, 0)
    return pl.pallas_call(
        _merge_kernel,
        grid=(b, s // ts),
        in_specs=[
            pl.BlockSpec((1, ts, Q_W), tile),
            pl.BlockSpec((1, ts, 2 * D_MODEL), tile),
            pl.BlockSpec((1, ts, D_MODEL), tile),
            pl.BlockSpec((1, ts, POOL_W), tile),
            pl.BlockSpec((1, POOL_HIST, POOL_W), lambda i, j: (i, jnp.maximum(j * hist_blocks - 1, 0), 0)),
            pl.BlockSpec((Q_W, D_MODEL), fixed2),
            pl.BlockSpec((len(POOL_WINDOWS), POOL_GROUP, POOL_GROUP), lambda i, j: (0, 0, 0)),
            pl.BlockSpec((1, POOL_W), fixed2),
            pl.BlockSpec((POOL_W, D_MODEL), fixed2),
            pl.BlockSpec((D_MODEL, D_MODEL), fixed2),
            pl.BlockSpec((1, D_MODEL), fixed2),
        ],
        out_specs=[pl.BlockSpec((1, ts, D_MODEL), tile), pl.BlockSpec((1, ts, D_MODEL), tile),
                   pl.BlockSpec((1, ts, D_MODEL), tile),
                   pl.BlockSpec((D_MODEL, ts), lambda i, j: (0, i * (s // ts) + j))],
        out_shape=[jax.ShapeDtypeStruct((b, s, D_MODEL), F32), jax.ShapeDtypeStruct((b, s, D_MODEL), F32),
                   jax.ShapeDtypeStruct((b, s, D_MODEL), BF16), jax.ShapeDtypeStruct((D_MODEL, b * s), BF16)],
        compiler_params=_tc_params("parallel", "arbitrary"),
        name="merge",
    )(o3, gate3, x, pz3, pz3, wau, wgrp, scale, wpu, wo, g)


def _rows(ref, g):
    return ref[g * SUBLANES:(g + 1) * SUBLANES, :]


def _better_left(left, right):
    (vl, pl_), (vr, pr) = left, right
    return jnp.maximum(vl, vr), jnp.where(vl >= vr, pl_, pr)


def _best_sublane(v, p, ordered):
    rows = SUBLANES
    while rows > 1:
        rows //= 2
        vl, vr, pl_, pr = v[:rows], v[rows:2 * rows], p[:rows], p[rows:2 * rows]
        take = (vl >= vr) if ordered else jnp.logical_or(vl > vr, jnp.logical_and(vl == vr, pl_ < pr))
        v, p = jnp.where(take, vl, vr), jnp.where(take, pl_, pr)
    return v, p


def _record(found, k, m, am):
    vals, poss = found
    rank = lax.broadcasted_iota(I32, vals.shape, 0)
    return jnp.where(rank == k, m, vals), jnp.where(rank == k, am, poss)


KEY_COLUMN = 4


def _rounds(one_round, init, rolled):
    if rolled:
        return lax.fori_loop(0, PEER_TOPK, one_round, init)
    carry = init
    for k in range(PEER_TOPK):
        carry = one_round(k, carry)
    return carry


def _top16_keys(problems, key_pos_ref, rolled):
    n_cols = N_KEYS // SUBLANES // KEY_COLUMN
    for val_ref, pos_ref in problems:
        for q in range(n_cols):
            lv = [_rows(val_ref, q * KEY_COLUMN + l) for l in range(KEY_COLUMN)]
            lp = [_rows(key_pos_ref, q * KEY_COLUMN + l) for l in range(KEY_COLUMN)]
            for span in range(KEY_COLUMN - 1, 0, -1):
                for i in range(span):
                    swap = lv[i + 1] > lv[i]
                    lv[i], lv[i + 1] = jnp.maximum(lv[i], lv[i + 1]), jnp.minimum(lv[i], lv[i + 1])
                    lp[i], lp[i + 1] = jnp.where(swap, lp[i + 1], lp[i]), jnp.where(swap, lp[i], lp[i + 1])
            for l in range(KEY_COLUMN):
                val_ref[(q * KEY_COLUMN + l) * SUBLANES:(q * KEY_COLUMN + l + 1) * SUBLANES, :] = lv[l]
                pos_ref[(q * KEY_COLUMN + l) * SUBLANES:(q * KEY_COLUMN + l + 1) * SUBLANES, :] = lp[l]

    def one_round(k, found):
        out = []
        for (val_ref, pos_ref), best in zip(problems, found):
            heads = [(_rows(val_ref, q * KEY_COLUMN), _rows(pos_ref, q * KEY_COLUMN)) for q in range(n_cols)]
            top = heads
            while len(top) > 1:
                top = [_better_left(top[i], top[i + 1]) for i in range(0, len(top), 2)]
            m, am = _best_sublane(*top[0], ordered=True)
            for q in range(n_cols):
                popped = heads[q][1] == am
                for l in range(KEY_COLUMN):
                    g = q * KEY_COLUMN + l
                    rows = slice(g * SUBLANES, (g + 1) * SUBLANES)
                    if l + 1 < KEY_COLUMN:
                        val_ref[rows, :] = jnp.where(popped, _rows(val_ref, g + 1), _rows(val_ref, g))
                        pos_ref[rows, :] = jnp.where(popped, _rows(pos_ref, g + 1), _rows(pos_ref, g))
                    else:
                        val_ref[rows, :] = jnp.where(popped, -jnp.inf, _rows(val_ref, g))
            out.append(_record(best, k, m, am))
        return tuple(out)

    blank = jnp.zeros((PEER_TOPK, key_pos_ref.shape[1]), F32)
    return _rounds(one_round, tuple((blank, blank) for _ in problems), rolled)


def _top16_pairs(sv0, sv1, cand_ref, extra_ref, rolled):
    c = sv0.shape[1]
    sub = lax.broadcasted_iota(I32, (SUBLANES, c), 0)
    for a in range(PEER_TOPK):
        cand_ref[a * SUBLANES:(a + 1) * SUBLANES, :] = jnp.where(
            sub < PEER_TOPK // (a + 1), sv0[a:a + 1] + sv1[:SUBLANES], -jnp.inf)
    extra_ref[...] = sv0[0:1] + sv1[SUBLANES:]
    sub_f = sub.astype(F32)
    extra_pos = sub_f + SUBLANES

    def one_round(k, carry):
        best, popped_count = carry
        head_pos = popped_count * PEER_TOPK + sub_f
        top = (_rows(cand_ref, 0), head_pos), (extra_ref[...], extra_pos)
        (vl, pl_), (vr, pr) = top
        take = jnp.logical_or(vl > vr, jnp.logical_and(vl == vr, pl_ < pr))
        m, am = _best_sublane(jnp.where(take, vl, vr), jnp.where(take, pl_, pr), ordered=False)
        popped = head_pos == am
        for a in range(PEER_TOPK):
            below = _rows(cand_ref, a + 1) if a + 1 < PEER_TOPK else -jnp.inf
            cand_ref[a * SUBLANES:(a + 1) * SUBLANES, :] = jnp.where(popped, below, _rows(cand_ref, a))
        extra_ref[...] = jnp.where(extra_pos == am, -jnp.inf, extra_ref[...])
        return _record(best, k, m, am), popped_count + jnp.where(popped, 1.0, 0.0)

    blank = jnp.zeros((PEER_TOPK, c), F32)
    (fv, fpos), _ = _rounds(one_round, ((blank, blank), jnp.zeros((SUBLANES, c), F32)), rolled)
    return fv, fpos


def _select_row(table, sel, pos16):
    out = []
    for k in range(PEER_TOPK):
        out.append(jnp.sum(jnp.where(pos16 == sel[k:k + 1], table, 0), axis=0, keepdims=True))
    return jnp.concatenate(out, axis=0)


def _route_positions(c):
    s = np.arange(SUBLANES)
    key_pos = np.concatenate([np.array(SUBLANE_BITREV)[s] * (N_KEYS // SUBLANES) + g
                              for g in range(N_KEYS // SUBLANES)])
    return jnp.asarray(np.broadcast_to(key_pos[:, None], (N_KEYS, c)), F32)


def _route_scratch(c):
    return [pltpu.VMEM((2, N_KEYS, c), F32), pltpu.VMEM((2, N_KEYS, c), F32),
            pltpu.VMEM((PEER_TOPK * SUBLANES, c), F32), pltpu.VMEM((SUBLANES, c), F32)]


def _route_head(xt, wq_ref, keys_ref, key_pos_ref, h, idx_scr, gate_scr, scratch, tokens, rolled):
    score_scr, pos_scr, cand_scr, extra_scr = scratch
    c = xt.shape[1]
    pos16 = lax.broadcasted_iota(I32, (PEER_TOPK, c), 0)
    wq = wq_ref[pl.ds(pl.multiple_of(h * 2 * D_HALF, 2 * D_HALF), 2 * D_HALF), :]
    q_t = jnp.dot(wq, xt, preferred_element_type=F32).astype(BF16)
    for half in range(2):
        score_scr[half] = jnp.dot(keys_ref[h * 2 + half], q_t[half * D_HALF:(half + 1) * D_HALF],
                                  preferred_element_type=F32)
    sv, si = [], []
    for v, p in _top16_keys([(score_scr.at[half], pos_scr.at[half]) for half in range(2)], key_pos_ref, rolled):
        sv.append(v)
        si.append(p.astype(I32))
    fv, fpos = _top16_pairs(sv[0], sv[1], cand_scr, extra_scr, rolled)
    fpos = fpos.astype(I32)
    i0 = _select_row(si[0], fpos >> TOPK_BITS, pos16)
    i1 = _select_row(si[1], fpos & (PEER_TOPK - 1), pos16)
    e = jnp.exp(fv - fv[0:1])
    rows = pl.ds(pl.multiple_of(h * PEER_TOPK, PEER_TOPK), PEER_TOPK)
    idx_scr[rows, tokens] = i0 * N_KEYS + i1
    gate_scr[rows, tokens] = e / jnp.sum(e, axis=0, keepdims=True)


def _key_rows():
    key = lax.broadcasted_iota(I32, (N_KEYS, PEER_SEL), 0)
    return key, key.astype(F32).astype(BF16)


def _split_keys(idx_rows):
    first, second = idx_rows >> KEY_BITS, idx_rows & (N_KEYS - 1)
    as_bf16 = lambda v: v.astype(F32).astype(BF16)
    return first, second, as_bf16(first), as_bf16(second)


def _route_act_kernel(xb_ref, xt_ref, ut_ref, wq_ref, keys_ref, key_pos_ref, idx_ref, gate_ref, a_ref,
                      grid_ref, idx_scr, gate_scr, *route_scratch):
    k = pl.program_id(1)
    c = xb_ref.shape[0]

    chunk_tokens = pl.ds(pl.multiple_of((k // PEER_HEADS) * TC_ROUTE, TC_ROUTE), TC_ROUTE)
    _route_head(xt_ref[:, chunk_tokens], wq_ref, keys_ref, key_pos_ref, k % PEER_HEADS,
                idx_scr, gate_scr, route_scratch, chunk_tokens, rolled=False)

    part = c // ROUTE_SPLIT
    part_tokens = pl.ds(pl.multiple_of((k % ROUTE_SPLIT) * part, part), part)
    dense = jnp.dot(xb_ref[part_tokens, :], ut_ref[...], preferred_element_type=F32)
    by_key = jnp.stack([dense[:, kk * N_KEYS:(kk + 1) * N_KEYS] for kk in range(KEYS_PER_BLOCK)], axis=0)
    second_rows = pl.ds(pl.multiple_of((k // ROUTE_SPLIT) * KEYS_PER_BLOCK, KEYS_PER_BLOCK), KEYS_PER_BLOCK)
    grid_ref[part_tokens, second_rows, :] = jnp.swapaxes(by_key, 0, 1)

    @pl.when(k == pl.num_programs(1) - 1)
    def _():
        idx_ref[...] = idx_scr[...].T
        gate_ref[...] = gate_scr[...].T
        key, key_bf = _key_rows()
        one, zero = jnp.ones((), BF16), jnp.zeros((), BF16)

        def group(gi, carry):
            t0 = pl.multiple_of(gi * TOKENS_PER_TRIP, TOKENS_PER_TRIP)
            _, second, first_bf, _ = _split_keys(idx_ref[pl.ds(t0, TOKENS_PER_TRIP), :])
            rows = []
            for g in range(TOKENS_PER_TRIP):
                pick_first = jnp.where(key_bf == first_bf[g:g + 1], one, zero)
                picked = jnp.dot(grid_ref[t0 + g].astype(BF16), pick_first,
                                 preferred_element_type=F32)
                rows.append(jnp.sum(jnp.where(key == second[g:g + 1], picked, 0.0), axis=0, keepdims=True))
            a_ref[pl.ds(t0, TOKENS_PER_TRIP), :] = jnp.concatenate(rows, axis=0)
            return carry

        lax.fori_loop(0, c // TOKENS_PER_TRIP, group, 0)


def _route_act(xb, xt, u_t, wq_t, keys, t):
    c = TC_EXPERT
    steps = ROUTE_SPLIT * N_EXPERTS // EXPERT_BLOCK
    assert steps == PEER_HEADS * (c // TC_ROUTE)
    sel = pl.BlockSpec((c, PEER_SEL), lambda i, k: (i, 0))
    once = pl.Buffered(1)
    key_pos = _route_positions(TC_ROUTE)
    return pl.pallas_call(
        _route_act_kernel,
        grid=(t // c, steps),
        in_specs=[
            pl.BlockSpec((c, D_MODEL), lambda i, k: (i, 0)),
            pl.BlockSpec((D_MODEL, c), lambda i, k: (0, i)),
            pl.BlockSpec((D_MODEL, EXPERT_BLOCK), lambda i, k: (0, k // ROUTE_SPLIT)),
            pl.BlockSpec((2 * PEER_HEADS * D_HALF, D_MODEL), lambda i, k: (0, 0), pipeline_mode=once),
            pl.BlockSpec((2 * PEER_HEADS, N_KEYS, D_HALF), lambda i, k: (0, 0, 0), pipeline_mode=once),
            pl.BlockSpec(key_pos.shape, lambda i, k: (0, 0), pipeline_mode=once),
        ],
        out_specs=[sel, sel, sel],
        out_shape=[jax.ShapeDtypeStruct((t, PEER_SEL), I32), jax.ShapeDtypeStruct((t, PEER_SEL), F32),
                   jax.ShapeDtypeStruct((t, PEER_SEL), F32)],
        scratch_shapes=[pltpu.VMEM((c, N_KEYS, N_KEYS), F32),
                        pltpu.VMEM((PEER_SEL, c), I32), pltpu.VMEM((PEER_SEL, c), F32)] + _route_scratch(TC_ROUTE),
        compiler_params=_tc_params("parallel", "arbitrary"),
        name="route_act",
    )(xb, xt, u_t, wq_t, keys, key_pos)


def _gelu(a):
    return 0.5 * a * (1.0 + lax.erf(a * math.sqrt(0.5)))


def _expert_mix_kernel(normalize, idx_ref, a_ref, gate_ref, v_ref, h_ref, g_ref, o_ref, grid_ref, w_ref, acc_ref):
    k = pl.program_id(1)
    c = idx_ref.shape[0]

    @pl.when(k == 0)
    def _():
        acc_ref[...] = h_ref[...]
        w_ref[...] = gate_ref[...] * _gelu(a_ref[...])

        _, key_bf = _key_rows()
        one, zero = jnp.ones((), BF16), jnp.zeros((), BF16)

        def group(gi, carry):
            t0 = pl.multiple_of(gi * TOKENS_PER_TRIP, TOKENS_PER_TRIP)
            _, _, first_bf, second_bf = _split_keys(idx_ref[pl.ds(t0, TOKENS_PER_TRIP), :])
            w_bf = w_ref[pl.ds(t0, TOKENS_PER_TRIP), :].astype(BF16)
            for g0 in range(0, TOKENS_PER_TRIP, SUBLANES):
                mats = []
                for g in range(g0, g0 + SUBLANES):
                    weighted = jnp.where(key_bf == first_bf[g:g + 1], w_bf[g:g + 1], zero)
                    pick_second = jnp.where(key_bf == second_bf[g:g + 1], one, zero)
                    mats.append(lax.dot_general(weighted, pick_second, (((1,), (1,)), ((), ())),
                                                preferred_element_type=F32))
                grid_ref[:, pl.ds(t0 + g0, SUBLANES), :] = jnp.swapaxes(jnp.stack(mats, axis=0), 0, 1)
            return carry

        lax.fori_loop(0, c // TOKENS_PER_TRIP, group, 0)

    dense = jnp.concatenate([grid_ref[k * KEYS_PER_BLOCK + kk] for kk in range(KEYS_PER_BLOCK)], axis=1)
    acc_ref[...] += jnp.dot(dense.astype(BF16), v_ref[...], preferred_element_type=F32)

    @pl.when(k == pl.num_programs(1) - 1)
    def _():
        hh = acc_ref[...]
        o_ref[...] = hh * _rms_scale(hh) * g_ref[...] if normalize else hh


def _expert_mix(idx, a, gate, v_tab, h2, g, normalize):
    t = idx.shape[0]
    c = TC_EXPERT
    sel = pl.BlockSpec((c, PEER_SEL), lambda i, k: (i, 0))
    tok = pl.BlockSpec((c, D_MODEL), lambda i, k: (i, 0))
    return pl.pallas_call(
        functools.partial(_expert_mix_kernel, normalize),
        grid=(t // c, N_EXPERTS // EXPERT_BLOCK),
        in_specs=[sel, sel, sel, pl.BlockSpec((EXPERT_BLOCK, D_MODEL), lambda i, k: (k, 0)), tok,
                  pl.BlockSpec((1, D_MODEL), lambda i, k: (0, 0))],
        out_specs=tok,
        out_shape=jax.ShapeDtypeStruct(h2.shape, F32),
        scratch_shapes=[pltpu.VMEM((N_KEYS, c, N_KEYS), F32), pltpu.VMEM((c, PEER_SEL), F32),
                        pltpu.VMEM((c, D_MODEL), F32)],
        compiler_params=_tc_params("parallel", "arbitrary"),
        name="expert_mix",
    )(idx, a, gate, v_tab, h2, g)


SC_CORES = 2
SC_SUBCORES = 16
SC_WORKERS = SC_CORES * SC_SUBCORES
SC_LANES = 16
SC_GATHER_ROWS = 16
SC_GATHERS_PER_TOKEN = PEER_SEL // SC_GATHER_ROWS
SC_TOKENS = 16
SC_CHUNKS = D_MODEL // SC_LANES
SC_SHARE = (17, 64)
SC_MIX_BACK = (4, 17)
TM_ELEM = 1024

_SC_PARAMS = pltpu.CompilerParams(needs_layout_passes=False)


def _route_kernel(xt_ref, wq_ref, keys_ref, key_pos_ref, idx_ref, gate_ref, idx_scr, gate_scr, *route_scratch):
    everything = pl.ds(0, xt_ref.shape[1])

    def head_body(h, carry):
        _route_head(xt_ref[...], wq_ref, keys_ref, key_pos_ref, h, idx_scr, gate_scr, route_scratch, everything,
                    rolled=True)
        return carry

    lax.fori_loop(0, PEER_HEADS, head_body, 0)
    idx_ref[...] = idx_scr[...].T
    gate_ref[...] = gate_scr[...].T


def _route(xt, wq_t, keys, first):
    t = xt.shape[1] - first
    c = TC_ROUTE
    sel = pl.BlockSpec((c, PEER_SEL), lambda i: (i, 0))
    key_pos = _route_positions(c)
    return pl.pallas_call(
        _route_kernel,
        grid=(t // c,),
        in_specs=[
            pl.BlockSpec((D_MODEL, c), lambda i: (0, i + first // c)),
            pl.BlockSpec((2 * PEER_HEADS * D_HALF, D_MODEL), lambda i: (0, 0)),
            pl.BlockSpec((2 * PEER_HEADS, N_KEYS, D_HALF), lambda i: (0, 0, 0)),
            pl.BlockSpec(key_pos.shape, lambda i: (0, 0)),
        ],
        out_specs=[sel, sel],
        out_shape=[jax.ShapeDtypeStruct((t, PEER_SEL), I32), jax.ShapeDtypeStruct((t, PEER_SEL), F32)],
        scratch_shapes=[pltpu.VMEM((PEER_SEL, c), I32), pltpu.VMEM((PEER_SEL, c), F32)] + _route_scratch(c),
        compiler_params=_tc_params("parallel"),
        name="route",
    )(xt, wq_t, keys, key_pos)


def _expert_weight_kernel(a_ref, g_ref, after_ref, w_ref):
    del after_ref
    w_ref[...] = g_ref[...] * _gelu(a_ref[...])


def _expert_weight(a, g, after):
    t = a.shape[0]
    spec = pl.BlockSpec((TM_ELEM, PEER_SEL), lambda i: (i, 0))
    return pl.pallas_call(
        _expert_weight_kernel,
        grid=(t // TM_ELEM,),
        in_specs=[spec, spec, pl.BlockSpec(memory_space=pl.ANY)],
        out_specs=spec,
        out_shape=jax.ShapeDtypeStruct((t, PEER_SEL), F32),
        compiler_params=_tc_params("parallel"),
        name="expert_weight",
    )(a, g, after)


def _residual_kernel(normalize, out_in_ref, h_ref, y_ref, g_ref, o_ref):
    del out_in_ref
    h = h_ref[...] + y_ref[...]
    o_ref[...] = h * _rms_scale(h) * g_ref[...] if normalize else h


def _residual(out, h2, y2, g, normalize):
    first = h2.shape[0] - y2.shape[0]
    tail = pl.BlockSpec((TM_ELEM, D_MODEL), lambda i: (i + first // TM_ELEM, 0))
    return pl.pallas_call(
        functools.partial(_residual_kernel, normalize),
        grid=(y2.shape[0] // TM_ELEM,),
        in_specs=[pl.BlockSpec(memory_space=pl.ANY), tail, pl.BlockSpec((TM_ELEM, D_MODEL), lambda i: (i, 0)),
                  pl.BlockSpec((1, D_MODEL), lambda i: (0, 0))],
        out_specs=tail,
        out_shape=jax.ShapeDtypeStruct(out.shape, F32),
        input_output_aliases={0: 0},
        compiler_params=_tc_params("parallel"),
        name="residual",
    )(out, h2, y2, g)


def _sc_mesh():
    return plsc.VectorSubcoreMesh(core_axis_name="c", subcore_axis_name="s")


def _sc_worker_id():
    return lax.axis_index("s") * SC_CORES + lax.axis_index("c")


def _sc_gather_pipeline(tab_hbm, idx_v, rows_v, sems, compute):
    n_gathers = idx_v.shape[0] * SC_GATHERS_PER_TOKEN

    def gather(n, slot):
        col = pl.multiple_of((n % SC_GATHERS_PER_TOKEN) * SC_GATHER_ROWS, SC_GATHER_ROWS)
        rows = idx_v[n // SC_GATHERS_PER_TOKEN, pl.ds(col, SC_GATHER_ROWS)]
        return pltpu.make_async_copy(tab_hbm.at[rows], rows_v.at[slot], sems.at[slot])

    gather(0, 0).start()

    def step(n2, carry):
        for slot in range(2):
            n = n2 * 2 + slot

            @pl.when(n + 1 < n_gathers)
            def _():
                gather(n + 1, 1 - slot).start()

            gather(n, slot).wait()
            compute(n, slot)
        return carry

    lax.fori_loop(0, n_gathers // 2, step, 0)


def _expert_dots_sc(u_tab, idx, xn2):
    t = idx.shape[0]
    first = xn2.shape[0] - t
    tok_per_worker = t // SC_WORKERS

    def body(u_hbm, idx_hbm, x_hbm, a_hbm, idx_v, x_v, rows_v, a_v, sems):
        wid = _sc_worker_id()
        lanes = lax.iota(I32, SC_LANES)

        def compute(n, slot):
            tl = n // SC_GATHERS_PER_TOKEN
            g = n % SC_GATHERS_PER_TOKEN

            def chunk(c, accs):
                off = pl.multiple_of(c * SC_LANES, SC_LANES)
                xv = x_v[tl, pl.ds(off, SC_LANES)]
                return tuple(accs[r] + rows_v[slot, r, pl.ds(off, SC_LANES)] * xv for r in range(SC_GATHER_ROWS))

            accs = lax.fori_loop(0, SC_CHUNKS, chunk,
                                 tuple(jnp.zeros((SC_LANES,), F32) for _ in range(SC_GATHER_ROWS)))
            tot = jnp.zeros((SC_LANES,), F32)
            for r in range(SC_GATHER_ROWS):
                tot = jnp.where(lanes == r, jnp.sum(accs[r]), tot)
            a_v[tl, pl.ds(pl.multiple_of(g * SC_GATHER_ROWS, SC_LANES), SC_LANES)] = tot

        def block(bi, carry):
            tok0 = wid * tok_per_worker + bi * SC_TOKENS
            pltpu.sync_copy(idx_hbm.at[pl.ds(tok0, SC_TOKENS)], idx_v)
            pltpu.sync_copy(x_hbm.at[pl.ds(first + tok0, SC_TOKENS)], x_v)
            _sc_gather_pipeline(u_hbm, idx_v, rows_v, sems, compute)
            pltpu.sync_copy(a_v, a_hbm.at[pl.ds(tok0, SC_TOKENS)])
            return carry

        lax.fori_loop(0, tok_per_worker // SC_TOKENS, block, 0)

    return pl.kernel(
        body,
        out_type=jax.ShapeDtypeStruct((t, PEER_SEL), F32),
        mesh=_sc_mesh(),
        scratch_types=[
            pltpu.VMEM((SC_TOKENS, PEER_SEL), I32),
            pltpu.VMEM((SC_TOKENS, D_MODEL), F32),
            pltpu.VMEM((2, SC_GATHER_ROWS, D_MODEL), F32),
            pltpu.VMEM((SC_TOKENS, PEER_SEL), F32),
            pltpu.SemaphoreType.DMA((2,)),
        ],
        compiler_params=_SC_PARAMS,
        name="expert_dots_sc",
    )(u_tab, idx, xn2)


def _expert_mix_sc(v_tab, idx, w):
    t = idx.shape[0]
    tok_per_worker = t // SC_WORKERS

    def body(v_hbm, idx_hbm, w_hbm, y_hbm, idx_v, w_v, rows_v, y_v, sems):
        wid = _sc_worker_id()

        def compute(n, slot):
            tl = n // SC_GATHERS_PER_TOKEN
            g = n % SC_GATHERS_PER_TOKEN
            wvec = w_v[tl, pl.ds(pl.multiple_of(g * SC_GATHER_ROWS, SC_LANES), SC_LANES)]

            for half in range(2):
                base = half * (D_MODEL // 2)

                def row(r, accs):
                    wv = jnp.take_along_axis(wvec, jnp.full((SC_LANES,), r, I32), axis=0)
                    return tuple(accs[c] + wv * rows_v[slot, r, pl.ds(base + c * SC_LANES, SC_LANES)]
                                 for c in range(SC_CHUNKS // 2))

                accs = lax.fori_loop(0, SC_GATHER_ROWS, row,
                                     tuple(jnp.zeros((SC_LANES,), F32) for _ in range(SC_CHUNKS // 2)))
                for c in range(SC_CHUNKS // 2):
                    sl = pl.ds(base + c * SC_LANES, SC_LANES)
                    y_v[tl, sl] = y_v[tl, sl] + accs[c]

        def block(bi, carry):
            tok0 = wid * tok_per_worker + bi * SC_TOKENS
            pltpu.sync_copy(idx_hbm.at[pl.ds(tok0, SC_TOKENS)], idx_v)
            pltpu.sync_copy(w_hbm.at[pl.ds(tok0, SC_TOKENS)], w_v)
            zero = jnp.zeros((SC_LANES,), F32)
            for tl in range(SC_TOKENS):
                for c in range(SC_CHUNKS):
                    y_v[tl, pl.ds(c * SC_LANES, SC_LANES)] = zero
            _sc_gather_pipeline(v_hbm, idx_v, rows_v, sems, compute)
            pltpu.sync_copy(y_v, y_hbm.at[pl.ds(tok0, SC_TOKENS)])
            return carry

        lax.fori_loop(0, tok_per_worker // SC_TOKENS, block, 0)

    return pl.kernel(
        body,
        out_type=jax.ShapeDtypeStruct((t, D_MODEL), F32),
        mesh=_sc_mesh(),
        scratch_types=[
            pltpu.VMEM((SC_TOKENS, PEER_SEL), I32),
            pltpu.VMEM((SC_TOKENS, PEER_SEL), F32),
            pltpu.VMEM((2, SC_GATHER_ROWS, D_MODEL), F32),
            pltpu.VMEM((SC_TOKENS, D_MODEL), F32),
            pltpu.SemaphoreType.DMA((2,)),
        ],
        compiler_params=_SC_PARAMS,
        name="expert_mix_sc",
    )(v_tab, idx, w)


def _rearranged_in_proj(w_in, b_in):
    assert math.frexp(ATTN_SCALE)[0] == 0.5

    def cols(a):
        q = a[..., :Q_W] * ATTN_SCALE
        k = a[..., Q_W:Q_W + N_KV_HEADS * HEAD_DIM]
        v = a[..., Q_W + N_KV_HEADS * HEAD_DIM:Q_W + 2 * N_KV_HEADS * HEAD_DIM]
        rest = a[..., Q_W + 2 * N_KV_HEADS * HEAD_DIM:]
        dup = lambda m: jnp.concatenate(
            [m[..., hd * HEAD_DIM:(hd + 1) * HEAD_DIM] for hd in range(N_KV_HEADS) for _ in range(2)], axis=-1)
        return jnp.concatenate([q, dup(k), dup(v), rest], axis=-1)
    return cols(w_in).astype(BF16), cols(b_in)[None, :]


def kernel(x, ln_mix_g, w_in, b_in, attn_sinks, w_attn_up, w_pool_grp, pool_scale, w_pool_up, w_o,
           ln_ffn_g, w_query, sub_keys, u_experts, v_experts, ln_final_g):
    b, s, d = x.shape
    t = b * s
    depth = w_in.shape[0]
    h = x
    for l in range(depth):
        w_r, b_r = _rearranged_in_proj(w_in[l], b_in[l])
        q, kv, pz, gate = _inproj(h.reshape(t, d), ln_mix_g[l][None, :], w_r, b_r)
        o = _attention(attn_sinks[l], q.reshape(b, s, Q_W), kv.reshape(b, s, KV_COLS))
        h, xn2, xb, xt = _merge(o, gate.reshape(b, s, 2 * d), h, pz.reshape(b, s, POOL_W),
                                w_attn_up[l].astype(BF16), w_pool_grp[l].astype(BF16), pool_scale[l][None, :],
                                w_pool_up[l].astype(BF16), w_o[l].astype(BF16), ln_ffn_g[l][None, :])
        xn2, xb = xn2.reshape(t, d), xb.reshape(t, d)
        wq_t = w_query[l].T.astype(BF16)
        keys = sub_keys[l].reshape(2 * PEER_HEADS, SUBLANES, N_KEYS // SUBLANES, D_HALF)
        keys = keys[:, jnp.array(SUBLANE_BITREV)].transpose(0, 2, 1, 3)
        keys = keys.reshape(2 * PEER_HEADS, N_KEYS, D_HALF).astype(BF16)
        u_t = u_experts[l].astype(BF16).reshape(N_KEYS, N_KEYS, d).transpose(2, 1, 0).reshape(d, N_EXPERTS)
        last = l + 1 == depth
        g_out = ln_final_g[None, :]
        h2 = h.reshape(t, d)
        sc_tiles = (t // TC_EXPERT) * SC_SHARE[0] // SC_SHARE[1]
        t_act = t - sc_tiles * TC_EXPERT
        back = sc_tiles * SC_MIX_BACK[0] // SC_MIX_BACK[1] * TC_EXPERT
        if sc_tiles:
            idx_sc, g_sc = _route(xt, wq_t, keys, t_act)
            a_sc = _expert_dots_sc(u_experts[l], idx_sc, xn2)
        idx, g, a = _route_act(xb, xt, u_t, wq_t, keys, t_act)
        if sc_tiles:
            y_sc = _expert_mix_sc(v_experts[l], idx_sc[back:], _expert_weight(a_sc[back:], g_sc[back:], a))
            idx, g, a = (jnp.concatenate([tc, sc[:back]]) for tc, sc in ((idx, idx_sc), (g, g_sc), (a, a_sc)))
        h2 = _expert_mix(idx, a, g, v_experts[l].astype(BF16), h2, g_out, last)
        if sc_tiles:
            h2 = _residual(h2, h.reshape(t, d), y_sc, g_out, last)
        h = h2.reshape(b, s, d)
    return h
```
